```python
import math
import jax
import jax.numpy as jnp
from jax import lax
import numpy as np

D_MODEL = 1024
BATCH = 4
SEQ = 4096
DEPTH = 4
DEC_BATCH = 32
DEC_SEQ = 4
PAST_LEN = 8192
PAGE_SIZE = 128

N_MIXERS = 3
N_HEADS = 8
A_DQK = 64
A_DV = 128
B_DK = 64
B_DV = 128
C_DK = 128
C_DV = 128
CONV_W = 4
CHUNK = 64
Q_BLOCK = 128
MEM_TOKENS = 256
MEM_HEADS = 4
MEM_DH = 128
NUM_BUCKETS = 32
MAX_DISTANCE = 128
D_FF = 2752
N_EXPERTS = 8
TOP_K = 2
D_FF_EXPERT = 3584
EXPERT_BLOCK = 128
EPS = 1e-6
ALPHA = (2 * DEPTH) ** 0.25
BETA_INIT = (8 * DEPTH) ** -0.25
N_LAYERS_A = (DEPTH + 2) // 3
N_LAYERS_B = (DEPTH + 1) // 3
N_LAYERS_C = DEPTH // 3
N_DENSE = (DEPTH + 1) // 2
N_MOE = DEPTH // 2
A_IN = 2 * N_HEADS * A_DQK + 2 * N_HEADS * A_DV + 2 * N_HEADS
B_IN = 2 * N_HEADS * 2 * B_DK + N_HEADS * B_DV
C_CONV = 2 * N_HEADS * C_DK + N_HEADS * C_DV
C_IN = C_CONV + N_HEADS * C_DV + 2 * N_HEADS
MEM_Q = MEM_HEADS * MEM_DH
MIX_OUT = N_HEADS * A_DV
OUT_IN = MIX_OUT + MEM_Q

kernel_name = 'hybrid_mlstm_diffattn_gdn_decoder_step'


def layer_norm(x, g, b):
    xf = x.astype(jnp.float32)
    mu = jnp.mean(xf, -1, keepdims=True)
    var = jnp.mean(jnp.square(xf - mu), -1, keepdims=True)
    return ((xf - mu) * lax.rsqrt(var + EPS) * g + b).astype(x.dtype)


def head_rmsnorm(h, g):
    hf = h.astype(jnp.float32)
    hf = hf * lax.rsqrt(jnp.mean(jnp.square(hf), -1, keepdims=True) + EPS)
    return hf.reshape(hf.shape[:-2] + (-1,)) * g


def l2norm(t):
    return t * lax.rsqrt(jnp.sum(jnp.square(t), -1, keepdims=True) + EPS)


def to_chunks(t, cl):
    b, l = t.shape[:2]
    t = t.reshape((b, l // cl, cl) + t.shape[2:])
    return jnp.moveaxis(t, (1, 3), (0, 2))


def from_chunks(t):
    t = jnp.moveaxis(t, (0, 2), (1, 3))
    return t.reshape((t.shape[0], -1) + t.shape[3:])


def mlstm_chunkwise(q, k, v, ig, logf, C0, n0, m0):
    L = q.shape[1]
    cl = min(CHUNK, L)
    causal = jnp.tril(jnp.ones((cl, cl), bool))

    def step(carry, xs):
        C, n, m = carry
        qc, kc, vc, igc, lfc = xs
        b = jnp.cumsum(lfc, -1)
        logD = jnp.where(causal, b[..., :, None] - b[..., None, :] + igc[..., None, :], -jnp.inf)
        inter = b + m[..., None]
        m_t = jnp.maximum(inter, jnp.max(logD, -1))
        D = jnp.exp(logD - m_t[..., None])
        w_inter = jnp.exp(inter - m_t)
        s = jnp.einsum('bhtd,bhsd->bhts', qc, kc) * D
        num = jnp.einsum('bhts,bhsv->bhtv', s, vc) + w_inter[..., None] * jnp.einsum('bhvd,bhtd->bhtv', C, qc)
        den = jnp.sum(s, -1) + w_inter * jnp.einsum('bhd,bhtd->bht', n, qc)
        h = num / jnp.maximum(jnp.abs(den), jnp.exp(-m_t))[..., None]
        w_end = D[..., -1, :]
        decay = w_inter[..., -1]
        C_new = decay[..., None, None] * C + jnp.einsum('bhs,bhsv,bhsd->bhvd', w_end, vc, kc)
        n_new = decay[..., None] * n + jnp.einsum('bhs,bhsd->bhd', w_end, kc)
        return (C_new, n_new, m_t[..., -1]), h

    xs = tuple(to_chunks(t, cl) for t in (q, k, v, ig, logf))
    state, h = lax.scan(step, (C0, n0, m0), xs)
    return from_chunks(h), state


def mlstm_mixer(u, b_gate, norm_g, C0, n0, m0):
    b, L = u.shape[:2]
    hq, hv = N_HEADS * A_DQK, N_HEADS * A_DV
    uf = u.astype(jnp.float32)
    q = uf[..., :hq].reshape(b, L, N_HEADS, A_DQK) * A_DQK ** -0.5
    k = uf[..., hq:2 * hq].reshape(b, L, N_HEADS, A_DQK)
    v = uf[..., 2 * hq:2 * hq + hv].reshape(b, L, N_HEADS, A_DV)
    o_gate = uf[..., 2 * hq + hv:2 * hq + 2 * hv]
    ig = uf[..., 2 * hq + 2 * hv:2 * hq + 2 * hv + N_HEADS] + b_gate[0]
    logf = jax.nn.log_sigmoid(uf[..., 2 * hq + 2 * hv + N_HEADS:] + b_gate[1])
    h, state = mlstm_chunkwise(q, k, v, ig, logf, C0.astype(jnp.float32), n0.astype(jnp.float32), m0.astype(jnp.float32))
    out = jax.nn.sigmoid(o_gate) * head_rmsnorm(h, norm_g)
    return out.astype(u.dtype), state


def t5_bucket(rel):
    n = jnp.maximum(rel, 0)
    max_exact = NUM_BUCKETS // 2
    nf = jnp.maximum(n, 1).astype(jnp.float32)
    large = max_exact + (jnp.log(nf / max_exact) / math.log(MAX_DISTANCE / max_exact) * (NUM_BUCKETS - max_exact)).astype(jnp.int32)
    return jnp.where(n < max_exact, n, jnp.minimum(large, NUM_BUCKETS - 1))


def diff_split(u):
    b, L = u.shape[:2]
    hq = N_HEADS * 2 * B_DK
    q = u[..., :hq].reshape(b, L, N_HEADS, 2, B_DK)
    k = u[..., hq:2 * hq].reshape(b, L, N_HEADS, 2, B_DK)
    v = u[..., 2 * hq:].reshape(b, L, N_HEADS, B_DV)
    return q, k, v


def diff_lambda(lw, lam_init):
    lw = lw.astype(jnp.float32)
    return jnp.exp(jnp.sum(lw[0] * lw[1])) - jnp.exp(jnp.sum(lw[2] * lw[3])) + lam_init


def diff_attention_core(q, k, v, q_pos, k_pos, rel_bias, lam):
    s = jnp.einsum('bqhcd,bkhcd->bhcqk', q, k).astype(jnp.float32) * B_DK ** -0.5
    bias = jnp.transpose(rel_bias[t5_bucket(q_pos[:, None] - k_pos[None, :])], (2, 0, 1))
    s = s + bias[None, :, None].astype(jnp.float32)
    s = jnp.where(k_pos[None, :] <= q_pos[:, None], s, -jnp.inf)
    p = jax.nn.softmax(s, axis=-1)
    a = p[:, :, 0] - lam * p[:, :, 1]
    return jnp.einsum('bhqk,bkhv->bqhv', a, v.astype(jnp.float32))


def diff_attn_prompt(u, lam_w, norm_g, rel_bias, lam_init):
    b, L = u.shape[:2]
    q, k, v = diff_split(u)
    lam = diff_lambda(lam_w, lam_init)
    qb = min(Q_BLOCK, L)
    nb = L // qb
    pos = jnp.arange(L)
    q_blocks = jnp.moveaxis(q.reshape((b, nb, qb) + q.shape[2:]), 1, 0)
    o = lax.map(lambda xs: diff_attention_core(xs[0], k, v, xs[1], pos, rel_bias, lam), (q_blocks, pos.reshape(nb, qb)))
    o = jnp.moveaxis(o, 0, 1).reshape(b, L, N_HEADS, B_DV)
    out = head_rmsnorm(o, norm_g) * (1.0 - lam_init)
    return out.astype(u.dtype), (k.reshape(b, L, N_HEADS, 2 * B_DK), v)


def diff_attn_sample(u, cache_k, cache_v, page_table, lam_w, norm_g, rel_bias, lam_init):
    b, L = u.shape[:2]
    q, k, v = diff_split(u)
    lam = diff_lambda(lam_w, lam_init)
    past_k = cache_k[page_table].reshape(b, PAST_LEN, N_HEADS, 2, B_DK).astype(k.dtype)
    past_v = cache_v[page_table].reshape(b, PAST_LEN, N_HEADS, B_DV).astype(v.dtype)
    k_all = jnp.concatenate([past_k, k], 1)
    v_all = jnp.concatenate([past_v, v], 1)
    q_pos = PAST_LEN + jnp.arange(L)
    k_pos = jnp.arange(PAST_LEN + L)
    o = diff_attention_core(q, k_all, v_all, q_pos, k_pos, rel_bias, lam)
    out = head_rmsnorm(o, norm_g) * (1.0 - lam_init)
    return out.astype(u.dtype), (k.reshape(b, L, N_HEADS, 2 * B_DK), v)


def gdn_chunkwise(q, k, v, beta, g, S0):
    L = q.shape[1]
    cl = min(CHUNK, L)
    causal = jnp.tril(jnp.ones((cl, cl), bool))
    strict = jnp.tril(jnp.ones((cl, cl), bool), -1)
    eye = jnp.eye(cl, dtype=jnp.float32)

    def step(S, xs):
        qc, kc, vc, bc, gc = xs
        gam = jnp.cumsum(gc, -1)
        dec = jnp.where(causal, jnp.exp(jnp.where(causal, gam[..., :, None] - gam[..., None, :], 0.0)), 0.0)
        kk = jnp.einsum('bhtd,bhsd->bhts', kc, kc)
        a_mat = jnp.where(strict, bc[..., :, None] * kk * dec, 0.0) + eye
        rhs = jnp.concatenate([bc[..., None] * vc, (bc * jnp.exp(gam))[..., None] * kc], -1)
        sol = lax.linalg.triangular_solve(a_mat, rhs, left_side=True, lower=True, unit_diagonal=True)
        delta = sol[..., :C_DV] - jnp.einsum('bhtd,bhvd->bhtv', sol[..., C_DV:], S)
        qk = jnp.einsum('bhtd,bhsd->bhts', qc, kc) * dec
        o = jnp.exp(gam)[..., None] * jnp.einsum('bhvd,bhtd->bhtv', S, qc) + jnp.einsum('bhts,bhsv->bhtv', qk, delta)
        g_end = gam[..., -1]
        w_end = jnp.exp(g_end[..., None] - gam)
        S_new = jnp.exp(g_end)[..., None, None] * S + jnp.einsum('bhs,bhsv,bhsd->bhvd', w_end, delta, kc)
        return S_new, o

    xs = tuple(to_chunks(t, cl) for t in (q, k, v, beta, g))
    S, o = lax.scan(step, S0, xs)
    return from_chunks(o), S


def gdn_mixer(u, conv_w, a_log, dt_bias, norm_g, S0, conv0):
    b, L = u.shape[:2]
    hk, hv = N_HEADS * C_DK, N_HEADS * C_DV
    qkv = u[..., :C_CONV]
    z = u[..., C_CONV:C_CONV + hv]
    beta_pre = u[..., C_CONV + hv:C_CONV + hv + N_HEADS]
    a_pre = u[..., C_CONV + hv + N_HEADS:]
    xp = jnp.concatenate([conv0.astype(qkv.dtype), qkv], 1)
    conv = sum(conv_w[j] * xp[:, j:j + L] for j in range(CONV_W))
    conv = jax.nn.silu(conv.astype(jnp.float32))
    new_conv = xp[:, -(CONV_W - 1):]
    q = l2norm(conv[..., :hk].reshape(b, L, N_HEADS, C_DK)) * C_DK ** -0.5
    k = l2norm(conv[..., hk:2 * hk].reshape(b, L, N_HEADS, C_DK))
    v = conv[..., 2 * hk:].reshape(b, L, N_HEADS, C_DV)
    beta = jax.nn.sigmoid(beta_pre.astype(jnp.float32))
    g = -jnp.exp(a_log.astype(jnp.float32)) * jax.nn.softplus(a_pre.astype(jnp.float32) + dt_bias)
    o, S = gdn_chunkwise(q, k, v, beta, g, S0.astype(jnp.float32))
    out = head_rmsnorm(o, norm_g) * jax.nn.silu(z.astype(jnp.float32))
    return out.astype(u.dtype), (S, new_conv)


def memory_attention(mq, mk, mv):
    b, L = mq.shape[:2]
    q = mq.reshape(b, L, MEM_HEADS, MEM_DH)
    s = jnp.einsum('bqhd,bmhd->bhqm', q, mk.astype(q.dtype)).astype(jnp.float32) * MEM_DH ** -0.5
    p = jax.nn.softmax(s, axis=-1)
    o = jnp.einsum('bhqm,bmhd->bqhd', p, mv.astype(jnp.float32))
    return o.reshape(b, L, MEM_Q).astype(mq.dtype)


def swiglu(x, w_gu, w_down):
    g, up = jnp.split(x @ w_gu, 2, axis=-1)
    return (jax.nn.silu(g) * up) @ w_down


def moe_swiglu(x, w_router, w_gu, w_down):
    shape = x.shape
    xt = x.reshape(-1, D_MODEL)
    n_assign = xt.shape[0] * TOP_K
    logits = xt.astype(jnp.float32) @ w_router.astype(jnp.float32)
    top_val, top_idx = lax.top_k(logits, TOP_K)
    gate = jax.nn.softmax(top_val, axis=-1).reshape(-1)
    flat_e = top_idx.reshape(-1)
    order = jnp.argsort(flat_e)
    e_sorted = flat_e[order]
    tok_sorted = order // TOP_K
    counts = jnp.zeros((N_EXPERTS,), jnp.int32).at[flat_e].add(1)
    padded = (counts + EXPERT_BLOCK - 1) // EXPERT_BLOCK * EXPERT_BLOCK
    p_end = jnp.cumsum(padded)
    dest = (p_end - padded)[e_sorted] + jnp.arange(n_assign) - (jnp.cumsum(counts) - counts)[e_sorted]
    n_blocks = -(-(n_assign + N_EXPERTS * (EXPERT_BLOCK - 1)) // EXPERT_BLOCK)
    buf = jnp.zeros((n_blocks * EXPERT_BLOCK, D_MODEL), x.dtype).at[dest].set(xt[tok_sorted])
    blk_e = jnp.minimum(jnp.searchsorted(p_end, jnp.arange(n_blocks) * EXPERT_BLOCK, side='right'), N_EXPERTS - 1)
    yb = lax.map(lambda a: swiglu(a[0], w_gu[a[1]], w_down[a[1]]), (buf.reshape(n_blocks, EXPERT_BLOCK, D_MODEL), blk_e))
    y_assign = (yb.reshape(-1, D_MODEL)[dest] * gate[order][:, None]).astype(x.dtype)
    y = jnp.zeros_like(xt).at[tok_sorted].add(y_assign)
    return y.reshape(shape)


def finish_layer(x, mix, mq, mem_k, mem_v, w_out_l, g, b, ffn):
    y = jnp.concatenate([mix, memory_attention(mq, mem_k, mem_v)], -1) @ w_out_l
    x = layer_norm(ALPHA * x + y, g[0], b[0])
    return layer_norm(ALPHA * x + ffn(x), g[1], b[1])


def setup_inputs(seed: int = 0) -> dict:
    key = jax.random.key(seed)
    keys = iter(jax.random.split(key, 48))

    def nrm(shape, scale=1.0):
        return jax.random.normal(next(keys), shape, jnp.float32) * scale

    n_pages = PAST_LEN // PAGE_SIZE
    n_phys = (DEC_BATCH * n_pages * 5) // 4
    page_table = jax.random.permutation(next(keys), n_phys)[:DEC_BATCH * n_pages].reshape(DEC_BATCH, n_pages).astype(jnp.int32)
    fan = D_MODEL ** -0.5
    b_gate_a = jnp.stack([nrm((N_LAYERS_A, N_HEADS), 0.1) - 2.0,
                          jnp.linspace(3.0, 6.0, N_HEADS)[None, :] + nrm((N_LAYERS_A, N_HEADS), 0.1)], axis=1)
    dt = jnp.exp(jax.random.uniform(next(keys), (N_LAYERS_C, N_HEADS), jnp.float32, math.log(1e-3), math.log(1e-1)))
    a_log_c = jnp.log(jax.random.uniform(next(keys), (N_LAYERS_C, N_HEADS), jnp.float32, 1.0, 16.0))
    return {
        'x_prompt': nrm((BATCH, SEQ, D_MODEL)),
        'x_sample': nrm((DEC_BATCH, DEC_SEQ, D_MODEL)),
        'mem_prompt': nrm((BATCH, MEM_TOKENS, D_MODEL)),
        'page_table': page_table,
        'cache_diff_k': nrm((N_LAYERS_B, n_phys, PAGE_SIZE, N_HEADS, 2 * B_DK)),
        'cache_diff_v': nrm((N_LAYERS_B, n_phys, PAGE_SIZE, N_HEADS, B_DV)),
        'cache_mem_k': nrm((DEPTH, DEC_BATCH, MEM_TOKENS, MEM_HEADS, MEM_DH)),
        'cache_mem_v': nrm((DEPTH, DEC_BATCH, MEM_TOKENS, MEM_HEADS, MEM_DH)),
        'state_mlstm_C': nrm((N_LAYERS_A, DEC_BATCH, N_HEADS, A_DV, A_DQK), 0.5),
        'state_mlstm_n': nrm((N_LAYERS_A, DEC_BATCH, N_HEADS, A_DQK), 0.5),
        'state_mlstm_m': nrm((N_LAYERS_A, DEC_BATCH, N_HEADS), 0.5) + 3.0,
        'state_gdn_S': nrm((N_LAYERS_C, DEC_BATCH, N_HEADS, C_DV, C_DK), 0.1),
        'state_gdn_conv': nrm((N_LAYERS_C, DEC_BATCH, CONV_W - 1, C_CONV)),
        'rel_bias': nrm((NUM_BUCKETS, N_HEADS), 0.3),
        'w_in_a': nrm((N_LAYERS_A, D_MODEL, A_IN + MEM_Q), fan),
        'b_gate_a': b_gate_a,
        'norm_a': 1.0 + nrm((N_LAYERS_A, N_HEADS * A_DV), 0.02),
        'w_in_b': nrm((N_LAYERS_B, D_MODEL, B_IN + MEM_Q), fan),
        'lambda_b': nrm((N_LAYERS_B, 4, B_DK), 0.1),
        'norm_b': 1.0 + nrm((N_LAYERS_B, N_HEADS * B_DV), 0.02),
        'w_in_c': nrm((N_LAYERS_C, D_MODEL, C_IN + MEM_Q), fan),
        'conv_c': nrm((N_LAYERS_C, CONV_W, C_CONV), CONV_W ** -0.5),
        'a_log_c': a_log_c,
        'dt_bias_c': dt + jnp.log(-jnp.expm1(-dt)),
        'norm_c': 1.0 + nrm((N_LAYERS_C, N_HEADS * C_DV), 0.02),
        'w_mem_kv': nrm((DEPTH, D_MODEL, 2 * MEM_Q), fan),
        'w_out': nrm((DEPTH, OUT_IN, D_MODEL), OUT_IN ** -0.5 * BETA_INIT),
        'ln_g': 1.0 + nrm((DEPTH, 2, D_MODEL), 0.02),
        'ln_b': nrm((DEPTH, 2, D_MODEL), 0.02),
        'w_ffn_gu': nrm((N_DENSE, D_MODEL, 2 * D_FF), fan),
        'w_ffn_down': nrm((N_DENSE, D_FF, D_MODEL), D_FF ** -0.5 * BETA_INIT),
        'w_router': nrm((N_MOE, D_MODEL, N_EXPERTS), fan),
        'w_exp_gu': nrm((N_MOE, N_EXPERTS, D_MODEL, 2 * D_FF_EXPERT), fan),
        'w_exp_down': nrm((N_MOE, N_EXPERTS, D_FF_EXPERT, D_MODEL), D_FF_EXPERT ** -0.5 * BETA_INIT),
    }


def reference(x_prompt, x_sample, mem_prompt, page_table, cache_diff_k, cache_diff_v, cache_mem_k, cache_mem_v,
              state_mlstm_C, state_mlstm_n, state_mlstm_m, state_gdn_S, state_gdn_conv, rel_bias,
              w_in_a, b_gate_a, norm_a, w_in_b, lambda_b, norm_b, w_in_c, conv_c, a_log_c, dt_bias_c, norm_c,
              w_mem_kv, w_out, ln_g, ln_b, w_ffn_gu, w_ffn_down, w_router, w_exp_gu, w_exp_down):
    b_p = x_prompt.shape[0]
    x_p, x_s = x_prompt, x_sample
    mlstm_p, mlstm_s, diff_p, diff_s, gdn_p, gdn_s, mem_k_new, mem_v_new = [], [], [], [], [], [], [], []
    for i in range(DEPTH):
        kind, j = i % N_MIXERS, i // N_MIXERS
        mem_kv = (mem_prompt @ w_mem_kv[i]).reshape(b_p, -1, 2, MEM_HEADS, MEM_DH)
        mk_p, mv_p = mem_kv[:, :, 0], mem_kv[:, :, 1]
        mem_k_new.append(mk_p)
        mem_v_new.append(mv_p)
        if kind == 0:
            u_p, u_s = x_p @ w_in_a[j], x_s @ w_in_a[j]
            C0 = jnp.zeros((b_p, N_HEADS, A_DV, A_DQK), jnp.float32)
            n0 = jnp.zeros((b_p, N_HEADS, A_DQK), jnp.float32)
            m0 = jnp.zeros((b_p, N_HEADS), jnp.float32)
            mix_p, st_p = mlstm_mixer(u_p[..., :A_IN], b_gate_a[j], norm_a[j], C0, n0, m0)
            mix_s, st_s = mlstm_mixer(u_s[..., :A_IN], b_gate_a[j], norm_a[j], state_mlstm_C[j], state_mlstm_n[j], state_mlstm_m[j])
            mlstm_p.append(st_p)
            mlstm_s.append(st_s)
        elif kind == 1:
            lam_init = 0.8 - 0.6 * math.exp(-0.3 * i)
            u_p, u_s = x_p @ w_in_b[j], x_s @ w_in_b[j]
            mix_p, st_p = diff_attn_prompt(u_p[..., :B_IN], lambda_b[j], norm_b[j], rel_bias, lam_init)
            mix_s, st_s = diff_attn_sample(u_s[..., :B_IN], cache_diff_k[j], cache_diff_v[j], page_table,
                                           lambda_b[j], norm_b[j], rel_bias, lam_init)
            diff_p.append(st_p)
            diff_s.append(st_s)
        else:
            u_p, u_s = x_p @ w_in_c[j], x_s @ w_in_c[j]
            S0 = jnp.zeros((b_p, N_HEADS, C_DV, C_DK), jnp.float32)
            conv0 = jnp.zeros((b_p, CONV_W - 1, C_CONV), x_p.dtype)
            mix_p, st_p = gdn_mixer(u_p[..., :C_IN], conv_c[j], a_log_c[j], dt_bias_c[j], norm_c[j], S0, conv0)
            mix_s, st_s = gdn_mixer(u_s[..., :C_IN], conv_c[j], a_log_c[j], dt_bias_c[j], norm_c[j], state_gdn_S[j], state_gdn_conv[j])
            gdn_p.append(st_p)
            gdn_s.append(st_s)
        k_f = i // 2
        if i % 2 == 0:
            ffn = lambda t: swiglu(t, w_ffn_gu[k_f], w_ffn_down[k_f])
        else:
            ffn = lambda t: moe_swiglu(t, w_router[k_f], w_exp_gu[k_f], w_exp_down[k_f])
        x_p = finish_layer(x_p, mix_p, u_p[..., -MEM_Q:], mk_p, mv_p, w_out[i], ln_g[i], ln_b[i], ffn)
        x_s = finish_layer(x_s, mix_s, u_s[..., -MEM_Q:], cache_mem_k[i], cache_mem_v[i], w_out[i], ln_g[i], ln_b[i], ffn)
    mlstm_C_p = jnp.stack([s[0] for s in mlstm_p])
    mlstm_n_p = jnp.stack([s[1] for s in mlstm_p])
    mlstm_m_p = jnp.stack([s[2] for s in mlstm_p])
    mlstm_C_s = jnp.stack([s[0] for s in mlstm_s])
    mlstm_n_s = jnp.stack([s[1] for s in mlstm_s])
    mlstm_m_s = jnp.stack([s[2] for s in mlstm_s])
    diff_k_p = jnp.stack([s[0] for s in diff_p])
    diff_v_p = jnp.stack([s[1] for s in diff_p])
    diff_k_s = jnp.stack([s[0] for s in diff_s])
    diff_v_s = jnp.stack([s[1] for s in diff_s])
    gdn_S_p = jnp.stack([s[0] for s in gdn_p])
    gdn_conv_p = jnp.stack([s[1] for s in gdn_p])
    gdn_S_s = jnp.stack([s[0] for s in gdn_s])
    gdn_conv_s = jnp.stack([s[1] for s in gdn_s])
    mem_k_p = jnp.stack(mem_k_new)
    mem_v_p = jnp.stack(mem_v_new)
    return (x_p, x_s, mlstm_C_p, mlstm_n_p, mlstm_m_p, mlstm_C_s, mlstm_n_s, mlstm_m_s,
            diff_k_p, diff_v_p, diff_k_s, diff_v_s, gdn_S_p, gdn_conv_p, gdn_S_s, gdn_conv_s, mem_k_p, mem_v_p)
```

```python
import functools
import math

import jax
import jax.numpy as jnp
from jax import lax
from jax.experimental import pallas as pl
from jax.experimental.pallas import tpu as pltpu

F32 = jnp.float32
BF16 = jnp.bfloat16
HI = lax.Precision.HIGHEST
NT = (((1,), (1,)), ((), ()))
TN = (((0,), (0,)), ((), ()))

D = 1024
H = 8
DEPTH = 4
SEQ = 4096
BATCH = 4
DEC_BATCH = 32
DEC_SEQ = 4
DEC_PAD = 8
PAST_LEN = 8192
PAGE = 128
N_PAGES = PAST_LEN // PAGE
MEM_TOKENS = 256
MEM_Q = 512
MEM_HEADS = 4
CHUNK = 64
N_EXPERTS = 8
D_FF = 2752
D_FF_PAD = 2816
D_FF_EXPERT = 3584
EPS = 1e-6
ALPHA = (2 * DEPTH) ** 0.25
NEG = -1e30
LANES = 128
VMEM_LIMIT = 56 * 1024 * 1024


def _cp(n_axes, vmem=VMEM_LIMIT):
    return pltpu.CompilerParams(dimension_semantics=("arbitrary",) * n_axes, vmem_limit_bytes=vmem)


def _dot(a, b):
    return jnp.dot(a.astype(BF16), b.astype(BF16), preferred_element_type=F32)


def _dot_nt(a, b):
    return lax.dot_general(a.astype(BF16), b.astype(BF16), NT, preferred_element_type=F32)


def _dot_tn(a, b):
    return lax.dot_general(a.astype(BF16), b.astype(BF16), TN, preferred_element_type=F32)


def _dot_hi(a, b):
    return jnp.dot(a, b, preferred_element_type=F32, precision=HI)


def _sigmoid(x):
    return 1.0 / (1.0 + jnp.exp(-x))


def _silu(x):
    return x * _sigmoid(x)


def _softplus(x):
    return jnp.maximum(x, 0.0) + jnp.log1p(jnp.exp(-jnp.abs(x)))


def _log_sigmoid(x):
    return -_softplus(-x)


def _layer_norm(v, g, b):
    mu = jnp.mean(v, axis=-1, keepdims=True)
    d = v - mu
    var = jnp.mean(d * d, axis=-1, keepdims=True)
    return d * lax.rsqrt(var + EPS) * g + b


def _rms(h):
    return h * lax.rsqrt(jnp.mean(h * h, axis=-1, keepdims=True) + EPS)


def _mm_kernel(x_ref, w_ref, o_ref, *, hi):
    if hi:
        o_ref[...] = _dot_hi(x_ref[...], w_ref[...])
    else:
        o_ref[...] = _dot(x_ref[...], w_ref[...])


def matmul(x, w, tm, tn, hi=False):
    m, k = x.shape
    n = w.shape[1]
    return pl.pallas_call(
        functools.partial(_mm_kernel, hi=hi),
        grid=(m // tm, n // tn),
        in_specs=[pl.BlockSpec((tm, k), lambda i, j: (i, 0)),
                  pl.BlockSpec((k, tn), lambda i, j: (0, j))],
        out_specs=pl.BlockSpec((tm, tn), lambda i, j: (i, j)),
        out_shape=jax.ShapeDtypeStruct((m, n), F32),
        compiler_params=_cp(2),
        name="matmul",
    )(x, w)


def _outproj_kernel(mix_ref, mo_ref, x_ref, w1_ref, w2_ref, g_ref, b_ref, o_ref):
    y = _dot(mix_ref[...], w1_ref[...]) + _dot(mo_ref[...], w2_ref[...])
    o_ref[...] = _layer_norm(ALPHA * x_ref[...] + y, g_ref[...], b_ref[...])


def outproj_ln(mix, mo, x, w_out, g, b, tm):
    m = x.shape[0]
    w1 = w_out[:D].astype(BF16)
    w2 = w_out[D:].astype(BF16)
    return pl.pallas_call(
        _outproj_kernel,
        grid=(m // tm,),
        in_specs=[pl.BlockSpec((tm, D), lambda i: (i, 0)),
                  pl.BlockSpec((tm, MEM_Q), lambda i: (i, 0)),
                  pl.BlockSpec((tm, D), lambda i: (i, 0)),
                  pl.BlockSpec((D, D), lambda i: (0, 0)),
                  pl.BlockSpec((MEM_Q, D), lambda i: (0, 0)),
                  pl.BlockSpec((1, D), lambda i: (0, 0)),
                  pl.BlockSpec((1, D), lambda i: (0, 0))],
        out_specs=pl.BlockSpec((tm, D), lambda i: (i, 0)),
        out_shape=jax.ShapeDtypeStruct((m, D), F32),
        compiler_params=_cp(1),
        name="outproj_ln",
    )(mix, mo, x, w1, w2, g.reshape(1, D), b.reshape(1, D))


def _ffn_kernel(x_ref, wg_ref, wu_ref, wd_ref, g_ref, b_ref, o_ref, xb_scr, acc_scr):
    j = pl.program_id(1)

    @pl.when(j == 0)
    def _():
        xb_scr[...] = x_ref[...].astype(BF16)
        acc_scr[...] = jnp.zeros_like(acc_scr)

    xb = xb_scr[...]
    gate = jnp.dot(xb, wg_ref[...], preferred_element_type=F32)
    up = jnp.dot(xb, wu_ref[...], preferred_element_type=F32)
    acc_scr[...] += _dot(_silu(gate) * up, wd_ref[...])

    @pl.when(j == pl.num_programs(1) - 1)
    def _():
        o_ref[...] = _layer_norm(ALPHA * x_ref[...] + acc_scr[...], g_ref[...], b_ref[...])


def ffn_ln(x, w_gu, w_down, g, b, tm, tf=256):
    m = x.shape[0]
    pad = D_FF_PAD - D_FF
    wg = jnp.pad(w_gu[:, :D_FF], ((0, 0), (0, pad))).astype(BF16)
    wu = jnp.pad(w_gu[:, D_FF:], ((0, 0), (0, pad))).astype(BF16)
    wd = jnp.pad(w_down, ((0, pad), (0, 0))).astype(BF16)
    return pl.pallas_call(
        _ffn_kernel,
        grid=(m // tm, D_FF_PAD // tf),
        in_specs=[pl.BlockSpec((tm, D), lambda i, j: (i, 0)),
                  pl.BlockSpec((D, tf), lambda i, j: (0, j)),
                  pl.BlockSpec((D, tf), lambda i, j: (0, j)),
                  pl.BlockSpec((tf, D), lambda i, j: (j, 0)),
                  pl.BlockSpec((1, D), lambda i, j: (0, 0)),
                  pl.BlockSpec((1, D), lambda i, j: (0, 0))],
        out_specs=pl.BlockSpec((tm, D), lambda i, j: (i, 0)),
        out_shape=jax.ShapeDtypeStruct((m, D), F32),
        scratch_shapes=[pltpu.VMEM((tm, D), BF16), pltpu.VMEM((tm, D), F32)],
        compiler_params=_cp(2),
        name="ffn_ln",
    )(x, wg, wu, wd, g.reshape(1, D), b.reshape(1, D))


def _memattn_kernel(q_ref, k_ref, v_ref, o_ref):
    q = q_ref[...]
    k = k_ref[...]
    v = v_ref[...]
    for h in range(MEM_HEADS):
        sl = slice(h * LANES, (h + 1) * LANES)
        s = _dot_nt(q[:, sl], k[:, sl]) * LANES ** -0.5
        s = s - jnp.max(s, axis=-1, keepdims=True)
        p = jnp.exp(s)
        p = p / jnp.sum(p, axis=-1, keepdims=True)
        o_ref[:, sl] = _dot(p, v[:, sl])


def mem_attention(u, q_col, mem_k, k_col, mem_v, v_col, tq):
    b, l, _ = u.shape
    return pl.pallas_call(
        _memattn_kernel,
        grid=(b, l // tq),
        in_specs=[pl.BlockSpec((None, tq, MEM_Q), lambda bi, i: (bi, i, q_col)),
                  pl.BlockSpec((None, MEM_TOKENS, MEM_Q), lambda bi, i: (bi, 0, k_col)),
                  pl.BlockSpec((None, MEM_TOKENS, MEM_Q), lambda bi, i: (bi, 0, v_col))],
        out_specs=pl.BlockSpec((None, tq, MEM_Q), lambda bi, i: (bi, i, 0)),
        out_shape=jax.ShapeDtypeStruct((b, l, MEM_Q), F32),
        compiler_params=_cp(2),
        name="mem_attention",
    )(u, mem_k, mem_v)


def _mlstm_kernel(qk_ref, v_ref, og_ref, gc_ref, gt_ref, bgc_ref, bgr_ref, ng_ref, c0_ref, n0_ref, m0_ref,
                  mix_ref, co_ref, no_ref, mo_ref, c_scr, n_scr, m_scr, *, T, nc, n_valid, sps):
    s = pl.program_id(0)

    @pl.when(s % sps == 0)
    def _():
        c_scr[...] = c0_ref[...]
        n_scr[...] = n0_ref[...]
        m_scr[...] = m0_ref[...]

    row = lax.broadcasted_iota(jnp.int32, (T, T), 0)
    col = lax.broadcasted_iota(jnp.int32, (T, T), 1)
    causal = col <= row
    tril = causal.astype(F32)
    triu = (row <= col).astype(F32)

    def chunk(c, carry):
        r0 = pl.multiple_of(c * T, T)
        rows = pl.ds(r0, T)
        g = gc_ref[rows, :] + bgc_ref[...]
        lfc = _log_sigmoid(g)
        gt = gt_ref[c] + bgr_ref[...]
        igr_all = gt[0:H, :]
        lfr = _log_sigmoid(gt[H:2 * H, :])
        if n_valid < T:
            rid = lax.broadcasted_iota(jnp.int32, (T, LANES), 0)
            g = jnp.where(rid < n_valid, g, NEG)
            lfc = jnp.where(rid < n_valid, lfc, 0.0)
            cid = lax.broadcasted_iota(jnp.int32, (H, T), 1)
            igr_all = jnp.where(cid < n_valid, igr_all, NEG)
            lfr = jnp.where(cid < n_valid, lfr, 0.0)
        bcol = _dot_hi(tril, lfc)
        brow = _dot_hi(lfr, triu)
        qk = qk_ref[rows, :]
        vv = v_ref[rows, :]
        og = og_ref[rows, :]
        for h in range(H):
            bc = bcol[:, H + h:H + h + 1]
            igc = g[:, h:h + 1]
            br = brow[h:h + 1, :]
            igr = igr_all[h:h + 1, :]
            m_prev = m_scr[:, h:h + 1]
            log_d = jnp.where(causal, bc - br + igr, -jnp.inf)
            inter = bc + m_prev
            m_t = jnp.maximum(inter, jnp.max(log_d, axis=-1, keepdims=True))
            dmat = jnp.exp(log_d - m_t)
            w_inter = jnp.exp(inter - m_t)
            q = qk[:, h * 64:(h + 1) * 64] * 0.125
            k = qk[:, 512 + h * 64:512 + (h + 1) * 64]
            v = vv[:, h * LANES:(h + 1) * LANES]
            c_h = c_scr[h]
            n_h = n_scr[h:h + 1, :]
            sc = _dot_nt(q, k) * dmat
            num = _dot(sc, v) + w_inter * _dot_nt(q, c_h)
            den = jnp.sum(sc, axis=-1, keepdims=True) + w_inter * jnp.sum(q * n_h, axis=-1, keepdims=True)
            hh = num / jnp.maximum(jnp.abs(den), jnp.exp(-m_t))
            out = _sigmoid(og[:, h * LANES:(h + 1) * LANES]) * (_rms(hh) * ng_ref[:, h * LANES:(h + 1) * LANES])
            mix_ref[rows, h * LANES:(h + 1) * LANES] = out
            m_last = m_t[T - 1:T, :]
            w_end = jnp.exp(bc[T - 1:T, :] - bc + igc - m_last)
            decay = w_inter[T - 1:T, :]
            c_scr[h] = decay * c_h + _dot_tn(w_end * v, k)
            n_scr[h:h + 1, :] = decay * n_h + jnp.sum(w_end * k, axis=0, keepdims=True)
            m_scr[:, h:h + 1] = m_last
        return carry

    lax.fori_loop(0, nc, chunk, 0)

    @pl.when(s % sps == sps - 1)
    def _():
        co_ref[...] = c_scr[...]
        no_ref[...] = n_scr[...]
        mo_ref[...] = m_scr[...]


def mlstm(u, gates, b_gate, norm_g, c0, n0, m0, nseq, seqlen, T, rows_per_step, n_valid):
    m = u.shape[0]
    nc = rows_per_step // T
    sps = seqlen // rows_per_step
    gt = jnp.transpose(gates[:, :2 * H].reshape(m // T, T, 2 * H), (0, 2, 1))
    bgc = jnp.zeros((1, LANES), F32).at[0, :2 * H].set(b_gate.reshape(-1))
    bgr = b_gate.reshape(2 * H, 1)
    r = rows_per_step
    kern = functools.partial(_mlstm_kernel, T=T, nc=nc, n_valid=n_valid, sps=sps)
    return pl.pallas_call(
        kern,
        grid=(m // r,),
        in_specs=[pl.BlockSpec((r, D), lambda s: (s, 0)),
                  pl.BlockSpec((r, D), lambda s: (s, 1)),
                  pl.BlockSpec((r, D), lambda s: (s, 2)),
                  pl.BlockSpec((r, LANES), lambda s: (s, 0)),
                  pl.BlockSpec((nc, 2 * H, T), lambda s: (s, 0, 0)),
                  pl.BlockSpec((1, LANES), lambda s: (0, 0)),
                  pl.BlockSpec((2 * H, 1), lambda s: (0, 0)),
                  pl.BlockSpec((1, D), lambda s: (0, 0)),
                  pl.BlockSpec((None, H, 128, 64), lambda s: (s // sps, 0, 0, 0)),
                  pl.BlockSpec((None, H, 64), lambda s: (s // sps, 0, 0)),
                  pl.BlockSpec((None, 1, H), lambda s: (s // sps, 0, 0))],
        out_specs=[pl.BlockSpec((r, D), lambda s: (s, 0)),
                   pl.BlockSpec((None, H, 128, 64), lambda s: (s // sps, 0, 0, 0)),
                   pl.BlockSpec((None, H, 64), lambda s: (s // sps, 0, 0)),
                   pl.BlockSpec((None, 1, H), lambda s: (s // sps, 0, 0))],
        out_shape=[jax.ShapeDtypeStruct((m, D), F32),
                   jax.ShapeDtypeStruct((nseq, H, 128, 64), F32),
                   jax.ShapeDtypeStruct((nseq, H, 64), F32),
                   jax.ShapeDtypeStruct((nseq, 1, H), F32)],
        scratch_shapes=[pltpu.VMEM((H, 128, 64), F32), pltpu.VMEM((H, 64), F32), pltpu.VMEM((1, H), F32)],
        compiler_params=_cp(1),
        name="mlstm",
    )(u, u, u, gates, gt, bgc, bgr, norm_g.reshape(1, D), c0, n0, m0.reshape(nseq, 1, H))


def _gdn_kernel(x_ref, z_ref, gc_ref, gt_ref, cw_ref, pc_ref, pr_ref, ng_ref, s0_ref, cv0_ref,
                mix_ref, so_ref, xbuf, cv_scr, s_scr, *, T, nc, n_valid, sps):
    s = pl.program_id(0)
    R = T * nc

    @pl.when(s % sps == 0)
    def _():
        s_scr[...] = s0_ref[...]
        xbuf[5:8, :] = cv0_ref[...]

    xbuf[8:8 + R, :] = x_ref[...]
    conv = (cw_ref[0:1, :] * xbuf[5:5 + R, :] + cw_ref[1:2, :] * xbuf[6:6 + R, :]
            + cw_ref[2:3, :] * xbuf[7:7 + R, :] + cw_ref[3:4, :] * xbuf[8:8 + R, :])
    cv_scr[...] = _silu(conv)
    xbuf[0:8, :] = xbuf[R:R + 8, :]

    row = lax.broadcasted_iota(jnp.int32, (T, T), 0)
    col = lax.broadcasted_iota(jnp.int32, (T, T), 1)
    causal = col <= row
    strict = col < row
    tril = causal.astype(F32)
    triu = (row <= col).astype(F32)
    eye = (row == col).astype(F32)

    def chunk(c, carry):
        r0 = pl.multiple_of(c * T, T)
        rows = pl.ds(r0, T)
        gpre = gc_ref[rows, :]
        beta_all = _sigmoid(gpre)
        g_all = -jnp.exp(pc_ref[0:1, :]) * _softplus(gpre + pc_ref[1:2, :])
        gt = gt_ref[c]
        gr_all = -jnp.exp(pr_ref[:, 0:1]) * _softplus(gt + pr_ref[:, 1:2])
        gr = gr_all[H:2 * H, :]
        if n_valid < T:
            rid = lax.broadcasted_iota(jnp.int32, (T, LANES), 0)
            beta_all = jnp.where(rid < n_valid, beta_all, 0.0)
            g_all = jnp.where(rid < n_valid, g_all, 0.0)
            cid = lax.broadcasted_iota(jnp.int32, (H, T), 1)
            gr = jnp.where(cid < n_valid, gr, 0.0)
        gamc_all = _dot_hi(tril, g_all)
        gamr_all = _dot_hi(gr, triu)
        for h in range(H):
            sl = slice(h * LANES, (h + 1) * LANES)
            qh = cv_scr[rows, h * LANES:(h + 1) * LANES]
            kh = cv_scr[rows, D + h * LANES:D + (h + 1) * LANES]
            vh = cv_scr[rows, 2 * D + h * LANES:2 * D + (h + 1) * LANES]
            qh = qh * lax.rsqrt(jnp.sum(qh * qh, axis=-1, keepdims=True) + EPS) * LANES ** -0.5
            kh = kh * lax.rsqrt(jnp.sum(kh * kh, axis=-1, keepdims=True) + EPS)
            beta = beta_all[:, h:h + 1]
            gamc = gamc_all[:, H + h:H + h + 1]
            gamr = gamr_all[h:h + 1, :]
            dec = jnp.exp(jnp.where(causal, gamc - gamr, NEG))
            kb = kh.astype(BF16)
            kk = lax.dot_general(kb, kb, NT, preferred_element_type=F32)
            qk = _dot_nt(qh, kb)
            nmat = jnp.where(strict, beta * kk * dec, 0.0)
            xinv = eye - nmat
            npow = nmat
            span = 2
            while span < T:
                npow = _dot_hi(npow, npow)
                xinv = xinv + _dot_hi(xinv, npow)
                span *= 2
            egam = jnp.exp(gamc)
            sol_v = _dot_hi(xinv, beta * vh)
            sol_k = _dot_hi(xinv, (beta * egam) * kh)
            s_h = s_scr[h]
            sb = s_h.astype(BF16)
            delta = sol_v - _dot_nt(sol_k, sb)
            o = egam * _dot_nt(qh, sb) + _dot(qk * dec, delta)
            g_end = gamc[T - 1:T, :]
            w_end = jnp.exp(g_end - gamc)
            s_scr[h] = jnp.exp(g_end) * s_h + _dot_tn(w_end * delta, kb)
            out = _rms(o) * ng_ref[:, sl] * _silu(z_ref[rows, sl])
            mix_ref[rows, sl] = out
        return carry

    lax.fori_loop(0, nc, chunk, 0)

    @pl.when(s % sps == sps - 1)
    def _():
        so_ref[...] = s_scr[...]


def gdn(u, gates, conv_w, a_log, dt_bias, norm_g, s0, conv0, nseq, seqlen, T, rows_per_step, n_valid):
    m = u.shape[0]
    nc = rows_per_step // T
    sps = seqlen // rows_per_step
    r = rows_per_step
    gt = jnp.transpose(gates[:, :2 * H].reshape(m // T, T, 2 * H), (0, 2, 1))
    pc = jnp.zeros((2, LANES), F32).at[0, H:2 * H].set(a_log).at[1, H:2 * H].set(dt_bias)
    pr = jnp.zeros((2 * H, 2), F32).at[H:, 0].set(a_log).at[H:, 1].set(dt_bias)
    kern = functools.partial(_gdn_kernel, T=T, nc=nc, n_valid=n_valid, sps=sps)
    return pl.pallas_call(
        kern,
        grid=(m // r,),
        in_specs=[pl.BlockSpec((r, 3 * D), lambda s: (s, 0)),
                  pl.BlockSpec((r, D), lambda s: (s, 3)),
                  pl.BlockSpec((r, LANES), lambda s: (s, 0)),
                  pl.BlockSpec((nc, 2 * H, T), lambda s: (s, 0, 0)),
                  pl.BlockSpec((4, 3 * D), lambda s: (0, 0)),
                  pl.BlockSpec((2, LANES), lambda s: (0, 0)),
                  pl.BlockSpec((2 * H, 2), lambda s: (0, 0)),
                  pl.BlockSpec((1, D), lambda s: (0, 0)),
                  pl.BlockSpec((None, H, 128, 128), lambda s: (s // sps, 0, 0, 0)),
                  pl.BlockSpec((None, 3, 3 * D), lambda s: (s // sps, 0, 0))],
        out_specs=[pl.BlockSpec((r, D), lambda s: (s, 0)),
                   pl.BlockSpec((None, H, 128, 128), lambda s: (s // sps, 0, 0, 0))],
        out_shape=[jax.ShapeDtypeStruct((m, D), F32),
                   jax.ShapeDtypeStruct((nseq, H, 128, 128), F32)],
        scratch_shapes=[pltpu.VMEM((r + 8, 3 * D), F32), pltpu.VMEM((r, 3 * D), F32),
                        pltpu.VMEM((H, 128, 128), F32)],
        compiler_params=_cp(1),
        name="gdn",
    )(u, u, gates, gt, conv_w, pc, pr, norm_g.reshape(1, D), s0, conv0)


def _bucket(rel):
    n = jnp.maximum(rel, 0)
    nf = jnp.maximum(n, 1).astype(F32)
    large = 16 + (jnp.log(nf / 16) / math.log(128 / 16) * 16).astype(jnp.int32)
    return jnp.where(n < 16, n, jnp.minimum(large, 31))


def _rel_bias_shifted(rel_bias, rel):
    return rel_bias[_bucket(rel)] - rel_bias[31]


def _diff_lambda(lw_ref, lam_init):
    lw = lw_ref[...]
    a = jnp.sum(lw[0:1, :] * lw[1:2, :], axis=-1, keepdims=True)
    b = jnp.sum(lw[2:3, :] * lw[3:4, :], axis=-1, keepdims=True)
    return jnp.exp(a) - jnp.exp(b) + lam_init


def _dattn_p_kernel(q_ref, k_ref, v_ref, bs_ref, bd_ref, lw_ref, ng_ref, o_ref, m_scr, l_scr, acc_scr,
                    *, TB, lam_init):
    i = pl.program_id(2)
    lane = lax.broadcasted_iota(jnp.int32, (TB, LANES), 1)
    q = q_ref[...] * 0.125
    qq = jnp.concatenate([jnp.where(lane < 64, q, 0.0), jnp.where(lane >= 64, q, 0.0)], axis=0).astype(BF16)
    m_scr[...] = jnp.full_like(m_scr, NEG)
    l_scr[...] = jnp.zeros_like(l_scr)
    acc_scr[...] = jnp.zeros_like(acc_scr)

    def step(j, bias):
        rows = pl.ds(pl.multiple_of(j * TB, TB), TB)
        kb = k_ref[rows, :].astype(BF16)
        vb = v_ref[rows, :].astype(BF16)
        sc = lax.dot_general(qq, kb, NT, preferred_element_type=F32)
        if bias is not None:
            sc = sc + bias
        m_old = m_scr[...]
        m_new = jnp.maximum(m_old, jnp.max(sc, axis=-1, keepdims=True))
        alpha = jnp.exp(m_old - m_new)
        p = jnp.exp(sc - m_new)
        l_scr[...] = alpha * l_scr[...] + jnp.sum(p, axis=-1, keepdims=True)
        acc_scr[...] = alpha * acc_scr[...] + jnp.dot(p.astype(BF16), vb, preferred_element_type=F32)
        m_scr[...] = m_new

    def body(j, carry):
        step(j, None)
        return carry

    lax.fori_loop(0, jnp.maximum(i - 1, 0), body, 0)

    @pl.when(i >= 1)
    def _():
        step(i - 1, bs_ref[...])

    step(i, bd_ref[...])
    lam = _diff_lambda(lw_ref, lam_init)
    o = acc_scr[0:TB, :] / l_scr[0:TB, :] - lam * (acc_scr[TB:2 * TB, :] / l_scr[TB:2 * TB, :])
    o_ref[...] = _rms(o) * ng_ref[...] * (1.0 - lam_init)


def diff_attn_prompt(u, lam_w, norm_g, rel_bias, lam_init, TB=256):
    nq = SEQ // TB
    ii = jnp.arange(TB)
    rel_d = ii[:, None] - ii[None, :]
    bd = jnp.where((rel_d >= 0)[..., None], _rel_bias_shifted(rel_bias, rel_d), NEG)
    bs = _rel_bias_shifted(rel_bias, rel_d + TB)
    bd = jnp.transpose(bd, (2, 0, 1))
    bs = jnp.transpose(bs, (2, 0, 1))
    bd = jnp.concatenate([bd, bd], axis=1)
    bs = jnp.concatenate([bs, bs], axis=1)
    kern = functools.partial(_dattn_p_kernel, TB=TB, lam_init=lam_init)
    return pl.pallas_call(
        kern,
        grid=(BATCH, H, nq),
        in_specs=[pl.BlockSpec((TB, LANES), lambda b, h, i: (b * nq + i, h)),
                  pl.BlockSpec((SEQ, LANES), lambda b, h, i: (b, H + h)),
                  pl.BlockSpec((SEQ, LANES), lambda b, h, i: (b, 2 * H + h)),
                  pl.BlockSpec((None, 2 * TB, TB), lambda b, h, i: (h, 0, 0)),
                  pl.BlockSpec((None, 2 * TB, TB), lambda b, h, i: (h, 0, 0)),
                  pl.BlockSpec((4, 64), lambda b, h, i: (0, 0)),
                  pl.BlockSpec((1, LANES), lambda b, h, i: (0, h))],
        out_specs=pl.BlockSpec((TB, LANES), lambda b, h, i: (b * nq + i, h)),
        out_shape=jax.ShapeDtypeStruct((BATCH * SEQ, D), F32),
        scratch_shapes=[pltpu.VMEM((2 * TB, 1), F32), pltpu.VMEM((2 * TB, 1), F32),
                        pltpu.VMEM((2 * TB, LANES), F32)],
        compiler_params=_cp(3),
        name="diff_attn_prompt",
    )(u, u, u, bs, bd, lam_w, norm_g.reshape(1, D))


def _dattn_s_kernel(pt_ref, qm_ref, qb_ref, kp_ref, vp_ref, kn_ref, vn_ref, bf_ref, bl_ref, bn_ref, lw_ref, ng_ref,
                    o_ref, m_scr, l_scr, acc_scr, *, lam_init):
    del pt_ref
    p = pl.program_id(1)
    last = pl.num_programs(1) - 1

    @pl.when(p == 0)
    def _():
        m_scr[...] = jnp.full_like(m_scr, NEG)
        l_scr[...] = jnp.zeros_like(l_scr)
        acc_scr[...] = jnp.zeros_like(acc_scr)

    sc = lax.dot_general(qm_ref[...].astype(BF16), kp_ref[...].astype(BF16), NT, preferred_element_type=F32)
    sc = sc + jnp.where(p == last, bl_ref[...], bf_ref[...])
    m_old = m_scr[...]
    m_new = jnp.maximum(m_old, jnp.max(sc, axis=-1, keepdims=True))
    alpha = jnp.exp(m_old - m_new)
    pr = jnp.exp(sc - m_new)
    l_scr[...] = alpha * l_scr[...] + jnp.sum(pr, axis=-1, keepdims=True)
    acc_scr[...] = alpha * acc_scr[...] + _dot(pr, vp_ref[...])
    m_scr[...] = m_new

    @pl.when(p == last)
    def _():
        scn = lax.dot_general(qb_ref[...], kn_ref[...], NT, preferred_element_type=F32) + bn_ref[...]
        m_o = m_scr[...]
        m_n = jnp.maximum(m_o, jnp.max(scn, axis=-1, keepdims=True))
        al = jnp.exp(m_o - m_n)
        pn = jnp.exp(scn - m_n)
        l_n = al * l_scr[...] + jnp.sum(pn, axis=-1, keepdims=True)
        pv = jnp.dot(pn, vn_ref[...], preferred_element_type=F32)
        lam = _diff_lambda(lw_ref, lam_init)
        for h in range(H):
            sl = slice(h * LANES, (h + 1) * LANES)
            r0 = slice(h * 16, h * 16 + 8)
            r1 = slice(h * 16 + 8, h * 16 + 16)
            a0 = (al[r0] * acc_scr[r0, :] + pv[r0, sl]) / l_n[r0]
            a1 = (al[r1] * acc_scr[r1, :] + pv[r1, sl]) / l_n[r1]
            o_ref[:, sl] = _rms(a0 - lam * a1) * ng_ref[:, sl] * (1.0 - lam_init)


def diff_attn_sample(u, cache_k, cache_v, page_table, lam_w, norm_g, rel_bias, lam_init):
    nb = DEC_BATCH
    nr = 2 * H * DEC_PAD
    q = u[:, :D].reshape(nb, DEC_PAD, D) * 0.125
    grp_row = jnp.arange(2 * H)[:, None, None]
    grp_col = (jnp.arange(D) // 64)[None, None, :]
    qbig = jnp.where(grp_row == grp_col, q[:, None, :, :], 0.0).reshape(nb, nr, D)
    q4 = jnp.transpose(q.reshape(nb, DEC_PAD, H, 1, LANES), (0, 2, 3, 1, 4))
    comp = (jnp.arange(LANES) // 64)[None, None, None, None, :]
    qm = jnp.where(comp == jnp.arange(2)[None, None, :, None, None], q4, 0.0).reshape(nb, nr, LANES)
    row_h = jnp.arange(nr) // (2 * DEC_PAD)
    row_q = jnp.arange(nr) % DEC_PAD
    col_t = jnp.arange(PAGE * H) // H
    col_h = jnp.arange(PAGE * H) % H
    same = row_h[:, None] == col_h[None, :]
    far = rel_bias[31][row_h][:, None]
    rel_last = PAGE + row_q[:, None] - col_t[None, :]
    bias_l = rel_bias[_bucket(rel_last), row_h[:, None]] - far
    b_far = jnp.where(same, 0.0, NEG).astype(F32)
    b_last = jnp.where(same, bias_l, NEG).astype(F32)
    kj = jnp.arange(DEC_PAD)
    rel_new = row_q[:, None] - kj[None, :]
    vis = (rel_new >= 0) & (kj[None, :] < DEC_SEQ)
    bias_n = rel_bias[_bucket(rel_new), row_h[:, None]] - far
    b_new = jnp.where(vis, bias_n, NEG).astype(F32)
    kern = functools.partial(_dattn_s_kernel, lam_init=lam_init)
    grid_spec = pltpu.PrefetchScalarGridSpec(
        num_scalar_prefetch=1,
        grid=(nb, N_PAGES),
        in_specs=[pl.BlockSpec((None, nr, LANES), lambda b, p, pt: (b, 0, 0)),
                  pl.BlockSpec((None, nr, D), lambda b, p, pt: (b, 0, 0)),
                  pl.BlockSpec((None, PAGE * H, LANES), lambda b, p, pt: (pt[b, p], 0, 0)),
                  pl.BlockSpec((None, PAGE * H, LANES), lambda b, p, pt: (pt[b, p], 0, 0)),
                  pl.BlockSpec((DEC_PAD, D), lambda b, p, pt: (b, 1)),
                  pl.BlockSpec((DEC_PAD, D), lambda b, p, pt: (b, 2)),
                  pl.BlockSpec((nr, PAGE * H), lambda b, p, pt: (0, 0)),
                  pl.BlockSpec((nr, PAGE * H), lambda b, p, pt: (0, 0)),
                  pl.BlockSpec((nr, DEC_PAD), lambda b, p, pt: (0, 0)),
                  pl.BlockSpec((4, 64), lambda b, p, pt: (0, 0)),
                  pl.BlockSpec((1, D), lambda b, p, pt: (0, 0))],
        out_specs=pl.BlockSpec((DEC_PAD, D), lambda b, p, pt: (b, 0)),
        scratch_shapes=[pltpu.VMEM((nr, 1), F32), pltpu.VMEM((nr, 1), F32), pltpu.VMEM((nr, LANES), F32)],
    )
    return pl.pallas_call(
        kern,
        grid_spec=grid_spec,
        out_shape=jax.ShapeDtypeStruct((nb * DEC_PAD, D), F32),
        compiler_params=_cp(2),
        name="diff_attn_sample",
    )(page_table, qm, qbig, cache_k, cache_v, u, u, b_far, b_last, b_new, lam_w, norm_g.reshape(1, D))


def _router_kernel(x_ref, w_ref, o_ref):
    logits = _dot_hi(x_ref[...], w_ref[...])
    lane = lax.broadcasted_iota(jnp.int32, logits.shape, 1)
    logits = jnp.where(lane < N_EXPERTS, logits, -jnp.inf)
    m1 = jnp.max(logits, axis=-1, keepdims=True)
    i1 = jnp.min(jnp.where(logits == m1, lane, LANES), axis=-1, keepdims=True)
    rest = jnp.where(lane == i1, -jnp.inf, logits)
    m2 = jnp.max(rest, axis=-1, keepdims=True)
    i2 = jnp.min(jnp.where(rest == m2, lane, LANES), axis=-1, keepdims=True)
    e2 = jnp.exp(m2 - m1)
    g1 = 1.0 / (1.0 + e2)
    g2 = e2 / (1.0 + e2)
    o_ref[...] = jnp.where(lane == 0, i1.astype(F32),
                           jnp.where(lane == 1, i2.astype(F32),
                                     jnp.where(lane == 2, g1, jnp.where(lane == 3, g2, 0.0))))


def router(x, w_router, tm):
    m = x.shape[0]
    w = jnp.pad(w_router, ((0, 0), (0, LANES - N_EXPERTS)))
    return pl.pallas_call(
        _router_kernel,
        grid=(m // tm,),
        in_specs=[pl.BlockSpec((tm, D), lambda i: (i, 0)),
                  pl.BlockSpec((D, LANES), lambda i: (0, 0))],
        out_specs=pl.BlockSpec((tm, LANES), lambda i: (i, 0)),
        out_shape=jax.ShapeDtypeStruct((m, LANES), F32),
        compiler_params=_cp(1),
        name="router",
    )(x, w)


def _gather_kernel(src_ref, x_hbm, o_ref, sem, *, G):
    base = pl.program_id(0) * G

    def issue(r, carry):
        pltpu.make_async_copy(x_hbm.at[pl.ds(src_ref[base + r], 1), :], o_ref.at[pl.ds(r, 1), :], sem).start()
        return carry

    lax.fori_loop(0, G, issue, 0)

    def drain(r, carry):
        pltpu.make_async_copy(x_hbm.at[pl.ds(0, 1), :], o_ref.at[pl.ds(r, 1), :], sem).wait()
        return carry

    lax.fori_loop(0, G, drain, 0)


def gather_rows(x, src, G):
    n = src.shape[0]
    grid_spec = pltpu.PrefetchScalarGridSpec(
        num_scalar_prefetch=1,
        grid=(n // G,),
        in_specs=[pl.BlockSpec(memory_space=pl.ANY)],
        out_specs=pl.BlockSpec((G, D), lambda i, s: (i, 0)),
        scratch_shapes=[pltpu.SemaphoreType.DMA(())],
    )
    return pl.pallas_call(
        functools.partial(_gather_kernel, G=G),
        grid_spec=grid_spec,
        out_shape=jax.ShapeDtypeStruct((n, D), x.dtype),
        compiler_params=_cp(1),
        name="gather_rows",
    )(src, x)


def _expert_kernel(be_ref, nu_ref, x_ref, wg_ref, wu_ref, wd_ref, o_ref, xb_scr, acc_scr):
    i = pl.program_id(0)
    j = pl.program_id(1)
    valid = i < nu_ref[0]

    @pl.when(valid & (j == 0))
    def _():
        xb_scr[...] = x_ref[...].astype(BF16)
        acc_scr[...] = jnp.zeros_like(acc_scr)

    @pl.when(valid)
    def _():
        xb = xb_scr[...]
        gate = jnp.dot(xb, wg_ref[...].astype(BF16), preferred_element_type=F32)
        up = jnp.dot(xb, wu_ref[...].astype(BF16), preferred_element_type=F32)
        acc_scr[...] += _dot(_silu(gate) * up, wd_ref[...])

    @pl.when(j == pl.num_programs(1) - 1)
    def _():
        o_ref[...] = jnp.where(valid, acc_scr[...], 0.0)


def expert_ffn(xs, w_gu, w_down, blk_e, n_used, tm, tf=512):
    n = xs.shape[0]
    nf = D_FF_EXPERT // tf

    def jeff(i, j, nu):
        return jnp.where(i < nu[0], j, nf - 1)

    grid_spec = pltpu.PrefetchScalarGridSpec(
        num_scalar_prefetch=2,
        grid=(n // tm, nf),
        in_specs=[pl.BlockSpec((tm, D), lambda i, j, be, nu: (i, 0)),
                  pl.BlockSpec((None, D, tf), lambda i, j, be, nu: (be[i], 0, jeff(i, j, nu))),
                  pl.BlockSpec((None, D, tf), lambda i, j, be, nu: (be[i], 0, nf + jeff(i, j, nu))),
                  pl.BlockSpec((None, tf, D), lambda i, j, be, nu: (be[i], jeff(i, j, nu), 0))],
        out_specs=pl.BlockSpec((tm, D), lambda i, j, be, nu: (i, 0)),
        scratch_shapes=[pltpu.VMEM((tm, D), BF16), pltpu.VMEM((tm, D), F32)],
    )
    return pl.pallas_call(
        _expert_kernel,
        grid_spec=grid_spec,
        out_shape=jax.ShapeDtypeStruct((n, D), F32),
        compiler_params=_cp(2),
        name="expert_ffn",
    )(blk_e, n_used, xs, w_gu, w_gu, w_down)


def _combine_kernel(d0_ref, d1_ref, yb_hbm, x_ref, rt_ref, g_ref, b_ref, o_ref, buf0, buf1, sem, *, G):
    base = pl.program_id(0) * G

    def issue(r, carry):
        pltpu.make_async_copy(yb_hbm.at[pl.ds(d0_ref[base + r], 1), :], buf0.at[pl.ds(r, 1), :], sem).start()
        pltpu.make_async_copy(yb_hbm.at[pl.ds(d1_ref[base + r], 1), :], buf1.at[pl.ds(r, 1), :], sem).start()
        return carry

    lax.fori_loop(0, G, issue, 0)

    def drain(r, carry):
        pltpu.make_async_copy(yb_hbm.at[pl.ds(0, 1), :], buf0.at[pl.ds(r, 1), :], sem).wait()
        pltpu.make_async_copy(yb_hbm.at[pl.ds(0, 1), :], buf1.at[pl.ds(r, 1), :], sem).wait()
        return carry

    lax.fori_loop(0, G, drain, 0)
    rt = rt_ref[...]
    y = rt[:, 2:3] * buf0[...] + rt[:, 3:4] * buf1[...]
    o_ref[...] = _layer_norm(ALPHA * x_ref[...] + y, g_ref[...], b_ref[...])


def combine_ln(yb, x, route, d0, d1, g, b, G):
    m = x.shape[0]
    grid_spec = pltpu.PrefetchScalarGridSpec(
        num_scalar_prefetch=2,
        grid=(m // G,),
        in_specs=[pl.BlockSpec(memory_space=pl.ANY),
                  pl.BlockSpec((G, D), lambda i, a, c: (i, 0)),
                  pl.BlockSpec((G, LANES), lambda i, a, c: (i, 0)),
                  pl.BlockSpec((1, D), lambda i, a, c: (0, 0)),
                  pl.BlockSpec((1, D), lambda i, a, c: (0, 0))],
        out_specs=pl.BlockSpec((G, D), lambda i, a, c: (i, 0)),
        scratch_shapes=[pltpu.VMEM((G, D), F32), pltpu.VMEM((G, D), F32), pltpu.SemaphoreType.DMA(())],
    )
    return pl.pallas_call(
        functools.partial(_combine_kernel, G=G),
        grid_spec=grid_spec,
        out_shape=jax.ShapeDtypeStruct((m, D), F32),
        compiler_params=_cp(1),
        name="combine_ln",
    )(d0, d1, yb, x, route, g.reshape(1, D), b.reshape(1, D))


def moe_ln(x, w_router, w_gu, w_down, g, b, tm=512):
    m = x.shape[0]
    route = router(x, w_router, 640)
    flat_e = route[:, :2].astype(jnp.int32).reshape(-1)
    onehot = (flat_e[:, None] == jnp.arange(N_EXPERTS)[None, :]).astype(jnp.int32)
    csum = jnp.cumsum(onehot, axis=0)
    counts = csum[-1]
    rank = jnp.sum((csum - 1) * onehot, axis=1)
    padded = (counts + tm - 1) // tm * tm
    p_end = jnp.cumsum(padded)
    dest = (p_end - padded)[flat_e] + rank
    n_blocks = -(-(2 * m + N_EXPERTS * (tm - 1)) // tm)
    n_rows = n_blocks * tm
    n_used = (p_end[-1] // tm).astype(jnp.int32).reshape(1)
    blk = jnp.minimum(jnp.arange(n_blocks), n_used[0] - 1) * tm
    blk_e = jnp.minimum(jnp.searchsorted(p_end, blk, side='right'), N_EXPERTS - 1).astype(jnp.int32)
    x_z = jnp.concatenate([x, jnp.zeros((8, D), x.dtype)], axis=0)
    src = jnp.full((n_rows,), m, jnp.int32).at[dest].set(jnp.arange(2 * m, dtype=jnp.int32) // 2)
    xs = gather_rows(x_z, src, tm)
    yb = expert_ffn(xs, w_gu, w_down, blk_e, n_used, tm)
    d = dest.reshape(m, 2).astype(jnp.int32)
    return combine_ln(yb, x, route, d[:, 0], d[:, 1], g, b, 640)


def kernel(x_prompt, x_sample, mem_prompt, page_table, cache_diff_k, cache_diff_v, cache_mem_k, cache_mem_v,
           state_mlstm_C, state_mlstm_n, state_mlstm_m, state_gdn_S, state_gdn_conv, rel_bias,
           w_in_a, b_gate_a, norm_a, w_in_b, lambda_b, norm_b, w_in_c, conv_c, a_log_c, dt_bias_c, norm_c,
           w_mem_kv, w_out, ln_g, ln_b, w_ffn_gu, w_ffn_down, w_router, w_exp_gu, w_exp_down):
    mp = BATCH * SEQ
    ms = DEC_BATCH * DEC_PAD
    x_p = x_prompt.reshape(mp, D)
    x_s = jnp.pad(x_sample, ((0, 0), (0, DEC_PAD - DEC_SEQ), (0, 0))).reshape(ms, D)
    mem2 = mem_prompt.reshape(BATCH * MEM_TOKENS, D)
    n_phys = cache_diff_k.shape[1]

    def sample_rows(t):
        return t.reshape(DEC_BATCH, DEC_PAD, -1)[:, :DEC_SEQ]

    def pad_gates(w):
        return jnp.pad(w, ((0, 0), (0, LANES - w.shape[1])))

    mlstm_p, mlstm_s, diff_p, diff_s, gdn_p, gdn_s, mem_k_new, mem_v_new = [], [], [], [], [], [], [], []
    for i in range(DEPTH):
        kind, j = i % 3, i // 3
        mem_kv = matmul(mem2, w_mem_kv[i], MEM_TOKENS, 512).reshape(BATCH, MEM_TOKENS, 2 * MEM_Q)
        mem_k_new.append(mem_kv[:, :, :MEM_Q].reshape(BATCH, MEM_TOKENS, MEM_HEADS, 128))
        mem_v_new.append(mem_kv[:, :, MEM_Q:].reshape(BATCH, MEM_TOKENS, MEM_HEADS, 128))
        if kind == 0:
            w = w_in_a[j]
            w_main = jnp.concatenate([w[:, :3 * D], w[:, 3 * D + 2 * H:]], axis=1)
            w_g = pad_gates(w[:, 3 * D:3 * D + 2 * H])
            u_p, u_s = matmul(x_p, w_main, 512, 512), matmul(x_s, w_main, ms, 512)
            g_p, g_s = matmul(x_p, w_g, 512, LANES), matmul(x_s, w_g, ms, LANES)
            zc = jnp.zeros((BATCH, H, 128, 64), F32)
            zn = jnp.zeros((BATCH, H, 64), F32)
            zm = jnp.zeros((BATCH, H), F32)
            mix_p, c_p, n_p, m_p = mlstm(u_p, g_p, b_gate_a[j], norm_a[j], zc, zn, zm,
                                         BATCH, SEQ, CHUNK, 256, CHUNK)
            mix_s, c_s, n_s, m_s = mlstm(u_s, g_s, b_gate_a[j], norm_a[j], state_mlstm_C[j], state_mlstm_n[j],
                                         state_mlstm_m[j], DEC_BATCH, DEC_PAD, DEC_PAD, DEC_PAD, DEC_SEQ)
            mlstm_p.append((c_p, n_p, m_p.reshape(BATCH, H)))
            mlstm_s.append((c_s, n_s, m_s.reshape(DEC_BATCH, H)))
            mq_col = 6
        elif kind == 1:
            lam_init = 0.8 - 0.6 * math.exp(-0.3 * i)
            w_main = w_in_b[j]
            u_p, u_s = matmul(x_p, w_main, 512, 512), matmul(x_s, w_main, ms, 512)
            mix_p = diff_attn_prompt(u_p, lambda_b[j], norm_b[j], rel_bias, lam_init)
            mix_s = diff_attn_sample(u_s, cache_diff_k[j].reshape(n_phys, PAGE * H, LANES),
                                     cache_diff_v[j].reshape(n_phys, PAGE * H, LANES), page_table,
                                     lambda_b[j], norm_b[j], rel_bias, lam_init)
            diff_p.append((u_p[:, D:2 * D].reshape(BATCH, SEQ, H, 128), u_p[:, 2 * D:3 * D].reshape(BATCH, SEQ, H, 128)))
            diff_s.append((sample_rows(u_s[:, D:2 * D]).reshape(DEC_BATCH, DEC_SEQ, H, 128),
                           sample_rows(u_s[:, 2 * D:3 * D]).reshape(DEC_BATCH, DEC_SEQ, H, 128)))
            mq_col = 6
        else:
            w = w_in_c[j]
            w_main = jnp.concatenate([w[:, :4 * D], w[:, 4 * D + 2 * H:]], axis=1)
            w_g = pad_gates(w[:, 4 * D:4 * D + 2 * H])
            u_p, u_s = matmul(x_p, w_main, 512, 512), matmul(x_s, w_main, ms, 512)
            g_p, g_s = matmul(x_p, w_g, 512, LANES), matmul(x_s, w_g, ms, LANES)
            zs = jnp.zeros((BATCH, H, 128, 128), F32)
            zv = jnp.zeros((BATCH, 3, 3 * D), F32)
            mix_p, s_p = gdn(u_p, g_p, conv_c[j], a_log_c[j], dt_bias_c[j], norm_c[j], zs, zv,
                             BATCH, SEQ, CHUNK, 256, CHUNK)
            mix_s, s_s = gdn(u_s, g_s, conv_c[j], a_log_c[j], dt_bias_c[j], norm_c[j], state_gdn_S[j],
                             state_gdn_conv[j], DEC_BATCH, DEC_PAD, DEC_PAD, DEC_PAD, DEC_SEQ)
            conv_p = u_p[:, :3 * D].reshape(BATCH, SEQ, 3 * D)[:, SEQ - 3:]
            conv_s = sample_rows(u_s[:, :3 * D])[:, DEC_SEQ - 3:]
            gdn_p.append((s_p, conv_p))
            gdn_s.append((s_s, conv_s))
            mq_col = 8
        n_main = u_p.shape[1]
        mo_p = mem_attention(u_p.reshape(BATCH, SEQ, n_main), mq_col, mem_kv, 0, mem_kv, 1, 512).reshape(mp, MEM_Q)
        cmk = cache_mem_k[i].reshape(DEC_BATCH, MEM_TOKENS, MEM_Q)
        cmv = cache_mem_v[i].reshape(DEC_BATCH, MEM_TOKENS, MEM_Q)
        mo_s = mem_attention(u_s.reshape(DEC_BATCH, DEC_PAD, n_main), mq_col, cmk, 0, cmv, 0, DEC_PAD).reshape(ms, MEM_Q)
        x_p = outproj_ln(mix_p, mo_p, x_p, w_out[i], ln_g[i, 0], ln_b[i, 0], 512)
        x_s = outproj_ln(mix_s, mo_s, x_s, w_out[i], ln_g[i, 0], ln_b[i, 0], ms)
        k_f = i // 2
        if i % 2 == 0:
            x_p = ffn_ln(x_p, w_ffn_gu[k_f], w_ffn_down[k_f], ln_g[i, 1], ln_b[i, 1], 1024)
            x_s = ffn_ln(x_s, w_ffn_gu[k_f], w_ffn_down[k_f], ln_g[i, 1], ln_b[i, 1], ms)
        else:
            x_all = moe_ln(jnp.concatenate([x_p, x_s], axis=0), w_router[k_f], w_exp_gu[k_f], w_exp_down[k_f],
                           ln_g[i, 1], ln_b[i, 1])
            x_p, x_s = x_all[:mp], x_all[mp:]

    def stack(lst, k):
        return jnp.stack([t[k] for t in lst])

    return (x_p.reshape(BATCH, SEQ, D), sample_rows(x_s),
            stack(mlstm_p, 0), stack(mlstm_p, 1), stack(mlstm_p, 2),
            stack(mlstm_s, 0), stack(mlstm_s, 1), stack(mlstm_s, 2),
            stack(diff_p, 0), stack(diff_p, 1), stack(diff_s, 0), stack(diff_s, 1),
            stack(gdn_p, 0), stack(gdn_p, 1), stack(gdn_s, 0), stack(gdn_s, 1),
            jnp.stack(mem_k_new), jnp.stack(mem_v_new))
```

```python
import functools
import math

import jax
import jax.numpy as jnp
from jax import lax
from jax.experimental import pallas as pl
from jax.experimental.pallas import tpu as pltpu

F32 = jnp.float32
BF16 = jnp.bfloat16
HI = lax.Precision.HIGHEST
NT = (((1,), (1,)), ((), ()))
TN = (((0,), (0,)), ((), ()))

D = 1024
H = 8
DEPTH = 4
SEQ = 4096
BATCH = 4
DEC_BATCH = 32
DEC_SEQ = 4
DEC_PAD = 8
PAST_LEN = 8192
PAGE = 128
N_PAGES = PAST_LEN // PAGE
PAGES_PER_STEP = 4
MEM_TOKENS = 256
MEM_Q = 512
MEM_HEADS = 4
CHUNK = 64
N_EXPERTS = 8
D_FF = 2752
D_FF_PAD = 2816
D_FF_EXPERT = 3584
EPS = 1e-6
ALPHA = (2 * DEPTH) ** 0.25
NEG = -1e30
LANES = 128
VMEM_LIMIT = 56 * 1024 * 1024


def _cp(n_axes, vmem=VMEM_LIMIT):
    return pltpu.CompilerParams(dimension_semantics=("arbitrary",) * n_axes, vmem_limit_bytes=vmem)


def _dot(a, b):
    return jnp.dot(a.astype(BF16), b.astype(BF16), preferred_element_type=F32)


def _dot_nt(a, b):
    return lax.dot_general(a.astype(BF16), b.astype(BF16), NT, preferred_element_type=F32)


def _dot_tn(a, b):
    return lax.dot_general(a.astype(BF16), b.astype(BF16), TN, preferred_element_type=F32)


def _dot_hi(a, b):
    return jnp.dot(a, b, preferred_element_type=F32, precision=HI)


def _split_bf16(a):
    hi = a.astype(BF16)
    return hi, (a - hi.astype(F32)).astype(BF16)


def _dot_split(a_hi, a_lo, b_hi, b_lo):
    return (jnp.dot(a_hi, b_hi, preferred_element_type=F32)
            + (jnp.dot(a_lo, b_hi, preferred_element_type=F32) + jnp.dot(a_hi, b_lo, preferred_element_type=F32)))


def _sigmoid(x):
    return 1.0 / (1.0 + jnp.exp(-x))


def _silu(x):
    return x * _sigmoid(x)


def _softplus(x):
    return jnp.maximum(x, 0.0) + jnp.log1p(jnp.exp(-jnp.abs(x)))


def _log_sigmoid(x):
    return -_softplus(-x)


def _layer_norm(v, g, b):
    mu = jnp.mean(v, axis=-1, keepdims=True)
    d = v - mu
    var = jnp.mean(d * d, axis=-1, keepdims=True)
    return d * lax.rsqrt(var + EPS) * g + b


def _rms(h):
    return h * lax.rsqrt(jnp.mean(h * h, axis=-1, keepdims=True) + EPS)


def _mm_kernel(x_ref, w_ref, o_ref, xb_scr):
    @pl.when(pl.program_id(1) == 0)
    def _():
        xb_scr[...] = x_ref[...].astype(BF16)

    o_ref[...] = jnp.dot(xb_scr[...], w_ref[...], preferred_element_type=F32)


def matmul(x, w, tm, tn):
    m, k = x.shape
    n = w.shape[1]
    return pl.pallas_call(
        _mm_kernel,
        grid=(m // tm, n // tn),
        in_specs=[pl.BlockSpec((tm, k), lambda i, j: (i, 0)),
                  pl.BlockSpec((k, tn), lambda i, j: (0, j))],
        out_specs=pl.BlockSpec((tm, tn), lambda i, j: (i, j)),
        out_shape=jax.ShapeDtypeStruct((m, n), F32),
        scratch_shapes=[pltpu.VMEM((tm, k), BF16)],
        compiler_params=_cp(2),
        name="matmul",
    )(x, w.astype(BF16))


def _headrows_kernel(k_ref, v_ref, ko_ref, vo_ref):
    tm = k_ref.shape[0]
    for h in range(H):
        sl = slice(h * LANES, (h + 1) * LANES)
        ko_ref[pl.ds(h, tm, stride=H), :] = k_ref[:, sl]
        vo_ref[pl.ds(h, tm, stride=H), :] = v_ref[:, sl]


def kv_head_rows(u, tm):
    m = u.shape[0]
    return pl.pallas_call(
        _headrows_kernel,
        grid=(m // tm,),
        in_specs=[pl.BlockSpec((tm, D), lambda i: (i, 1)),
                  pl.BlockSpec((tm, D), lambda i: (i, 2))],
        out_specs=[pl.BlockSpec((tm * H, LANES), lambda i: (i, 0)),
                   pl.BlockSpec((tm * H, LANES), lambda i: (i, 0))],
        out_shape=[jax.ShapeDtypeStruct((m * H, LANES), F32), jax.ShapeDtypeStruct((m * H, LANES), F32)],
        compiler_params=_cp(1),
        name="kv_head_rows",
    )(u, u)


def _outproj_kernel(mix_ref, mo_ref, x_ref, w1_ref, w2_ref, g_ref, b_ref, o_ref):
    y = _dot(mix_ref[...], w1_ref[...]) + _dot(mo_ref[...], w2_ref[...])
    o_ref[...] = _layer_norm(ALPHA * x_ref[...] + y, g_ref[...], b_ref[...])


def outproj_ln(mix, mo, x, w_out, g, b, tm):
    m = x.shape[0]
    w1 = w_out[:D].astype(BF16)
    w2 = w_out[D:].astype(BF16)
    return pl.pallas_call(
        _outproj_kernel,
        grid=(m // tm,),
        in_specs=[pl.BlockSpec((tm, D), lambda i: (i, 0)),
                  pl.BlockSpec((tm, MEM_Q), lambda i: (i, 0)),
                  pl.BlockSpec((tm, D), lambda i: (i, 0)),
                  pl.BlockSpec((D, D), lambda i: (0, 0)),
                  pl.BlockSpec((MEM_Q, D), lambda i: (0, 0)),
                  pl.BlockSpec((1, D), lambda i: (0, 0)),
                  pl.BlockSpec((1, D), lambda i: (0, 0))],
        out_specs=pl.BlockSpec((tm, D), lambda i: (i, 0)),
        out_shape=jax.ShapeDtypeStruct((m, D), F32),
        compiler_params=_cp(1),
        name="outproj_ln",
    )(mix, mo, x, w1, w2, g.reshape(1, D), b.reshape(1, D))


def _ffn_kernel(x_ref, wg_ref, wu_ref, wd_ref, g_ref, b_ref, o_ref, xb_scr, acc_scr):
    j = pl.program_id(1)

    @pl.when(j == 0)
    def _():
        xb_scr[...] = x_ref[...].astype(BF16)
        acc_scr[...] = jnp.zeros_like(acc_scr)

    xb = xb_scr[...]
    gate = jnp.dot(xb, wg_ref[...], preferred_element_type=F32)
    up = jnp.dot(xb, wu_ref[...], preferred_element_type=F32)
    acc_scr[...] += _dot(_silu(gate) * up, wd_ref[...])

    @pl.when(j == pl.num_programs(1) - 1)
    def _():
        o_ref[...] = _layer_norm(ALPHA * x_ref[...] + acc_scr[...], g_ref[...], b_ref[...])


def ffn_ln(x, w_gu, w_down, g, b, tm, tf=256):
    m = x.shape[0]
    pad = D_FF_PAD - D_FF
    wg = jnp.pad(w_gu[:, :D_FF], ((0, 0), (0, pad))).astype(BF16)
    wu = jnp.pad(w_gu[:, D_FF:], ((0, 0), (0, pad))).astype(BF16)
    wd = jnp.pad(w_down, ((0, pad), (0, 0))).astype(BF16)
    return pl.pallas_call(
        _ffn_kernel,
        grid=(m // tm, D_FF_PAD // tf),
        in_specs=[pl.BlockSpec((tm, D), lambda i, j: (i, 0)),
                  pl.BlockSpec((D, tf), lambda i, j: (0, j)),
                  pl.BlockSpec((D, tf), lambda i, j: (0, j)),
                  pl.BlockSpec((tf, D), lambda i, j: (j, 0)),
                  pl.BlockSpec((1, D), lambda i, j: (0, 0)),
                  pl.BlockSpec((1, D), lambda i, j: (0, 0))],
        out_specs=pl.BlockSpec((tm, D), lambda i, j: (i, 0)),
        out_shape=jax.ShapeDtypeStruct((m, D), F32),
        scratch_shapes=[pltpu.VMEM((tm, D), BF16), pltpu.VMEM((tm, D), F32)],
        compiler_params=_cp(2),
        name="ffn_ln",
    )(x, wg, wu, wd, g.reshape(1, D), b.reshape(1, D))


def _memattn_kernel(q_ref, k_ref, v_ref, o_ref):
    q = q_ref[...]
    k = k_ref[...]
    v = v_ref[...]
    for h in range(MEM_HEADS):
        sl = slice(h * LANES, (h + 1) * LANES)
        s = _dot_nt(q[:, sl], k[:, sl]) * LANES ** -0.5
        s = s - jnp.max(s, axis=-1, keepdims=True)
        p = jnp.exp(s)
        p = p / jnp.sum(p, axis=-1, keepdims=True)
        o_ref[:, sl] = _dot(p, v[:, sl])


def mem_attention(u, q_col, mem_k, k_col, mem_v, v_col, tq):
    b, l, _ = u.shape
    return pl.pallas_call(
        _memattn_kernel,
        grid=(b, l // tq),
        in_specs=[pl.BlockSpec((None, tq, MEM_Q), lambda bi, i: (bi, i, q_col)),
                  pl.BlockSpec((None, MEM_TOKENS, MEM_Q), lambda bi, i: (bi, 0, k_col)),
                  pl.BlockSpec((None, MEM_TOKENS, MEM_Q), lambda bi, i: (bi, 0, v_col))],
        out_specs=pl.BlockSpec((None, tq, MEM_Q), lambda bi, i: (bi, i, 0)),
        out_shape=jax.ShapeDtypeStruct((b, l, MEM_Q), F32),
        compiler_params=_cp(2),
        name="mem_attention",
    )(u, mem_k, mem_v)


def _mlstm_kernel(qk_ref, v_ref, og_ref, gc_ref, gt_ref, bgc_ref, bgr_ref, ng_ref, c0_ref, n0_ref, m0_ref,
                  mix_ref, co_ref, no_ref, mo_ref, c_scr, n_scr, m_scr, *, T, nc, n_valid, sps):
    s = pl.program_id(0)

    @pl.when(s % sps == 0)
    def _():
        c_scr[...] = c0_ref[...]
        n_scr[...] = n0_ref[...]
        m_scr[...] = m0_ref[...]

    row = lax.broadcasted_iota(jnp.int32, (T, T), 0)
    col = lax.broadcasted_iota(jnp.int32, (T, T), 1)
    causal = col <= row
    tril = causal.astype(F32)
    triu = (row <= col).astype(F32)

    def chunk(c, carry):
        r0 = pl.multiple_of(c * T, T)
        rows = pl.ds(r0, T)
        g = gc_ref[rows, :] + bgc_ref[...]
        lfc = _log_sigmoid(g)
        gt = gt_ref[c] + bgr_ref[...]
        igr_all = gt[0:H, :]
        lfr = _log_sigmoid(gt[H:2 * H, :])
        if n_valid < T:
            rid = lax.broadcasted_iota(jnp.int32, (T, LANES), 0)
            g = jnp.where(rid < n_valid, g, NEG)
            lfc = jnp.where(rid < n_valid, lfc, 0.0)
            cid = lax.broadcasted_iota(jnp.int32, (H, T), 1)
            igr_all = jnp.where(cid < n_valid, igr_all, NEG)
            lfr = jnp.where(cid < n_valid, lfr, 0.0)
        bcol = _dot_hi(tril, lfc)
        brow = _dot_hi(lfr, triu)
        qk = qk_ref[rows, :]
        vv = v_ref[rows, :]
        og = og_ref[rows, :]
        hs = range(H)
        m_all = m_scr[...]
        bc = [bcol[:, H + h:H + h + 1] for h in hs]
        q = [(qk[:, h * 64:(h + 1) * 64] * 0.125).astype(BF16) for h in hs]
        k = [qk[:, 512 + h * 64:512 + (h + 1) * 64] for h in hs]
        kb = [k[h].astype(BF16) for h in hs]
        v = [vv[:, h * LANES:(h + 1) * LANES] for h in hs]
        c_old = [c_scr[h] for h in hs]
        n_old = [n_scr[h:h + 1, :] for h in hs]
        qkt = [lax.dot_general(q[h], kb[h], NT, preferred_element_type=F32) for h in hs]
        qc = [lax.dot_general(q[h], c_old[h].astype(BF16), NT, preferred_element_type=F32) for h in hs]
        log_d = [jnp.where(causal, bc[h] - brow[h:h + 1, :] + igr_all[h:h + 1, :], -jnp.inf) for h in hs]
        inter = [bc[h] + m_all[:, h:h + 1] for h in hs]
        m_t = [jnp.maximum(inter[h], jnp.max(log_d[h], axis=-1, keepdims=True)) for h in hs]
        w_inter = [jnp.exp(inter[h] - m_t[h]) for h in hs]
        sc = [qkt[h] * jnp.exp(log_d[h] - m_t[h]) for h in hs]
        sv = [jnp.dot(sc[h].astype(BF16), v[h].astype(BF16), preferred_element_type=F32) for h in hs]
        m_last = [m_t[h][T - 1:T, :] for h in hs]
        w_end = [jnp.exp(bc[h][T - 1:T, :] - bc[h] + g[:, h:h + 1] - m_last[h]) for h in hs]
        vk = [lax.dot_general((w_end[h] * v[h]).astype(BF16), kb[h], TN, preferred_element_type=F32) for h in hs]
        for h in hs:
            num = sv[h] + w_inter[h] * qc[h]
            den = (jnp.sum(sc[h], axis=-1, keepdims=True)
                   + w_inter[h] * jnp.sum(q[h].astype(F32) * n_old[h], axis=-1, keepdims=True))
            hh = num / jnp.maximum(jnp.abs(den), jnp.exp(-m_t[h]))
            sl = slice(h * LANES, (h + 1) * LANES)
            mix_ref[rows, sl] = _sigmoid(og[:, sl]) * (_rms(hh) * ng_ref[:, sl])
            decay = w_inter[h][T - 1:T, :]
            c_scr[h] = decay * c_old[h] + vk[h]
            n_scr[h:h + 1, :] = decay * n_old[h] + jnp.sum(w_end[h] * k[h], axis=0, keepdims=True)
            m_scr[:, h:h + 1] = m_last[h]
        return carry

    lax.fori_loop(0, nc, chunk, 0)

    @pl.when(s % sps == sps - 1)
    def _():
        co_ref[...] = c_scr[...]
        no_ref[...] = n_scr[...]
        mo_ref[...] = m_scr[...]


def mlstm(u, gates, b_gate, norm_g, c0, n0, m0, nseq, seqlen, T, rows_per_step, n_valid):
    m = u.shape[0]
    nc = rows_per_step // T
    sps = seqlen // rows_per_step
    gt = jnp.transpose(gates[:, :2 * H].reshape(m // T, T, 2 * H), (0, 2, 1))
    bgc = jnp.zeros((1, LANES), F32).at[0, :2 * H].set(b_gate.reshape(-1))
    bgr = b_gate.reshape(2 * H, 1)
    r = rows_per_step
    kern = functools.partial(_mlstm_kernel, T=T, nc=nc, n_valid=n_valid, sps=sps)
    return pl.pallas_call(
        kern,
        grid=(m // r,),
        in_specs=[pl.BlockSpec((r, D), lambda s: (s, 0)),
                  pl.BlockSpec((r, D), lambda s: (s, 1)),
                  pl.BlockSpec((r, D), lambda s: (s, 2)),
                  pl.BlockSpec((r, LANES), lambda s: (s, 0)),
                  pl.BlockSpec((nc, 2 * H, T), lambda s: (s, 0, 0)),
                  pl.BlockSpec((1, LANES), lambda s: (0, 0)),
                  pl.BlockSpec((2 * H, 1), lambda s: (0, 0)),
                  pl.BlockSpec((1, D), lambda s: (0, 0)),
                  pl.BlockSpec((None, H, 128, 64), lambda s: (s // sps, 0, 0, 0)),
                  pl.BlockSpec((None, H, 64), lambda s: (s // sps, 0, 0)),
                  pl.BlockSpec((None, 1, H), lambda s: (s // sps, 0, 0))],
        out_specs=[pl.BlockSpec((r, D), lambda s: (s, 0)),
                   pl.BlockSpec((None, H, 128, 64), lambda s: (s // sps, 0, 0, 0)),
                   pl.BlockSpec((None, H, 64), lambda s: (s // sps, 0, 0)),
                   pl.BlockSpec((None, 1, H), lambda s: (s // sps, 0, 0))],
        out_shape=[jax.ShapeDtypeStruct((m, D), F32),
                   jax.ShapeDtypeStruct((nseq, H, 128, 64), F32),
                   jax.ShapeDtypeStruct((nseq, H, 64), F32),
                   jax.ShapeDtypeStruct((nseq, 1, H), F32)],
        scratch_shapes=[pltpu.VMEM((H, 128, 64), F32), pltpu.VMEM((H, 64), F32), pltpu.VMEM((1, H), F32)],
        compiler_params=_cp(1),
        name="mlstm",
    )(u, u, u, gates, gt, bgc, bgr, norm_g.reshape(1, D), c0, n0, m0.reshape(nseq, 1, H))


def _gdn_kernel(x_ref, z_ref, gc_ref, gt_ref, cw_ref, pc_ref, pr_ref, ng_ref, s0_ref, cv0_ref,
                mix_ref, so_ref, xbuf, cv_scr, s_scr, *, T, nc, n_valid, sps):
    s = pl.program_id(0)
    R = T * nc

    @pl.when(s % sps == 0)
    def _():
        s_scr[...] = s0_ref[...]
        xbuf[5:8, :] = cv0_ref[...]

    xbuf[8:8 + R, :] = x_ref[...]
    conv = (cw_ref[0:1, :] * xbuf[5:5 + R, :] + cw_ref[1:2, :] * xbuf[6:6 + R, :]
            + cw_ref[2:3, :] * xbuf[7:7 + R, :] + cw_ref[3:4, :] * xbuf[8:8 + R, :])
    cv_scr[...] = _silu(conv)
    xbuf[0:8, :] = xbuf[R:R + 8, :]

    row = lax.broadcasted_iota(jnp.int32, (T, T), 0)
    col = lax.broadcasted_iota(jnp.int32, (T, T), 1)
    causal = col <= row
    strict = col < row
    tril = causal.astype(F32)
    triu = (row <= col).astype(F32)

    def chunk(c, carry):
        r0 = pl.multiple_of(c * T, T)
        rows = pl.ds(r0, T)
        gpre = gc_ref[rows, :]
        beta_all = _sigmoid(gpre)
        g_all = -jnp.exp(pc_ref[0:1, :]) * _softplus(gpre + pc_ref[1:2, :])
        gt = gt_ref[c]
        gr_all = -jnp.exp(pr_ref[:, 0:1]) * _softplus(gt + pr_ref[:, 1:2])
        gr = gr_all[H:2 * H, :]
        if n_valid < T:
            rid = lax.broadcasted_iota(jnp.int32, (T, LANES), 0)
            beta_all = jnp.where(rid < n_valid, beta_all, 0.0)
            g_all = jnp.where(rid < n_valid, g_all, 0.0)
            cid = lax.broadcasted_iota(jnp.int32, (H, T), 1)
            gr = jnp.where(cid < n_valid, gr, 0.0)
        gamc_all = _dot_hi(tril, g_all)
        gamr_all = _dot_hi(gr, triu)
        hs = range(H)
        qf = [cv_scr[rows, h * LANES:(h + 1) * LANES] for h in hs]
        kf = [cv_scr[rows, D + h * LANES:D + (h + 1) * LANES] for h in hs]
        vf = [cv_scr[rows, 2 * D + h * LANES:2 * D + (h + 1) * LANES] for h in hs]
        qb = [(qf[h] * lax.rsqrt(jnp.sum(qf[h] * qf[h], axis=-1, keepdims=True) + EPS) * LANES ** -0.5).astype(BF16)
              for h in hs]
        kn = [kf[h] * lax.rsqrt(jnp.sum(kf[h] * kf[h], axis=-1, keepdims=True) + EPS) for h in hs]
        kb = [kn[h].astype(BF16) for h in hs]
        beta = [beta_all[:, h:h + 1] for h in hs]
        gamc = [gamc_all[:, H + h:H + h + 1] for h in hs]
        egam = [jnp.exp(gamc[h]) for h in hs]
        dec = [jnp.exp(jnp.where(causal, gamc[h] - gamr_all[h:h + 1, :], NEG)) for h in hs]
        s_old = [s_scr[h] for h in hs]
        sb = [s_old[h].astype(BF16) for h in hs]
        qkk = [lax.dot_general(jnp.concatenate([qb[h], kb[h]], axis=0), kb[h], NT, preferred_element_type=F32)
               for h in hs]
        qs = [lax.dot_general(qb[h], sb[h], NT, preferred_element_type=F32) for h in hs]
        nmat = [jnp.where(strict, beta[h] * qkk[h][T:2 * T] * dec[h], 0.0) for h in hs]
        y = [-nmat[h] for h in hs]
        nsp = [_split_bf16(nmat[h]) for h in hs]
        span = 2
        while span < T:
            npow = [_dot_split(nsp[h][0], nsp[h][1], nsp[h][0], nsp[h][1]) for h in hs]
            nsp = [_split_bf16(npow[h]) for h in hs]
            ysp = [_split_bf16(y[h]) for h in hs]
            y = [y[h] + npow[h] + _dot_split(ysp[h][0], ysp[h][1], nsp[h][0], nsp[h][1]) for h in hs]
            span *= 2
        rhs = [jnp.concatenate([beta[h] * vf[h], (beta[h] * egam[h]) * kn[h]], axis=1) for h in hs]
        sol = [rhs[h] + _dot(y[h], rhs[h]) for h in hs]
        delta = [sol[h][:, 0:LANES] - _dot_nt(sol[h][:, LANES:2 * LANES], sb[h]) for h in hs]
        o = [egam[h] * qs[h] + _dot(qkk[h][0:T] * dec[h], delta[h]) for h in hs]
        g_end = [gamc[h][T - 1:T, :] for h in hs]
        upd = [_dot_tn(jnp.exp(g_end[h] - gamc[h]) * delta[h], kb[h]) for h in hs]
        for h in hs:
            sl = slice(h * LANES, (h + 1) * LANES)
            s_scr[h] = jnp.exp(g_end[h]) * s_old[h] + upd[h]
            mix_ref[rows, sl] = _rms(o[h]) * ng_ref[:, sl] * _silu(z_ref[rows, sl])
        return carry

    lax.fori_loop(0, nc, chunk, 0)

    @pl.when(s % sps == sps - 1)
    def _():
        so_ref[...] = s_scr[...]


def gdn(u, gates, conv_w, a_log, dt_bias, norm_g, s0, conv0, nseq, seqlen, T, rows_per_step, n_valid):
    m = u.shape[0]
    nc = rows_per_step // T
    sps = seqlen // rows_per_step
    r = rows_per_step
    gt = jnp.transpose(gates[:, :2 * H].reshape(m // T, T, 2 * H), (0, 2, 1))
    pc = jnp.zeros((2, LANES), F32).at[0, H:2 * H].set(a_log).at[1, H:2 * H].set(dt_bias)
    pr = jnp.zeros((2 * H, 2), F32).at[H:, 0].set(a_log).at[H:, 1].set(dt_bias)
    kern = functools.partial(_gdn_kernel, T=T, nc=nc, n_valid=n_valid, sps=sps)
    return pl.pallas_call(
        kern,
        grid=(m // r,),
        in_specs=[pl.BlockSpec((r, 3 * D), lambda s: (s, 0)),
                  pl.BlockSpec((r, D), lambda s: (s, 3)),
                  pl.BlockSpec((r, LANES), lambda s: (s, 0)),
                  pl.BlockSpec((nc, 2 * H, T), lambda s: (s, 0, 0)),
                  pl.BlockSpec((4, 3 * D), lambda s: (0, 0)),
                  pl.BlockSpec((2, LANES), lambda s: (0, 0)),
                  pl.BlockSpec((2 * H, 2), lambda s: (0, 0)),
                  pl.BlockSpec((1, D), lambda s: (0, 0)),
                  pl.BlockSpec((None, H, 128, 128), lambda s: (s // sps, 0, 0, 0)),
                  pl.BlockSpec((None, 3, 3 * D), lambda s: (s // sps, 0, 0))],
        out_specs=[pl.BlockSpec((r, D), lambda s: (s, 0)),
                   pl.BlockSpec((None, H, 128, 128), lambda s: (s // sps, 0, 0, 0))],
        out_shape=[jax.ShapeDtypeStruct((m, D), F32),
                   jax.ShapeDtypeStruct((nseq, H, 128, 128), F32)],
        scratch_shapes=[pltpu.VMEM((r + 8, 3 * D), F32), pltpu.VMEM((r, 3 * D), F32),
                        pltpu.VMEM((H, 128, 128), F32)],
        compiler_params=_cp(1),
        name="gdn",
    )(u, u, gates, gt, conv_w, pc, pr, norm_g.reshape(1, D), s0, conv0)


def _bucket(rel):
    n = jnp.maximum(rel, 0)
    nf = jnp.maximum(n, 1).astype(F32)
    large = 16 + (jnp.log(nf / 16) / math.log(128 / 16) * 16).astype(jnp.int32)
    return jnp.where(n < 16, n, jnp.minimum(large, 31))


def _diff_lambda(lw_ref, lam_init):
    lw = lw_ref[...]
    a = jnp.sum(lw[0:1, :] * lw[1:2, :], axis=-1, keepdims=True)
    b = jnp.sum(lw[2:3, :] * lw[3:4, :], axis=-1, keepdims=True)
    return jnp.exp(a) - jnp.exp(b) + lam_init


def _dattn_p_kernel(q_ref, k_ref, v_ref, bn_ref, lw_ref, ng_ref, o_ref, m_scr, acc_scr, *, TB, lam_init):
    i = pl.program_id(2)
    W = 2 * TB
    lane = lax.broadcasted_iota(jnp.int32, (TB, LANES), 1)
    q = q_ref[...] * 0.125
    qq = jnp.concatenate([jnp.where(lane < 64, q, 0.0), jnp.where(lane >= 64, q, 0.0)], axis=0).astype(BF16)
    m_scr[...] = jnp.full_like(m_scr, NEG)
    acc_scr[...] = jnp.zeros_like(acc_scr)

    def step(k0, width, bias):
        rows = pl.ds(pl.multiple_of(k0, TB), width)
        kb = k_ref[rows, :].astype(BF16)
        vext = jnp.concatenate([v_ref[rows, :].astype(BF16), jnp.ones((width, LANES), BF16)], axis=1)
        sc = lax.dot_general(qq, kb, NT, preferred_element_type=F32)
        if bias is not None:
            sc = sc + bias
        ng = width // LANES
        mloc = sc[:, 0:LANES]
        for g in range(1, ng):
            mloc = jnp.maximum(mloc, sc[:, g * LANES:(g + 1) * LANES])
        m_old = m_scr[...]
        m_new = jnp.maximum(m_old, jnp.max(mloc, axis=-1, keepdims=True))
        alpha = jnp.exp(m_old - m_new)
        p = jnp.exp(sc - jnp.concatenate([m_new] * ng, axis=1)).astype(BF16)
        acc_scr[...] = (jnp.concatenate([alpha, alpha], axis=1) * acc_scr[...]
                        + jnp.dot(p, vext, preferred_element_type=F32))
        m_scr[...] = m_new

    n_far = jnp.maximum(i - 1, 0)
    n_big = n_far // 2

    def body(c, carry):
        step(c * W, W, None)
        return carry

    lax.fori_loop(0, n_big, body, 0)

    @pl.when(n_far % 2 == 1)
    def _():
        step(n_big * W, TB, None)

    step(n_far * TB, W, bn_ref[...])
    lam = _diff_lambda(lw_ref, lam_init)
    acc = acc_scr[...]
    o = (acc[0:TB, 0:LANES] / acc[0:TB, LANES:2 * LANES]
         - lam * (acc[TB:2 * TB, 0:LANES] / acc[TB:2 * TB, LANES:2 * LANES]))
    o_ref[...] = _rms(o) * ng_ref[...] * (1.0 - lam_init)


def _bias_table(rel_bias, rel):
    onehot = (_bucket(rel)[..., None] == jnp.arange(32)).astype(F32)
    return jnp.einsum('...k,kh->h...', onehot, rel_bias - rel_bias[31:32], precision=HI)


def diff_attn_prompt(u, lam_w, norm_g, rel_bias, lam_init, TB=256):
    nq = SEQ // TB
    ii = jnp.arange(TB)
    rel_d = ii[:, None] - ii[None, :]
    bd = jnp.where((rel_d >= 0)[None], _bias_table(rel_bias, rel_d), NEG)
    bs = _bias_table(rel_bias, rel_d + TB)
    masked = jnp.full((H, TB, TB), NEG, F32)
    bn = jnp.stack([jnp.concatenate([bd, masked], axis=2), jnp.concatenate([bs, bd], axis=2)], axis=1)
    bn = jnp.concatenate([bn, bn], axis=2)
    kern = functools.partial(_dattn_p_kernel, TB=TB, lam_init=lam_init)
    return pl.pallas_call(
        kern,
        grid=(BATCH, H, nq),
        in_specs=[pl.BlockSpec((TB, LANES), lambda b, h, i: (b * nq + i, h)),
                  pl.BlockSpec((SEQ, LANES), lambda b, h, i: (b, H + h)),
                  pl.BlockSpec((SEQ, LANES), lambda b, h, i: (b, 2 * H + h)),
                  pl.BlockSpec((None, None, 2 * TB, 2 * TB), lambda b, h, i: (h, jnp.minimum(i, 1), 0, 0)),
                  pl.BlockSpec((4, 64), lambda b, h, i: (0, 0)),
                  pl.BlockSpec((1, LANES), lambda b, h, i: (0, h))],
        out_specs=pl.BlockSpec((TB, LANES), lambda b, h, i: (b * nq + i, h)),
        out_shape=jax.ShapeDtypeStruct((BATCH * SEQ, D), F32),
        scratch_shapes=[pltpu.VMEM((2 * TB, LANES), F32), pltpu.VMEM((2 * TB, 2 * LANES), F32)],
        compiler_params=_cp(3),
        name="diff_attn_prompt",
    )(u, u, u, bn, lam_w, norm_g.reshape(1, D))


def _dattn_s_kernel(pt_ref, qm_ref, qb_ref, *refs, lam_init):
    del pt_ref
    kp_refs = refs[0:PAGES_PER_STEP]
    vp_refs = refs[PAGES_PER_STEP:2 * PAGES_PER_STEP]
    kn_ref, vn_ref, bl_ref, bn_ref, lw_ref, ng_ref, o_ref, m_scr, l_scr, acc_scr = refs[2 * PAGES_PER_STEP:]
    p = pl.program_id(1)
    last = pl.num_programs(1) - 1
    rows_h = 2 * DEC_PAD

    @pl.when(p == 0)
    def _():
        m_scr[...] = jnp.full_like(m_scr, NEG)
        l_scr[...] = jnp.zeros_like(l_scr)
        acc_scr[...] = jnp.zeros_like(acc_scr)

    def head_rows(page_refs, h):
        return jnp.concatenate([r[pl.ds(h, PAGE, stride=H), :].astype(BF16) for r in page_refs], axis=0)

    qm = qm_ref[...].astype(BF16)
    sc = jnp.concatenate(
        [lax.dot_general(qm[h * rows_h:(h + 1) * rows_h], head_rows(kp_refs, h), NT, preferred_element_type=F32)
         for h in range(H)], axis=0)
    sc = sc + jnp.where(p == last, bl_ref[...], 0.0)
    m_old = m_scr[...]
    m_new = jnp.maximum(m_old, jnp.max(sc, axis=-1, keepdims=True))
    alpha = jnp.exp(m_old - m_new)
    pr = jnp.exp(sc - jnp.concatenate([m_new] * PAGES_PER_STEP, axis=1))
    l_scr[...] = alpha * l_scr[...] + jnp.sum(pr, axis=-1, keepdims=True)
    prb = pr.astype(BF16)
    pv = jnp.concatenate(
        [jnp.dot(prb[h * rows_h:(h + 1) * rows_h], head_rows(vp_refs, h), preferred_element_type=F32)
         for h in range(H)], axis=0)
    acc_scr[...] = alpha * acc_scr[...] + pv
    m_scr[...] = m_new

    @pl.when(p == last)
    def _():
        scn = lax.dot_general(qb_ref[...], kn_ref[...], NT, preferred_element_type=F32) + bn_ref[...]
        m_o = m_scr[...]
        m_n = jnp.maximum(m_o, jnp.max(scn, axis=-1, keepdims=True))
        al = jnp.exp(m_o - m_n)
        pn = jnp.exp(scn - m_n[:, 0:DEC_PAD])
        l_n = al * l_scr[...] + jnp.sum(pn, axis=-1, keepdims=True)
        pvn = jnp.dot(pn, vn_ref[...], preferred_element_type=F32)
        lam = _diff_lambda(lw_ref, lam_init)
        for h in range(H):
            sl = slice(h * LANES, (h + 1) * LANES)
            r0 = slice(h * rows_h, h * rows_h + DEC_PAD)
            r1 = slice(h * rows_h + DEC_PAD, (h + 1) * rows_h)
            a0 = (al[r0] * acc_scr[r0, :] + pvn[r0, sl]) / l_n[r0]
            a1 = (al[r1] * acc_scr[r1, :] + pvn[r1, sl]) / l_n[r1]
            o_ref[:, sl] = _rms(a0 - lam * a1) * ng_ref[:, sl] * (1.0 - lam_init)


def diff_attn_sample(u, cache_k, cache_v, page_table, lam_w, norm_g, rel_bias, lam_init):
    nb = DEC_BATCH
    nr = 2 * H * DEC_PAD
    q = u[:, :D].reshape(nb, DEC_PAD, D) * 0.125
    grp_row = jnp.arange(2 * H)[:, None, None]
    grp_col = (jnp.arange(D) // 64)[None, None, :]
    qbig = jnp.where(grp_row == grp_col, q[:, None, :, :], 0.0).reshape(nb, nr, D)
    q4 = jnp.transpose(q.reshape(nb, DEC_PAD, H, 1, LANES), (0, 2, 3, 1, 4))
    comp = (jnp.arange(LANES) // 64)[None, None, None, None, :]
    qm = jnp.where(comp == jnp.arange(2)[None, None, :, None, None], q4, 0.0).reshape(nb, nr, LANES)
    qi = jnp.arange(DEC_PAD)
    rel_last = PAGE + qi[:, None] - jnp.arange(PAGE)[None, :]
    b_last = jnp.broadcast_to(_bias_table(rel_bias, rel_last)[:, None], (H, 2, DEC_PAD, PAGE)).reshape(nr, PAGE)
    b_last = jnp.pad(b_last, ((0, 0), ((PAGES_PER_STEP - 1) * PAGE, 0)))
    rel_new = qi[:, None] - qi[None, :]
    vis = (rel_new >= 0) & (qi[None, :] < DEC_SEQ)
    b_new = jnp.where(vis[None], _bias_table(rel_bias, rel_new), NEG)
    b_new = jnp.broadcast_to(b_new[:, None], (H, 2, DEC_PAD, DEC_PAD)).reshape(nr, DEC_PAD)
    kern = functools.partial(_dattn_s_kernel, lam_init=lam_init)
    def page_spec(slot):
        return pl.BlockSpec((None, PAGE * H, LANES), lambda b, p, pt: (pt[b, p * PAGES_PER_STEP + slot], 0, 0))

    pages = [page_spec(slot) for slot in range(PAGES_PER_STEP)]
    grid_spec = pltpu.PrefetchScalarGridSpec(
        num_scalar_prefetch=1,
        grid=(nb, N_PAGES // PAGES_PER_STEP),
        in_specs=[pl.BlockSpec((None, nr, LANES), lambda b, p, pt: (b, 0, 0)),
                  pl.BlockSpec((None, nr, D), lambda b, p, pt: (b, 0, 0))] + pages + pages + [
                  pl.BlockSpec((DEC_PAD, D), lambda b, p, pt: (b, 1)),
                  pl.BlockSpec((DEC_PAD, D), lambda b, p, pt: (b, 2)),
                  pl.BlockSpec((nr, PAGES_PER_STEP * PAGE), lambda b, p, pt: (0, 0)),
                  pl.BlockSpec((nr, DEC_PAD), lambda b, p, pt: (0, 0)),
                  pl.BlockSpec((4, 64), lambda b, p, pt: (0, 0)),
                  pl.BlockSpec((1, D), lambda b, p, pt: (0, 0))],
        out_specs=pl.BlockSpec((DEC_PAD, D), lambda b, p, pt: (b, 0)),
        scratch_shapes=[pltpu.VMEM((nr, LANES), F32), pltpu.VMEM((nr, LANES), F32), pltpu.VMEM((nr, LANES), F32)],
    )
    return pl.pallas_call(
        kern,
        grid_spec=grid_spec,
        out_shape=jax.ShapeDtypeStruct((nb * DEC_PAD, D), F32),
        compiler_params=_cp(2),
        name="diff_attn_sample",
    )(page_table, qm, qbig, *([cache_k] * PAGES_PER_STEP), *([cache_v] * PAGES_PER_STEP), u, u, b_last, b_new,
      lam_w, norm_g.reshape(1, D))


def _router_kernel(x_ref, w_ref, o_ref):
    logits = _dot_hi(x_ref[...], w_ref[...])
    lane = lax.broadcasted_iota(jnp.int32, logits.shape, 1)
    logits = jnp.where(lane < N_EXPERTS, logits, -jnp.inf)
    m1 = jnp.max(logits, axis=-1, keepdims=True)
    i1 = jnp.min(jnp.where(logits == m1, lane, LANES), axis=-1, keepdims=True)
    rest = jnp.where(lane == i1, -jnp.inf, logits)
    m2 = jnp.max(rest, axis=-1, keepdims=True)
    i2 = jnp.min(jnp.where(rest == m2, lane, LANES), axis=-1, keepdims=True)
    e2 = jnp.exp(m2 - m1)
    g1 = 1.0 / (1.0 + e2)
    g2 = e2 / (1.0 + e2)
    o_ref[...] = jnp.where(lane == 0, i1.astype(F32),
                           jnp.where(lane == 1, i2.astype(F32),
                                     jnp.where(lane == 2, g1, jnp.where(lane == 3, g2, 0.0))))


def router(x, w_router, tm):
    m = x.shape[0]
    w = jnp.pad(w_router, ((0, 0), (0, LANES - N_EXPERTS)))
    return pl.pallas_call(
        _router_kernel,
        grid=(m // tm,),
        in_specs=[pl.BlockSpec((tm, D), lambda i: (i, 0)),
                  pl.BlockSpec((D, LANES), lambda i: (0, 0))],
        out_specs=pl.BlockSpec((tm, LANES), lambda i: (i, 0)),
        out_shape=jax.ShapeDtypeStruct((m, LANES), F32),
        compiler_params=_cp(1),
        name="router",
    )(x, w)


def _gather_kernel(src_ref, x_hbm, o_ref, sem, *, G):
    base = pl.program_id(0) * G

    def issue(r, carry):
        pltpu.make_async_copy(x_hbm.at[pl.ds(src_ref[base + r], 1), :], o_ref.at[pl.ds(r, 1), :], sem).start()
        return carry

    lax.fori_loop(0, G, issue, 0, unroll=8)

    def drain(r, carry):
        pltpu.make_async_copy(x_hbm.at[pl.ds(0, 1), :], o_ref.at[pl.ds(r, 1), :], sem).wait()
        return carry

    lax.fori_loop(0, G, drain, 0, unroll=8)


def gather_rows(x, src, G):
    n = src.shape[0]
    grid_spec = pltpu.PrefetchScalarGridSpec(
        num_scalar_prefetch=1,
        grid=(n // G,),
        in_specs=[pl.BlockSpec(memory_space=pl.ANY)],
        out_specs=pl.BlockSpec((G, D), lambda i, s: (i, 0)),
        scratch_shapes=[pltpu.SemaphoreType.DMA(())],
    )
    return pl.pallas_call(
        functools.partial(_gather_kernel, G=G),
        grid_spec=grid_spec,
        out_shape=jax.ShapeDtypeStruct((n, D), x.dtype),
        compiler_params=_cp(1),
        name="gather_rows",
    )(src, x)


def _expert_kernel(be_ref, nu_ref, x_ref, wg_ref, wu_ref, wd_ref, o_ref, xb_scr, acc_scr):
    i = pl.program_id(0)
    j = pl.program_id(1)
    valid = i < nu_ref[0]

    @pl.when(valid & (j == 0))
    def _():
        xb_scr[...] = x_ref[...].astype(BF16)
        acc_scr[...] = jnp.zeros_like(acc_scr)

    @pl.when(valid)
    def _():
        xb = xb_scr[...]
        gate = jnp.dot(xb, wg_ref[...].astype(BF16), preferred_element_type=F32)
        up = jnp.dot(xb, wu_ref[...].astype(BF16), preferred_element_type=F32)
        acc_scr[...] += _dot(_silu(gate) * up, wd_ref[...])

    @pl.when(j == pl.num_programs(1) - 1)
    def _():
        o_ref[...] = jnp.where(valid, acc_scr[...], 0.0)


def expert_ffn(xs, w_gu, w_down, blk_e, n_used, tm, tf=512):
    n = xs.shape[0]
    nf = D_FF_EXPERT // tf

    def jeff(i, j, nu):
        return jnp.where(i < nu[0], j, nf - 1)

    grid_spec = pltpu.PrefetchScalarGridSpec(
        num_scalar_prefetch=2,
        grid=(n // tm, nf),
        in_specs=[pl.BlockSpec((tm, D), lambda i, j, be, nu: (i, 0)),
                  pl.BlockSpec((None, D, tf), lambda i, j, be, nu: (be[i], 0, jeff(i, j, nu))),
                  pl.BlockSpec((None, D, tf), lambda i, j, be, nu: (be[i], 0, nf + jeff(i, j, nu))),
                  pl.BlockSpec((None, tf, D), lambda i, j, be, nu: (be[i], jeff(i, j, nu), 0))],
        out_specs=pl.BlockSpec((tm, D), lambda i, j, be, nu: (i, 0)),
        scratch_shapes=[pltpu.VMEM((tm, D), BF16), pltpu.VMEM((tm, D), F32)],
    )
    return pl.pallas_call(
        _expert_kernel,
        grid_spec=grid_spec,
        out_shape=jax.ShapeDtypeStruct((n, D), F32),
        compiler_params=_cp(2),
        name="expert_ffn",
    )(blk_e, n_used, xs, w_gu, w_gu, w_down)


def _combine_kernel(d0_ref, d1_ref, yb_hbm, x_ref, rt_ref, g_ref, b_ref, o_ref, buf0, buf1, sem, *, G):
    base = pl.program_id(0) * G

    def issue(r, carry):
        pltpu.make_async_copy(yb_hbm.at[pl.ds(d0_ref[base + r], 1), :], buf0.at[pl.ds(r, 1), :], sem).start()
        pltpu.make_async_copy(yb_hbm.at[pl.ds(d1_ref[base + r], 1), :], buf1.at[pl.ds(r, 1), :], sem).start()
        return carry

    lax.fori_loop(0, G, issue, 0, unroll=4)

    def drain(r, carry):
        pltpu.make_async_copy(yb_hbm.at[pl.ds(0, 1), :], buf0.at[pl.ds(r, 1), :], sem).wait()
        pltpu.make_async_copy(yb_hbm.at[pl.ds(0, 1), :], buf1.at[pl.ds(r, 1), :], sem).wait()
        return carry

    lax.fori_loop(0, G, drain, 0, unroll=4)
    rt = rt_ref[...]
    y = rt[:, 2:3] * buf0[...] + rt[:, 3:4] * buf1[...]
    o_ref[...] = _layer_norm(ALPHA * x_ref[...] + y, g_ref[...], b_ref[...])


def combine_ln(yb, x, route, d0, d1, g, b, G):
    m = x.shape[0]
    grid_spec = pltpu.PrefetchScalarGridSpec(
        num_scalar_prefetch=2,
        grid=(m // G,),
        in_specs=[pl.BlockSpec(memory_space=pl.ANY),
                  pl.BlockSpec((G, D), lambda i, a, c: (i, 0)),
                  pl.BlockSpec((G, LANES), lambda i, a, c: (i, 0)),
                  pl.BlockSpec((1, D), lambda i, a, c: (0, 0)),
                  pl.BlockSpec((1, D), lambda i, a, c: (0, 0))],
        out_specs=pl.BlockSpec((G, D), lambda i, a, c: (i, 0)),
        scratch_shapes=[pltpu.VMEM((G, D), F32), pltpu.VMEM((G, D), F32), pltpu.SemaphoreType.DMA(())],
    )
    return pl.pallas_call(
        functools.partial(_combine_kernel, G=G),
        grid_spec=grid_spec,
        out_shape=jax.ShapeDtypeStruct((m, D), F32),
        compiler_params=_cp(1),
        name="combine_ln",
    )(d0, d1, yb, x, route, g.reshape(1, D), b.reshape(1, D))


def moe_ln(x, w_router, w_gu, w_down, g, b, tm=512):
    m = x.shape[0]
    route = router(x, w_router, 640)
    flat_e = route[:, :2].astype(jnp.int32).reshape(-1)
    onehot = (flat_e[:, None] == jnp.arange(N_EXPERTS)[None, :]).astype(jnp.int32)
    csum = jnp.cumsum(onehot, axis=0)
    counts = csum[-1]
    padded = (counts + tm - 1) // tm * tm
    p_end = jnp.cumsum(padded)
    dest = jnp.sum(onehot * ((p_end - padded)[None, :] + csum - 1), axis=1)
    n_blocks = -(-(2 * m + N_EXPERTS * (tm - 1)) // tm)
    n_rows = n_blocks * tm
    n_used = (p_end[-1] // tm).astype(jnp.int32).reshape(1)
    blk = jnp.minimum(jnp.arange(n_blocks), n_used[0] - 1) * tm
    blk_e = jnp.minimum(jnp.sum((blk[:, None] >= p_end[None, :]).astype(jnp.int32), axis=1), N_EXPERTS - 1)
    x_z = jnp.concatenate([x, jnp.zeros((8, D), x.dtype)], axis=0)
    src = jnp.full((n_rows,), m, jnp.int32).at[dest].set(jnp.arange(2 * m, dtype=jnp.int32) // 2)
    xs = gather_rows(x_z, src, tm)
    yb = expert_ffn(xs, w_gu, w_down, blk_e, n_used, tm)
    d = dest.reshape(m, 2).astype(jnp.int32)
    return combine_ln(yb, x, route, d[:, 0], d[:, 1], g, b, 640)


def kernel(x_prompt, x_sample, mem_prompt, page_table, cache_diff_k, cache_diff_v, cache_mem_k, cache_mem_v,
           state_mlstm_C, state_mlstm_n, state_mlstm_m, state_gdn_S, state_gdn_conv, rel_bias,
           w_in_a, b_gate_a, norm_a, w_in_b, lambda_b, norm_b, w_in_c, conv_c, a_log_c, dt_bias_c, norm_c,
           w_mem_kv, w_out, ln_g, ln_b, w_ffn_gu, w_ffn_down, w_router, w_exp_gu, w_exp_down):
    mp = BATCH * SEQ
    ms = DEC_BATCH * DEC_PAD
    x_p = x_prompt.reshape(mp, D)
    x_s = jnp.pad(x_sample, ((0, 0), (0, DEC_PAD - DEC_SEQ), (0, 0))).reshape(ms, D)
    mem2 = mem_prompt.reshape(BATCH * MEM_TOKENS, D)
    n_phys = cache_diff_k.shape[1]

    def sample_rows(t):
        return t.reshape(DEC_BATCH, DEC_PAD, -1)[:, :DEC_SEQ]

    def pad_gates(w):
        return jnp.pad(w, ((0, 0), (0, LANES - w.shape[1])))

    mlstm_p, mlstm_s, diff_p, diff_s, gdn_p, gdn_s, mem_k_new, mem_v_new = [], [], [], [], [], [], [], []
    for i in range(DEPTH):
        kind, j = i % 3, i // 3
        mem_kv = matmul(mem2, w_mem_kv[i], MEM_TOKENS, 512).reshape(BATCH, MEM_TOKENS, 2 * MEM_Q)
        mem_k_new.append(mem_kv[:, :, :MEM_Q].reshape(BATCH, MEM_TOKENS, MEM_HEADS, 128))
        mem_v_new.append(mem_kv[:, :, MEM_Q:].reshape(BATCH, MEM_TOKENS, MEM_HEADS, 128))
        if kind == 0:
            w = w_in_a[j]
            w_main = jnp.concatenate([w[:, :3 * D], w[:, 3 * D + 2 * H:]], axis=1)
            w_g = pad_gates(w[:, 3 * D:3 * D + 2 * H])
            u_p, u_s = matmul(x_p, w_main, 512, 512), matmul(x_s, w_main, ms, 512)
            g_p, g_s = matmul(x_p, w_g, 512, LANES), matmul(x_s, w_g, ms, LANES)
            zc = jnp.zeros((BATCH, H, 128, 64), F32)
            zn = jnp.zeros((BATCH, H, 64), F32)
            zm = jnp.zeros((BATCH, H), F32)
            mix_p, c_p, n_p, m_p = mlstm(u_p, g_p, b_gate_a[j], norm_a[j], zc, zn, zm,
                                         BATCH, SEQ, CHUNK, 256, CHUNK)
            mix_s, c_s, n_s, m_s = mlstm(u_s, g_s, b_gate_a[j], norm_a[j], state_mlstm_C[j], state_mlstm_n[j],
                                         state_mlstm_m[j], DEC_BATCH, DEC_PAD, DEC_PAD, DEC_PAD, DEC_SEQ)
            mlstm_p.append((c_p, n_p, m_p.reshape(BATCH, H)))
            mlstm_s.append((c_s, n_s, m_s.reshape(DEC_BATCH, H)))
            mq_col = 6
        elif kind == 1:
            lam_init = 0.8 - 0.6 * math.exp(-0.3 * i)
            w_main = w_in_b[j]
            u_p, u_s = matmul(x_p, w_main, 512, 512), matmul(x_s, w_main, ms, 512)
            mix_p = diff_attn_prompt(u_p, lambda_b[j], norm_b[j], rel_bias, lam_init)
            mix_s = diff_attn_sample(u_s, cache_diff_k[j].reshape(n_phys, PAGE * H, LANES),
                                     cache_diff_v[j].reshape(n_phys, PAGE * H, LANES), page_table,
                                     lambda_b[j], norm_b[j], rel_bias, lam_init)
            k_rows, v_rows = kv_head_rows(u_p, 256)
            diff_p.append((k_rows.reshape(BATCH, SEQ, H, 128), v_rows.reshape(BATCH, SEQ, H, 128)))
            diff_s.append((sample_rows(u_s[:, D:2 * D]).reshape(DEC_BATCH, DEC_SEQ, H, 128),
                           sample_rows(u_s[:, 2 * D:3 * D]).reshape(DEC_BATCH, DEC_SEQ, H, 128)))
            mq_col = 6
        else:
            w = w_in_c[j]
            w_main = jnp.concatenate([w[:, :4 * D], w[:, 4 * D + 2 * H:]], axis=1)
            w_g = pad_gates(w[:, 4 * D:4 * D + 2 * H])
            u_p, u_s = matmul(x_p, w_main, 512, 512), matmul(x_s, w_main, ms, 512)
            g_p, g_s = matmul(x_p, w_g, 512, LANES), matmul(x_s, w_g, ms, LANES)
            zs = jnp.zeros((BATCH, H, 128, 128), F32)
            zv = jnp.zeros((BATCH, 3, 3 * D), F32)
            mix_p, s_p = gdn(u_p, g_p, conv_c[j], a_log_c[j], dt_bias_c[j], norm_c[j], zs, zv,
                             BATCH, SEQ, CHUNK, 256, CHUNK)
            mix_s, s_s = gdn(u_s, g_s, conv_c[j], a_log_c[j], dt_bias_c[j], norm_c[j], state_gdn_S[j],
                             state_gdn_conv[j], DEC_BATCH, DEC_PAD, DEC_PAD, DEC_PAD, DEC_SEQ)
            conv_p = u_p[:, :3 * D].reshape(BATCH, SEQ, 3 * D)[:, SEQ - 3:]
            conv_s = sample_rows(u_s[:, :3 * D])[:, DEC_SEQ - 3:]
            gdn_p.append((s_p, conv_p))
            gdn_s.append((s_s, conv_s))
            mq_col = 8
        n_main = u_p.shape[1]
        mo_p = mem_attention(u_p.reshape(BATCH, SEQ, n_main), mq_col, mem_kv, 0, mem_kv, 1, 512).reshape(mp, MEM_Q)
        cmk = cache_mem_k[i].reshape(DEC_BATCH, MEM_TOKENS, MEM_Q)
        cmv = cache_mem_v[i].reshape(DEC_BATCH, MEM_TOKENS, MEM_Q)
        mo_s = mem_attention(u_s.reshape(DEC_BATCH, DEC_PAD, n_main), mq_col, cmk, 0, cmv, 0, DEC_PAD).reshape(ms, MEM_Q)
        x_p = outproj_ln(mix_p, mo_p, x_p, w_out[i], ln_g[i, 0], ln_b[i, 0], 512)
        x_s = outproj_ln(mix_s, mo_s, x_s, w_out[i], ln_g[i, 0], ln_b[i, 0], ms)
        k_f = i // 2
        if i % 2 == 0:
            x_p = ffn_ln(x_p, w_ffn_gu[k_f], w_ffn_down[k_f], ln_g[i, 1], ln_b[i, 1], 1024)
            x_s = ffn_ln(x_s, w_ffn_gu[k_f], w_ffn_down[k_f], ln_g[i, 1], ln_b[i, 1], ms)
        else:
            x_all = moe_ln(jnp.concatenate([x_p, x_s], axis=0), w_router[k_f], w_exp_gu[k_f], w_exp_down[k_f],
                           ln_g[i, 1], ln_b[i, 1])
            x_p, x_s = x_all[:mp], x_all[mp:]

    def stack(lst, k):
        return jnp.stack([t[k] for t in lst])

    return (x_p.reshape(BATCH, SEQ, D), sample_rows(x_s),
            stack(mlstm_p, 0), stack(mlstm_p, 1), stack(mlstm_p, 2),
            stack(mlstm_s, 0), stack(mlstm_s, 1), stack(mlstm_s, 2),
            stack(diff_p, 0), stack(diff_p, 1), stack(diff_s, 0), stack(diff_s, 1),
            stack(gdn_p, 0), stack(gdn_p, 1), stack(gdn_s, 0), stack(gdn_s, 1),
            jnp.stack(mem_k_new), jnp.stack(mem_v_new))
```

```python
import functools
import math

import jax
import jax.numpy as jnp
from jax import lax
from jax.experimental import pallas as pl
from jax.experimental.pallas import tpu as pltpu

F32 = jnp.float32
BF16 = jnp.bfloat16
HI = lax.Precision.HIGHEST
NT = (((1,), (1,)), ((), ()))
TN = (((0,), (0,)), ((), ()))

D = 1024
H = 8
DEPTH = 4
SEQ = 4096
BATCH = 4
DEC_BATCH = 32
DEC_SEQ = 4
DEC_PAD = 8
PAST_LEN = 8192
PAGE = 128
N_PAGES = PAST_LEN // PAGE
PAGES_PER_STEP = 4
MEM_TOKENS = 256
MEM_Q = 512
MEM_HEADS = 4
CHUNK = 64
N_EXPERTS = 8
D_FF = 2752
D_FF_PAD = 2816
D_FF_EXPERT = 3584
EPS = 1e-6
ALPHA = (2 * DEPTH) ** 0.25
NEG = -1e30
LANES = 128
VMEM_LIMIT = 56 * 1024 * 1024


def _cp(n_axes, vmem=VMEM_LIMIT):
    return pltpu.CompilerParams(dimension_semantics=("arbitrary",) * n_axes, vmem_limit_bytes=vmem)


def _dot(a, b):
    return jnp.dot(a.astype(BF16), b.astype(BF16), preferred_element_type=F32)


def _dot_nt(a, b):
    return lax.dot_general(a.astype(BF16), b.astype(BF16), NT, preferred_element_type=F32)


def _dot_tn(a, b):
    return lax.dot_general(a.astype(BF16), b.astype(BF16), TN, preferred_element_type=F32)


def _dot_hi(a, b):
    return jnp.dot(a, b, preferred_element_type=F32, precision=HI)


def _split_bf16(a):
    hi = a.astype(BF16)
    return hi, (a - hi.astype(F32)).astype(BF16)


def _dot_split(a_hi, a_lo, b_hi, b_lo):
    return (jnp.dot(a_hi, b_hi, preferred_element_type=F32)
            + (jnp.dot(a_lo, b_hi, preferred_element_type=F32) + jnp.dot(a_hi, b_lo, preferred_element_type=F32)))


def _sigmoid(x):
    return 1.0 / (1.0 + jnp.exp(-x))


def _silu(x):
    return x * _sigmoid(x)


def _softplus(x):
    return jnp.maximum(x, 0.0) + jnp.log1p(jnp.exp(-jnp.abs(x)))


def _log_sigmoid(x):
    return -_softplus(-x)


def _layer_norm(v, g, b):
    mu = jnp.mean(v, axis=-1, keepdims=True)
    d = v - mu
    var = jnp.mean(d * d, axis=-1, keepdims=True)
    return d * lax.rsqrt(var + EPS) * g + b


def _rms(h):
    return h * lax.rsqrt(jnp.mean(h * h, axis=-1, keepdims=True) + EPS)


def _mm_kernel(x_ref, w_ref, o_ref, *, tn):
    xb = x_ref[...].astype(BF16)
    for n0 in range(0, o_ref.shape[1], tn):
        o_ref[:, n0:n0 + tn] = jnp.dot(xb, w_ref[:, n0:n0 + tn], preferred_element_type=F32)


def matmul(x, w, tm, tn):
    m, k = x.shape
    n = w.shape[1]
    return pl.pallas_call(
        functools.partial(_mm_kernel, tn=min(tn, n)),
        grid=(m // tm,),
        in_specs=[pl.BlockSpec((tm, k), lambda i: (i, 0)),
                  pl.BlockSpec((k, n), lambda i: (0, 0), pipeline_mode=pl.Buffered(1))],
        out_specs=pl.BlockSpec((tm, n), lambda i: (i, 0)),
        out_shape=jax.ShapeDtypeStruct((m, n), F32),
        compiler_params=_cp(1),
        name="matmul",
    )(x, w.astype(BF16))


def _headrows_kernel(k_ref, v_ref, ko_ref, vo_ref):
    tm = k_ref.shape[0]
    for h in range(H):
        sl = slice(h * LANES, (h + 1) * LANES)
        ko_ref[pl.ds(h, tm, stride=H), :] = k_ref[:, sl]
        vo_ref[pl.ds(h, tm, stride=H), :] = v_ref[:, sl]


def kv_head_rows(u, tm):
    m = u.shape[0]
    return pl.pallas_call(
        _headrows_kernel,
        grid=(m // tm,),
        in_specs=[pl.BlockSpec((tm, D), lambda i: (i, 1)),
                  pl.BlockSpec((tm, D), lambda i: (i, 2))],
        out_specs=[pl.BlockSpec((tm * H, LANES), lambda i: (i, 0)),
                   pl.BlockSpec((tm * H, LANES), lambda i: (i, 0))],
        out_shape=[jax.ShapeDtypeStruct((m * H, LANES), F32), jax.ShapeDtypeStruct((m * H, LANES), F32)],
        compiler_params=_cp(1),
        name="kv_head_rows",
    )(u, u)


def _outproj_kernel(mix_ref, mo_ref, x_ref, w1_ref, w2_ref, g_ref, b_ref, o_ref):
    y = _dot(mix_ref[...], w1_ref[...]) + _dot(mo_ref[...], w2_ref[...])
    o_ref[...] = _layer_norm(ALPHA * x_ref[...] + y, g_ref[...], b_ref[...])


def outproj_ln(mix, mo, x, w_out, g, b, tm):
    m = x.shape[0]
    w1 = w_out[:D].astype(BF16)
    w2 = w_out[D:].astype(BF16)
    return pl.pallas_call(
        _outproj_kernel,
        grid=(m // tm,),
        in_specs=[pl.BlockSpec((tm, D), lambda i: (i, 0)),
                  pl.BlockSpec((tm, MEM_Q), lambda i: (i, 0)),
                  pl.BlockSpec((tm, D), lambda i: (i, 0)),
                  pl.BlockSpec((D, D), lambda i: (0, 0)),
                  pl.BlockSpec((MEM_Q, D), lambda i: (0, 0)),
                  pl.BlockSpec((1, D), lambda i: (0, 0)),
                  pl.BlockSpec((1, D), lambda i: (0, 0))],
        out_specs=pl.BlockSpec((tm, D), lambda i: (i, 0)),
        out_shape=jax.ShapeDtypeStruct((m, D), F32),
        compiler_params=_cp(1),
        name="outproj_ln",
    )(mix, mo, x, w1, w2, g.reshape(1, D), b.reshape(1, D))


def _ffn_kernel(x_ref, wg_ref, wu_ref, wd_ref, g_ref, b_ref, o_ref, xb_scr, acc_scr):
    j = pl.program_id(1)

    @pl.when(j == 0)
    def _():
        xb_scr[...] = x_ref[...].astype(BF16)
        acc_scr[...] = jnp.zeros_like(acc_scr)

    xb = xb_scr[...]
    gate = jnp.dot(xb, wg_ref[...], preferred_element_type=F32)
    up = jnp.dot(xb, wu_ref[...], preferred_element_type=F32)
    acc_scr[...] += _dot(_silu(gate) * up, wd_ref[...])

    @pl.when(j == pl.num_programs(1) - 1)
    def _():
        o_ref[...] = _layer_norm(ALPHA * x_ref[...] + acc_scr[...], g_ref[...], b_ref[...])


def ffn_ln(x, w_gu, w_down, g, b, tm, tf=256):
    m = x.shape[0]
    pad = D_FF_PAD - D_FF
    wg = jnp.pad(w_gu[:, :D_FF], ((0, 0), (0, pad))).astype(BF16)
    wu = jnp.pad(w_gu[:, D_FF:], ((0, 0), (0, pad))).astype(BF16)
    wd = jnp.pad(w_down, ((0, pad), (0, 0))).astype(BF16)
    return pl.pallas_call(
        _ffn_kernel,
        grid=(m // tm, D_FF_PAD // tf),
        in_specs=[pl.BlockSpec((tm, D), lambda i, j: (i, 0)),
                  pl.BlockSpec((D, tf), lambda i, j: (0, j)),
                  pl.BlockSpec((D, tf), lambda i, j: (0, j)),
                  pl.BlockSpec((tf, D), lambda i, j: (j, 0)),
                  pl.BlockSpec((1, D), lambda i, j: (0, 0)),
                  pl.BlockSpec((1, D), lambda i, j: (0, 0))],
        out_specs=pl.BlockSpec((tm, D), lambda i, j: (i, 0)),
        out_shape=jax.ShapeDtypeStruct((m, D), F32),
        scratch_shapes=[pltpu.VMEM((tm, D), BF16), pltpu.VMEM((tm, D), F32)],
        compiler_params=_cp(2),
        name="ffn_ln",
    )(x, wg, wu, wd, g.reshape(1, D), b.reshape(1, D))


def _memattn_kernel(q_ref, k_ref, v_ref, o_ref):
    q = q_ref[...]
    k = k_ref[...]
    v = v_ref[...]
    for h in range(MEM_HEADS):
        sl = slice(h * LANES, (h + 1) * LANES)
        s = _dot_nt(q[:, sl], k[:, sl]) * LANES ** -0.5
        s = s - jnp.max(s, axis=-1, keepdims=True)
        p = jnp.exp(s)
        p = p / jnp.sum(p, axis=-1, keepdims=True)
        o_ref[:, sl] = _dot(p, v[:, sl])


def mem_attention(u, q_col, mem_k, k_col, mem_v, v_col, tq):
    b, l, _ = u.shape
    return pl.pallas_call(
        _memattn_kernel,
        grid=(b, l // tq),
        in_specs=[pl.BlockSpec((None, tq, MEM_Q), lambda bi, i: (bi, i, q_col)),
                  pl.BlockSpec((None, MEM_TOKENS, MEM_Q), lambda bi, i: (bi, 0, k_col)),
                  pl.BlockSpec((None, MEM_TOKENS, MEM_Q), lambda bi, i: (bi, 0, v_col))],
        out_specs=pl.BlockSpec((None, tq, MEM_Q), lambda bi, i: (bi, i, 0)),
        out_shape=jax.ShapeDtypeStruct((b, l, MEM_Q), F32),
        compiler_params=_cp(2),
        name="mem_attention",
    )(u, mem_k, mem_v)


def _mlstm_kernel(qk_ref, v_ref, og_ref, gc_ref, gt_ref, bgc_ref, bgr_ref, ng_ref, c0_ref, n0_ref, m0_ref,
                  mix_ref, co_ref, no_ref, mo_ref, c_scr, n_scr, m_scr, *, T, nc, n_valid, sps):
    s = pl.program_id(0)

    @pl.when(s % sps == 0)
    def _():
        c_scr[...] = c0_ref[...]
        n_scr[...] = n0_ref[...]
        m_scr[...] = m0_ref[...]

    row = lax.broadcasted_iota(jnp.int32, (T, T), 0)
    col = lax.broadcasted_iota(jnp.int32, (T, T), 1)
    causal = col <= row
    tril = causal.astype(F32)
    triu = (row <= col).astype(F32)

    def chunk(c, carry):
        r0 = pl.multiple_of(c * T, T)
        rows = pl.ds(r0, T)
        g = gc_ref[rows, :] + bgc_ref[...]
        lfc = _log_sigmoid(g)
        gt = gt_ref[c] + bgr_ref[...]
        igr_all = gt[0:H, :]
        lfr = _log_sigmoid(gt[H:2 * H, :])
        if n_valid < T:
            rid = lax.broadcasted_iota(jnp.int32, (T, LANES), 0)
            g = jnp.where(rid < n_valid, g, NEG)
            lfc = jnp.where(rid < n_valid, lfc, 0.0)
            cid = lax.broadcasted_iota(jnp.int32, (H, T), 1)
            igr_all = jnp.where(cid < n_valid, igr_all, NEG)
            lfr = jnp.where(cid < n_valid, lfr, 0.0)
        bcol = _dot_hi(tril, lfc)
        brow = _dot_hi(lfr, triu)
        qk = qk_ref[rows, :]
        vv = v_ref[rows, :]
        og = og_ref[rows, :]
        hs = range(H)
        m_all = m_scr[...]
        bc = [bcol[:, H + h:H + h + 1] for h in hs]
        q = [(qk[:, h * 64:(h + 1) * 64] * 0.125).astype(BF16) for h in hs]
        k = [qk[:, 512 + h * 64:512 + (h + 1) * 64] for h in hs]
        kb = [k[h].astype(BF16) for h in hs]
        v = [vv[:, h * LANES:(h + 1) * LANES] for h in hs]
        c_old = [c_scr[h] for h in hs]
        n_old = [n_scr[h:h + 1, :] for h in hs]
        qkt = [lax.dot_general(q[h], kb[h], NT, preferred_element_type=F32) for h in hs]
        qc = [lax.dot_general(q[h], c_old[h].astype(BF16), NT, preferred_element_type=F32) for h in hs]
        log_d = [jnp.where(causal, bc[h] - brow[h:h + 1, :] + igr_all[h:h + 1, :], -jnp.inf) for h in hs]
        inter = [bc[h] + m_all[:, h:h + 1] for h in hs]
        m_t = [jnp.maximum(inter[h], jnp.max(log_d[h], axis=-1, keepdims=True)) for h in hs]
        w_inter = [jnp.exp(inter[h] - m_t[h]) for h in hs]
        sc = [qkt[h] * jnp.exp(log_d[h] - m_t[h]) for h in hs]
        sv = [jnp.dot(sc[h].astype(BF16), v[h].astype(BF16), preferred_element_type=F32) for h in hs]
        m_last = [m_t[h][T - 1:T, :] for h in hs]
        w_end = [jnp.exp(bc[h][T - 1:T, :] - bc[h] + g[:, h:h + 1] - m_last[h]) for h in hs]
        vk = [lax.dot_general((w_end[h] * v[h]).astype(BF16), kb[h], TN, preferred_element_type=F32) for h in hs]
        for h in hs:
            num = sv[h] + w_inter[h] * qc[h]
            den = (jnp.sum(sc[h], axis=-1, keepdims=True)
                   + w_inter[h] * jnp.sum(q[h].astype(F32) * n_old[h], axis=-1, keepdims=True))
            hh = num / jnp.maximum(jnp.abs(den), jnp.exp(-m_t[h]))
            sl = slice(h * LANES, (h + 1) * LANES)
            mix_ref[rows, sl] = _sigmoid(og[:, sl]) * (_rms(hh) * ng_ref[:, sl])
            decay = w_inter[h][T - 1:T, :]
            c_scr[h] = decay * c_old[h] + vk[h]
            n_scr[h:h + 1, :] = decay * n_old[h] + jnp.sum(w_end[h] * k[h], axis=0, keepdims=True)
            m_scr[:, h:h + 1] = m_last[h]
        return carry

    lax.fori_loop(0, nc, chunk, 0)

    @pl.when(s % sps == sps - 1)
    def _():
        co_ref[...] = c_scr[...]
        no_ref[...] = n_scr[...]
        mo_ref[...] = m_scr[...]


def mlstm(u, gates, b_gate, norm_g, c0, n0, m0, nseq, seqlen, T, rows_per_step, n_valid):
    m = u.shape[0]
    nc = rows_per_step // T
    sps = seqlen // rows_per_step
    gt = jnp.transpose(gates[:, :2 * H].reshape(m // T, T, 2 * H), (0, 2, 1))
    bgc = jnp.zeros((1, LANES), F32).at[0, :2 * H].set(b_gate.reshape(-1))
    bgr = b_gate.reshape(2 * H, 1)
    r = rows_per_step
    kern = functools.partial(_mlstm_kernel, T=T, nc=nc, n_valid=n_valid, sps=sps)
    return pl.pallas_call(
        kern,
        grid=(m // r,),
        in_specs=[pl.BlockSpec((r, D), lambda s: (s, 0)),
                  pl.BlockSpec((r, D), lambda s: (s, 1)),
                  pl.BlockSpec((r, D), lambda s: (s, 2)),
                  pl.BlockSpec((r, LANES), lambda s: (s, 0)),
                  pl.BlockSpec((nc, 2 * H, T), lambda s: (s, 0, 0)),
                  pl.BlockSpec((1, LANES), lambda s: (0, 0)),
                  pl.BlockSpec((2 * H, 1), lambda s: (0, 0)),
                  pl.BlockSpec((1, D), lambda s: (0, 0)),
                  pl.BlockSpec((None, H, 128, 64), lambda s: (s // sps, 0, 0, 0)),
                  pl.BlockSpec((None, H, 64), lambda s: (s // sps, 0, 0)),
                  pl.BlockSpec((None, 1, H), lambda s: (s // sps, 0, 0))],
        out_specs=[pl.BlockSpec((r, D), lambda s: (s, 0)),
                   pl.BlockSpec((None, H, 128, 64), lambda s: (s // sps, 0, 0, 0)),
                   pl.BlockSpec((None, H, 64), lambda s: (s // sps, 0, 0)),
                   pl.BlockSpec((None, 1, H), lambda s: (s // sps, 0, 0))],
        out_shape=[jax.ShapeDtypeStruct((m, D), F32),
                   jax.ShapeDtypeStruct((nseq, H, 128, 64), F32),
                   jax.ShapeDtypeStruct((nseq, H, 64), F32),
                   jax.ShapeDtypeStruct((nseq, 1, H), F32)],
        scratch_shapes=[pltpu.VMEM((H, 128, 64), F32), pltpu.VMEM((H, 64), F32), pltpu.VMEM((1, H), F32)],
        compiler_params=_cp(1),
        name="mlstm",
    )(u, u, u, gates, gt, bgc, bgr, norm_g.reshape(1, D), c0, n0, m0.reshape(nseq, 1, H))


def _gdn_kernel(x_ref, z_ref, gc_ref, gt_ref, cw_ref, pc_ref, pr_ref, ng_ref, s0_ref, cv0_ref,
                mix_ref, so_ref, xbuf, cv_scr, s_scr, *, T, nc, n_valid, sps):
    s = pl.program_id(0)
    R = T * nc

    @pl.when(s % sps == 0)
    def _():
        s_scr[...] = s0_ref[...]
        xbuf[5:8, :] = cv0_ref[...]

    xbuf[8:8 + R, :] = x_ref[...]
    conv = (cw_ref[0:1, :] * xbuf[5:5 + R, :] + cw_ref[1:2, :] * xbuf[6:6 + R, :]
            + cw_ref[2:3, :] * xbuf[7:7 + R, :] + cw_ref[3:4, :] * xbuf[8:8 + R, :])
    cv_scr[...] = _silu(conv)
    xbuf[0:8, :] = xbuf[R:R + 8, :]

    row = lax.broadcasted_iota(jnp.int32, (T, T), 0)
    col = lax.broadcasted_iota(jnp.int32, (T, T), 1)
    causal = col <= row
    strict = col < row
    tril = causal.astype(F32)
    triu = (row <= col).astype(F32)

    def chunk(c, carry):
        r0 = pl.multiple_of(c * T, T)
        rows = pl.ds(r0, T)
        gpre = gc_ref[rows, :]
        beta_all = _sigmoid(gpre)
        g_all = -jnp.exp(pc_ref[0:1, :]) * _softplus(gpre + pc_ref[1:2, :])
        gt = gt_ref[c]
        gr_all = -jnp.exp(pr_ref[:, 0:1]) * _softplus(gt + pr_ref[:, 1:2])
        gr = gr_all[H:2 * H, :]
        if n_valid < T:
            rid = lax.broadcasted_iota(jnp.int32, (T, LANES), 0)
            beta_all = jnp.where(rid < n_valid, beta_all, 0.0)
            g_all = jnp.where(rid < n_valid, g_all, 0.0)
            cid = lax.broadcasted_iota(jnp.int32, (H, T), 1)
            gr = jnp.where(cid < n_valid, gr, 0.0)
        gamc_all = _dot_hi(tril, g_all)
        gamr_all = _dot_hi(gr, triu)
        hs = range(H)
        qf = [cv_scr[rows, h * LANES:(h + 1) * LANES] for h in hs]
        kf = [cv_scr[rows, D + h * LANES:D + (h + 1) * LANES] for h in hs]
        vf = [cv_scr[rows, 2 * D + h * LANES:2 * D + (h + 1) * LANES] for h in hs]
        qb = [(qf[h] * lax.rsqrt(jnp.sum(qf[h] * qf[h], axis=-1, keepdims=True) + EPS) * LANES ** -0.5).astype(BF16)
              for h in hs]
        kn = [kf[h] * lax.rsqrt(jnp.sum(kf[h] * kf[h], axis=-1, keepdims=True) + EPS) for h in hs]
        kb = [kn[h].astype(BF16) for h in hs]
        beta = [beta_all[:, h:h + 1] for h in hs]
        gamc = [gamc_all[:, H + h:H + h + 1] for h in hs]
        egam = [jnp.exp(gamc[h]) for h in hs]
        dec = [jnp.exp(jnp.where(causal, gamc[h] - gamr_all[h:h + 1, :], NEG)) for h in hs]
        s_old = [s_scr[h] for h in hs]
        sb = [s_old[h].astype(BF16) for h in hs]
        qkk = [lax.dot_general(jnp.concatenate([qb[h], kb[h]], axis=0), kb[h], NT, preferred_element_type=F32)
               for h in hs]
        qs = [lax.dot_general(qb[h], sb[h], NT, preferred_element_type=F32) for h in hs]
        nmat = [jnp.where(strict, beta[h] * qkk[h][T:2 * T] * dec[h], 0.0) for h in hs]
        y = [-nmat[h] for h in hs]
        nsp = [_split_bf16(nmat[h]) for h in hs]
        span = 2
        while span < T:
            npow = [_dot_split(nsp[h][0], nsp[h][1], nsp[h][0], nsp[h][1]) for h in hs]
            nsp = [_split_bf16(npow[h]) for h in hs]
            ysp = [_split_bf16(y[h]) for h in hs]
            y = [y[h] + npow[h] + _dot_split(ysp[h][0], ysp[h][1], nsp[h][0], nsp[h][1]) for h in hs]
            span *= 2
        rhs = [jnp.concatenate([beta[h] * vf[h], (beta[h] * egam[h]) * kn[h]], axis=1) for h in hs]
        sol = [rhs[h] + _dot(y[h], rhs[h]) for h in hs]
        delta = [sol[h][:, 0:LANES] - _dot_nt(sol[h][:, LANES:2 * LANES], sb[h]) for h in hs]
        o = [egam[h] * qs[h] + _dot(qkk[h][0:T] * dec[h], delta[h]) for h in hs]
        g_end = [gamc[h][T - 1:T, :] for h in hs]
        upd = [_dot_tn(jnp.exp(g_end[h] - gamc[h]) * delta[h], kb[h]) for h in hs]
        for h in hs:
            sl = slice(h * LANES, (h + 1) * LANES)
            s_scr[h] = jnp.exp(g_end[h]) * s_old[h] + upd[h]
            mix_ref[rows, sl] = _rms(o[h]) * ng_ref[:, sl] * _silu(z_ref[rows, sl])
        return carry

    lax.fori_loop(0, nc, chunk, 0)

    @pl.when(s % sps == sps - 1)
    def _():
        so_ref[...] = s_scr[...]


def gdn(u, gates, conv_w, a_log, dt_bias, norm_g, s0, conv0, nseq, seqlen, T, rows_per_step, n_valid):
    m = u.shape[0]
    nc = rows_per_step // T
    sps = seqlen // rows_per_step
    r = rows_per_step
    gt = jnp.transpose(gates[:, :2 * H].reshape(m // T, T, 2 * H), (0, 2, 1))
    pc = jnp.zeros((2, LANES), F32).at[0, H:2 * H].set(a_log).at[1, H:2 * H].set(dt_bias)
    pr = jnp.zeros((2 * H, 2), F32).at[H:, 0].set(a_log).at[H:, 1].set(dt_bias)
    kern = functools.partial(_gdn_kernel, T=T, nc=nc, n_valid=n_valid, sps=sps)
    return pl.pallas_call(
        kern,
        grid=(m // r,),
        in_specs=[pl.BlockSpec((r, 3 * D), lambda s: (s, 0)),
                  pl.BlockSpec((r, D), lambda s: (s, 3)),
                  pl.BlockSpec((r, LANES), lambda s: (s, 0)),
                  pl.BlockSpec((nc, 2 * H, T), lambda s: (s, 0, 0)),
                  pl.BlockSpec((4, 3 * D), lambda s: (0, 0)),
                  pl.BlockSpec((2, LANES), lambda s: (0, 0)),
                  pl.BlockSpec((2 * H, 2), lambda s: (0, 0)),
                  pl.BlockSpec((1, D), lambda s: (0, 0)),
                  pl.BlockSpec((None, H, 128, 128), lambda s: (s // sps, 0, 0, 0)),
                  pl.BlockSpec((None, 3, 3 * D), lambda s: (s // sps, 0, 0))],
        out_specs=[pl.BlockSpec((r, D), lambda s: (s, 0)),
                   pl.BlockSpec((None, H, 128, 128), lambda s: (s // sps, 0, 0, 0))],
        out_shape=[jax.ShapeDtypeStruct((m, D), F32),
                   jax.ShapeDtypeStruct((nseq, H, 128, 128), F32)],
        scratch_shapes=[pltpu.VMEM((r + 8, 3 * D), F32), pltpu.VMEM((r, 3 * D), F32),
                        pltpu.VMEM((H, 128, 128), F32)],
        compiler_params=_cp(1),
        name="gdn",
    )(u, u, gates, gt, conv_w, pc, pr, norm_g.reshape(1, D), s0, conv0)


def _bucket(rel):
    n = jnp.maximum(rel, 0)
    nf = jnp.maximum(n, 1).astype(F32)
    large = 16 + (jnp.log(nf / 16) / math.log(128 / 16) * 16).astype(jnp.int32)
    return jnp.where(n < 16, n, jnp.minimum(large, 31))


def _diff_lambda(lw_ref, lam_init):
    lw = lw_ref[...]
    a = jnp.sum(lw[0:1, :] * lw[1:2, :], axis=-1, keepdims=True)
    b = jnp.sum(lw[2:3, :] * lw[3:4, :], axis=-1, keepdims=True)
    return jnp.exp(a) - jnp.exp(b) + lam_init


def _dattn_p_kernel(q_ref, k_ref, v_ref, bn_ref, lw_ref, ng_ref, o_ref, m_scr, acc_scr, *, TB, lam_init):
    i = pl.program_id(2)
    W = 2 * TB
    lane = lax.broadcasted_iota(jnp.int32, (TB, LANES), 1)
    q = q_ref[...] * 0.125
    qq = jnp.concatenate([jnp.where(lane < 64, q, 0.0), jnp.where(lane >= 64, q, 0.0)], axis=0).astype(BF16)
    m_scr[...] = jnp.full_like(m_scr, NEG)
    acc_scr[...] = jnp.zeros_like(acc_scr)

    def step(k0, width, bias):
        rows = pl.ds(pl.multiple_of(k0, TB), width)
        kb = k_ref[rows, :].astype(BF16)
        vext = jnp.concatenate([v_ref[rows, :].astype(BF16), jnp.ones((width, LANES), BF16)], axis=1)
        sc = lax.dot_general(qq, kb, NT, preferred_element_type=F32)
        if bias is not None:
            sc = sc + bias
        ng = width // LANES
        mloc = sc[:, 0:LANES]
        for g in range(1, ng):
            mloc = jnp.maximum(mloc, sc[:, g * LANES:(g + 1) * LANES])
        m_old = m_scr[...]
        m_new = jnp.maximum(m_old, jnp.max(mloc, axis=-1, keepdims=True))
        alpha = jnp.exp(m_old - m_new)
        p = jnp.exp(sc - jnp.concatenate([m_new] * ng, axis=1)).astype(BF16)
        acc_scr[...] = (jnp.concatenate([alpha, alpha], axis=1) * acc_scr[...]
                        + jnp.dot(p, vext, preferred_element_type=F32))
        m_scr[...] = m_new

    n_far = jnp.maximum(i - 1, 0)
    n_big = n_far // 2

    def body(c, carry):
        step(c * W, W, None)
        return carry

    lax.fori_loop(0, n_big, body, 0)

    @pl.when(n_far % 2 == 1)
    def _():
        step(n_big * W, TB, None)

    step(n_far * TB, W, bn_ref[...])
    lam = _diff_lambda(lw_ref, lam_init)
    acc = acc_scr[...]
    o = (acc[0:TB, 0:LANES] / acc[0:TB, LANES:2 * LANES]
         - lam * (acc[TB:2 * TB, 0:LANES] / acc[TB:2 * TB, LANES:2 * LANES]))
    o_ref[...] = _rms(o) * ng_ref[...] * (1.0 - lam_init)


def _bias_table(rel_bias, rel):
    onehot = (_bucket(rel)[..., None] == jnp.arange(32)).astype(F32)
    return jnp.einsum('...k,kh->h...', onehot, rel_bias - rel_bias[31:32], precision=HI)


def diff_attn_prompt(u, lam_w, norm_g, rel_bias, lam_init, TB=256):
    nq = SEQ // TB
    ii = jnp.arange(TB)
    rel_d = ii[:, None] - ii[None, :]
    bd = jnp.where((rel_d >= 0)[None], _bias_table(rel_bias, rel_d), NEG)
    bs = _bias_table(rel_bias, rel_d + TB)
    masked = jnp.full((H, TB, TB), NEG, F32)
    bn = jnp.stack([jnp.concatenate([bd, masked], axis=2), jnp.concatenate([bs, bd], axis=2)], axis=1)
    bn = jnp.concatenate([bn, bn], axis=2)
    kern = functools.partial(_dattn_p_kernel, TB=TB, lam_init=lam_init)
    return pl.pallas_call(
        kern,
        grid=(BATCH, H, nq),
        in_specs=[pl.BlockSpec((TB, LANES), lambda b, h, i: (b * nq + i, h)),
                  pl.BlockSpec((SEQ, LANES), lambda b, h, i: (b, H + h)),
                  pl.BlockSpec((SEQ, LANES), lambda b, h, i: (b, 2 * H + h)),
                  pl.BlockSpec((None, None, 2 * TB, 2 * TB), lambda b, h, i: (h, jnp.minimum(i, 1), 0, 0)),
                  pl.BlockSpec((4, 64), lambda b, h, i: (0, 0)),
                  pl.BlockSpec((1, LANES), lambda b, h, i: (0, h))],
        out_specs=pl.BlockSpec((TB, LANES), lambda b, h, i: (b * nq + i, h)),
        out_shape=jax.ShapeDtypeStruct((BATCH * SEQ, D), F32),
        scratch_shapes=[pltpu.VMEM((2 * TB, LANES), F32), pltpu.VMEM((2 * TB, 2 * LANES), F32)],
        compiler_params=_cp(3),
        name="diff_attn_prompt",
    )(u, u, u, bn, lam_w, norm_g.reshape(1, D))


def _dattn_s_kernel(pt_ref, qm_ref, qb_ref, *refs, lam_init):
    del pt_ref
    kp_refs = refs[0:PAGES_PER_STEP]
    vp_refs = refs[PAGES_PER_STEP:2 * PAGES_PER_STEP]
    kn_ref, vn_ref, bl_ref, bn_ref, lw_ref, ng_ref, o_ref, m_scr, l_scr, acc_scr = refs[2 * PAGES_PER_STEP:]
    p = pl.program_id(1)
    last = pl.num_programs(1) - 1
    rows_h = 2 * DEC_PAD

    @pl.when(p == 0)
    def _():
        m_scr[...] = jnp.full_like(m_scr, NEG)
        l_scr[...] = jnp.zeros_like(l_scr)
        acc_scr[...] = jnp.zeros_like(acc_scr)

    def head_rows(page_refs, h):
        return jnp.concatenate([r[pl.ds(h, PAGE, stride=H), :].astype(BF16) for r in page_refs], axis=0)

    qm = qm_ref[...].astype(BF16)
    sc = jnp.concatenate(
        [lax.dot_general(qm[h * rows_h:(h + 1) * rows_h], head_rows(kp_refs, h), NT, preferred_element_type=F32)
         for h in range(H)], axis=0)
    sc = sc + jnp.where(p == last, bl_ref[...], 0.0)
    m_old = m_scr[...]
    m_new = jnp.maximum(m_old, jnp.max(sc, axis=-1, keepdims=True))
    alpha = jnp.exp(m_old - m_new)
    pr = jnp.exp(sc - jnp.concatenate([m_new] * PAGES_PER_STEP, axis=1))
    l_scr[...] = alpha * l_scr[...] + jnp.sum(pr, axis=-1, keepdims=True)
    prb = pr.astype(BF16)
    pv = jnp.concatenate(
        [jnp.dot(prb[h * rows_h:(h + 1) * rows_h], head_rows(vp_refs, h), preferred_element_type=F32)
         for h in range(H)], axis=0)
    acc_scr[...] = alpha * acc_scr[...] + pv
    m_scr[...] = m_new

    @pl.when(p == last)
    def _():
        scn = lax.dot_general(qb_ref[...], kn_ref[...], NT, preferred_element_type=F32) + bn_ref[...]
        m_o = m_scr[...]
        m_n = jnp.maximum(m_o, jnp.max(scn, axis=-1, keepdims=True))
        al = jnp.exp(m_o - m_n)
        pn = jnp.exp(scn - m_n[:, 0:DEC_PAD])
        l_n = al * l_scr[...] + jnp.sum(pn, axis=-1, keepdims=True)
        pvn = jnp.dot(pn, vn_ref[...], preferred_element_type=F32)
        lam = _diff_lambda(lw_ref, lam_init)
        for h in range(H):
            sl = slice(h * LANES, (h + 1) * LANES)
            r0 = slice(h * rows_h, h * rows_h + DEC_PAD)
            r1 = slice(h * rows_h + DEC_PAD, (h + 1) * rows_h)
            a0 = (al[r0] * acc_scr[r0, :] + pvn[r0, sl]) / l_n[r0]
            a1 = (al[r1] * acc_scr[r1, :] + pvn[r1, sl]) / l_n[r1]
            o_ref[:, sl] = _rms(a0 - lam * a1) * ng_ref[:, sl] * (1.0 - lam_init)


def diff_attn_sample(u, cache_k, cache_v, page_table, lam_w, norm_g, rel_bias, lam_init):
    nb = DEC_BATCH
    nr = 2 * H * DEC_PAD
    q = u[:, :D].reshape(nb, DEC_PAD, D) * 0.125
    grp_row = jnp.arange(2 * H)[:, None, None]
    grp_col = (jnp.arange(D) // 64)[None, None, :]
    qbig = jnp.where(grp_row == grp_col, q[:, None, :, :], 0.0).reshape(nb, nr, D)
    q4 = jnp.transpose(q.reshape(nb, DEC_PAD, H, 1, LANES), (0, 2, 3, 1, 4))
    comp = (jnp.arange(LANES) // 64)[None, None, None, None, :]
    qm = jnp.where(comp == jnp.arange(2)[None, None, :, None, None], q4, 0.0).reshape(nb, nr, LANES)
    qi = jnp.arange(DEC_PAD)
    rel_last = PAGE + qi[:, None] - jnp.arange(PAGE)[None, :]
    b_last = jnp.broadcast_to(_bias_table(rel_bias, rel_last)[:, None], (H, 2, DEC_PAD, PAGE)).reshape(nr, PAGE)
    b_last = jnp.pad(b_last, ((0, 0), ((PAGES_PER_STEP - 1) * PAGE, 0)))
    rel_new = qi[:, None] - qi[None, :]
    vis = (rel_new >= 0) & (qi[None, :] < DEC_SEQ)
    b_new = jnp.where(vis[None], _bias_table(rel_bias, rel_new), NEG)
    b_new = jnp.broadcast_to(b_new[:, None], (H, 2, DEC_PAD, DEC_PAD)).reshape(nr, DEC_PAD)
    kern = functools.partial(_dattn_s_kernel, lam_init=lam_init)
    def page_spec(slot):
        return pl.BlockSpec((None, PAGE * H, LANES), lambda b, p, pt: (pt[b, p * PAGES_PER_STEP + slot], 0, 0))

    pages = [page_spec(slot) for slot in range(PAGES_PER_STEP)]
    grid_spec = pltpu.PrefetchScalarGridSpec(
        num_scalar_prefetch=1,
        grid=(nb, N_PAGES // PAGES_PER_STEP),
        in_specs=[pl.BlockSpec((None, nr, LANES), lambda b, p, pt: (b, 0, 0)),
                  pl.BlockSpec((None, nr, D), lambda b, p, pt: (b, 0, 0))] + pages + pages + [
                  pl.BlockSpec((DEC_PAD, D), lambda b, p, pt: (b, 1)),
                  pl.BlockSpec((DEC_PAD, D), lambda b, p, pt: (b, 2)),
                  pl.BlockSpec((nr, PAGES_PER_STEP * PAGE), lambda b, p, pt: (0, 0)),
                  pl.BlockSpec((nr, DEC_PAD), lambda b, p, pt: (0, 0)),
                  pl.BlockSpec((4, 64), lambda b, p, pt: (0, 0)),
                  pl.BlockSpec((1, D), lambda b, p, pt: (0, 0))],
        out_specs=pl.BlockSpec((DEC_PAD, D), lambda b, p, pt: (b, 0)),
        scratch_shapes=[pltpu.VMEM((nr, LANES), F32), pltpu.VMEM((nr, LANES), F32), pltpu.VMEM((nr, LANES), F32)],
    )
    return pl.pallas_call(
        kern,
        grid_spec=grid_spec,
        out_shape=jax.ShapeDtypeStruct((nb * DEC_PAD, D), F32),
        compiler_params=_cp(2),
        name="diff_attn_sample",
    )(page_table, qm, qbig, *([cache_k] * PAGES_PER_STEP), *([cache_v] * PAGES_PER_STEP), u, u, b_last, b_new,
      lam_w, norm_g.reshape(1, D))


def _router_kernel(x_ref, w_ref, o_ref):
    logits = _dot(x_ref[...], w_ref[...])
    lane = lax.broadcasted_iota(jnp.int32, logits.shape, 1)
    logits = jnp.where(lane < N_EXPERTS, logits, -jnp.inf)
    m1 = jnp.max(logits, axis=-1, keepdims=True)
    i1 = jnp.min(jnp.where(logits == m1, lane, LANES), axis=-1, keepdims=True)
    rest = jnp.where(lane == i1, -jnp.inf, logits)
    m2 = jnp.max(rest, axis=-1, keepdims=True)
    i2 = jnp.min(jnp.where(rest == m2, lane, LANES), axis=-1, keepdims=True)
    e2 = jnp.exp(m2 - m1)
    g1 = 1.0 / (1.0 + e2)
    g2 = e2 / (1.0 + e2)
    o_ref[...] = jnp.where(lane == 0, i1.astype(F32),
                           jnp.where(lane == 1, i2.astype(F32),
                                     jnp.where(lane == 2, g1, jnp.where(lane == 3, g2, 0.0))))


def router(x, w_router, tm):
    m = x.shape[0]
    w = jnp.pad(w_router, ((0, 0), (0, LANES - N_EXPERTS)))
    return pl.pallas_call(
        _router_kernel,
        grid=(m // tm,),
        in_specs=[pl.BlockSpec((tm, D), lambda i: (i, 0)),
                  pl.BlockSpec((D, LANES), lambda i: (0, 0))],
        out_specs=pl.BlockSpec((tm, LANES), lambda i: (i, 0)),
        out_shape=jax.ShapeDtypeStruct((m, LANES), F32),
        compiler_params=_cp(1),
        name="router",
    )(x, w)


def _gather_kernel(src_ref, x_hbm, o_ref, sem, *, G):
    base = pl.program_id(0) * G

    def issue(r, carry):
        pltpu.make_async_copy(x_hbm.at[pl.ds(src_ref[base + r], 1), :], o_ref.at[pl.ds(r, 1), :], sem).start()
        return carry

    lax.fori_loop(0, G, issue, 0, unroll=8)

    def drain(r, carry):
        pltpu.make_async_copy(x_hbm.at[pl.ds(0, 1), :], o_ref.at[pl.ds(r, 1), :], sem).wait()
        return carry

    lax.fori_loop(0, G, drain, 0, unroll=8)


def gather_rows(x, src, G):
    n = src.shape[0]
    grid_spec = pltpu.PrefetchScalarGridSpec(
        num_scalar_prefetch=1,
        grid=(n // G,),
        in_specs=[pl.BlockSpec(memory_space=pl.ANY)],
        out_specs=pl.BlockSpec((G, D), lambda i, s: (i, 0)),
        scratch_shapes=[pltpu.SemaphoreType.DMA(())],
    )
    return pl.pallas_call(
        functools.partial(_gather_kernel, G=G),
        grid_spec=grid_spec,
        out_shape=jax.ShapeDtypeStruct((n, D), x.dtype),
        compiler_params=_cp(1),
        name="gather_rows",
    )(src, x)


def _expert_kernel(be_ref, nu_ref, x_ref, wgu_ref, wd_ref, o_ref, *, tf):
    i = pl.program_id(0)
    valid = i < nu_ref[0]

    @pl.when(valid)
    def _():
        xb = x_ref[...].astype(BF16)
        for f0 in range(0, D_FF_EXPERT, tf):
            gate = jnp.dot(xb, wgu_ref[:, f0:f0 + tf], preferred_element_type=F32)
            up = jnp.dot(xb, wgu_ref[:, D_FF_EXPERT + f0:D_FF_EXPERT + f0 + tf], preferred_element_type=F32)
            part = jnp.dot((_silu(gate) * up).astype(BF16), wd_ref[f0:f0 + tf, :], preferred_element_type=F32)
            if f0 == 0:
                o_ref[...] = part
            else:
                o_ref[...] += part

    @pl.when(jnp.logical_not(valid))
    def _():
        o_ref[...] = jnp.zeros_like(o_ref)


def expert_ffn(xs, w_gu, w_down, blk_e, n_used, tm, tf=512):
    n = xs.shape[0]
    grid_spec = pltpu.PrefetchScalarGridSpec(
        num_scalar_prefetch=2,
        grid=(n // tm,),
        in_specs=[pl.BlockSpec((tm, D), lambda i, be, nu: (i, 0)),
                  pl.BlockSpec((None, D, 2 * D_FF_EXPERT), lambda i, be, nu: (be[i], 0, 0),
                               pipeline_mode=pl.Buffered(1)),
                  pl.BlockSpec((None, D_FF_EXPERT, D), lambda i, be, nu: (be[i], 0, 0),
                               pipeline_mode=pl.Buffered(1))],
        out_specs=pl.BlockSpec((tm, D), lambda i, be, nu: (i, 0)),
    )
    return pl.pallas_call(
        functools.partial(_expert_kernel, tf=tf),
        grid_spec=grid_spec,
        out_shape=jax.ShapeDtypeStruct((n, D), F32),
        compiler_params=_cp(1),
        name="expert_ffn",
    )(blk_e, n_used, xs, w_gu, w_down)


def _combine_kernel(d0_ref, d1_ref, yb_hbm, x_ref, rt_ref, g_ref, b_ref, o_ref, buf0, buf1, sem, *, G):
    base = pl.program_id(0) * G

    def issue(r, carry):
        pltpu.make_async_copy(yb_hbm.at[pl.ds(d0_ref[base + r], 1), :], buf0.at[pl.ds(r, 1), :], sem).start()
        pltpu.make_async_copy(yb_hbm.at[pl.ds(d1_ref[base + r], 1), :], buf1.at[pl.ds(r, 1), :], sem).start()
        return carry

    lax.fori_loop(0, G, issue, 0, unroll=4)

    def drain(r, carry):
        pltpu.make_async_copy(yb_hbm.at[pl.ds(0, 1), :], buf0.at[pl.ds(r, 1), :], sem).wait()
        pltpu.make_async_copy(yb_hbm.at[pl.ds(0, 1), :], buf1.at[pl.ds(r, 1), :], sem).wait()
        return carry

    lax.fori_loop(0, G, drain, 0, unroll=4)
    rt = rt_ref[...]
    y = rt[:, 2:3] * buf0[...] + rt[:, 3:4] * buf1[...]
    o_ref[...] = _layer_norm(ALPHA * x_ref[...] + y, g_ref[...], b_ref[...])


def combine_ln(yb, x, route, d0, d1, g, b, G):
    m = x.shape[0]
    grid_spec = pltpu.PrefetchScalarGridSpec(
        num_scalar_prefetch=2,
        grid=(m // G,),
        in_specs=[pl.BlockSpec(memory_space=pl.ANY),
                  pl.BlockSpec((G, D), lambda i, a, c: (i, 0)),
                  pl.BlockSpec((G, LANES), lambda i, a, c: (i, 0)),
                  pl.BlockSpec((1, D), lambda i, a, c: (0, 0)),
                  pl.BlockSpec((1, D), lambda i, a, c: (0, 0))],
        out_specs=pl.BlockSpec((G, D), lambda i, a, c: (i, 0)),
        scratch_shapes=[pltpu.VMEM((G, D), F32), pltpu.VMEM((G, D), F32), pltpu.SemaphoreType.DMA(())],
    )
    return pl.pallas_call(
        functools.partial(_combine_kernel, G=G),
        grid_spec=grid_spec,
        out_shape=jax.ShapeDtypeStruct((m, D), F32),
        compiler_params=_cp(1),
        name="combine_ln",
    )(d0, d1, yb, x, route, g.reshape(1, D), b.reshape(1, D))


def moe_ln(x, w_router, w_gu, w_down, g, b, tm=512):
    m = x.shape[0]
    route = router(x, w_router, 640)
    flat_e = route[:, :2].astype(jnp.int32).reshape(-1)
    onehot = (flat_e[:, None] == jnp.arange(N_EXPERTS)[None, :]).astype(jnp.int32)
    csum = jnp.cumsum(onehot, axis=0)
    counts = csum[-1]
    padded = (counts + tm - 1) // tm * tm
    p_end = jnp.cumsum(padded)
    dest = jnp.sum(onehot * ((p_end - padded)[None, :] + csum - 1), axis=1)
    n_blocks = -(-(2 * m + N_EXPERTS * (tm - 1)) // tm)
    n_rows = n_blocks * tm
    n_used = (p_end[-1] // tm).astype(jnp.int32).reshape(1)
    blk = jnp.minimum(jnp.arange(n_blocks), n_used[0] - 1) * tm
    blk_e = jnp.minimum(jnp.sum((blk[:, None] >= p_end[None, :]).astype(jnp.int32), axis=1), N_EXPERTS - 1)
    x_z = jnp.concatenate([x, jnp.zeros((8, D), x.dtype)], axis=0)
    src = jnp.full((n_rows,), m, jnp.int32).at[dest].set(jnp.arange(2 * m, dtype=jnp.int32) // 2)
    xs = gather_rows(x_z, src, tm)
    yb = expert_ffn(xs, w_gu.astype(BF16), w_down.astype(BF16), blk_e, n_used, tm)
    d = dest.reshape(m, 2).astype(jnp.int32)
    return combine_ln(yb, x, route, d[:, 0], d[:, 1], g, b, 640)


def kernel(x_prompt, x_sample, mem_prompt, page_table, cache_diff_k, cache_diff_v, cache_mem_k, cache_mem_v,
           state_mlstm_C, state_mlstm_n, state_mlstm_m, state_gdn_S, state_gdn_conv, rel_bias,
           w_in_a, b_gate_a, norm_a, w_in_b, lambda_b, norm_b, w_in_c, conv_c, a_log_c, dt_bias_c, norm_c,
           w_mem_kv, w_out, ln_g, ln_b, w_ffn_gu, w_ffn_down, w_router, w_exp_gu, w_exp_down):
    mp = BATCH * SEQ
    ms = DEC_BATCH * DEC_PAD
    x_p = x_prompt.reshape(mp, D)
    x_s = jnp.pad(x_sample, ((0, 0), (0, DEC_PAD - DEC_SEQ), (0, 0))).reshape(ms, D)
    mem2 = mem_prompt.reshape(BATCH * MEM_TOKENS, D)
    n_phys = cache_diff_k.shape[1]

    def sample_rows(t):
        return t.reshape(DEC_BATCH, DEC_PAD, -1)[:, :DEC_SEQ]

    def pad_gates(w):
        return jnp.pad(w, ((0, 0), (0, LANES - w.shape[1])))

    mlstm_p, mlstm_s, diff_p, diff_s, gdn_p, gdn_s, mem_k_new, mem_v_new = [], [], [], [], [], [], [], []
    for i in range(DEPTH):
        kind, j = i % 3, i // 3
        mem_kv = matmul(mem2, w_mem_kv[i], MEM_TOKENS, 512).reshape(BATCH, MEM_TOKENS, 2 * MEM_Q)
        mem_k_new.append(mem_kv[:, :, :MEM_Q].reshape(BATCH, MEM_TOKENS, MEM_HEADS, 128))
        mem_v_new.append(mem_kv[:, :, MEM_Q:].reshape(BATCH, MEM_TOKENS, MEM_HEADS, 128))
        if kind == 0:
            w = w_in_a[j]
            w_main = jnp.concatenate([w[:, :3 * D], w[:, 3 * D + 2 * H:]], axis=1)
            w_g = pad_gates(w[:, 3 * D:3 * D + 2 * H])
            u_p, u_s = matmul(x_p, w_main, 512, 512), matmul(x_s, w_main, ms, 512)
            g_p, g_s = matmul(x_p, w_g, 512, LANES), matmul(x_s, w_g, ms, LANES)
            zc = jnp.zeros((BATCH, H, 128, 64), F32)
            zn = jnp.zeros((BATCH, H, 64), F32)
            zm = jnp.zeros((BATCH, H), F32)
            mix_p, c_p, n_p, m_p = mlstm(u_p, g_p, b_gate_a[j], norm_a[j], zc, zn, zm,
                                         BATCH, SEQ, CHUNK, 256, CHUNK)
            mix_s, c_s, n_s, m_s = mlstm(u_s, g_s, b_gate_a[j], norm_a[j], state_mlstm_C[j], state_mlstm_n[j],
                                         state_mlstm_m[j], DEC_BATCH, DEC_PAD, DEC_PAD, DEC_PAD, DEC_SEQ)
            mlstm_p.append((c_p, n_p, m_p.reshape(BATCH, H)))
            mlstm_s.append((c_s, n_s, m_s.reshape(DEC_BATCH, H)))
            mq_col = 6
        elif kind == 1:
            lam_init = 0.8 - 0.6 * math.exp(-0.3 * i)
            w_main = w_in_b[j]
            u_p, u_s = matmul(x_p, w_main, 512, 512), matmul(x_s, w_main, ms, 512)
            mix_p = diff_attn_prompt(u_p, lambda_b[j], norm_b[j], rel_bias, lam_init)
            mix_s = diff_attn_sample(u_s, cache_diff_k[j].reshape(n_phys, PAGE * H, LANES),
                                     cache_diff_v[j].reshape(n_phys, PAGE * H, LANES), page_table,
                                     lambda_b[j], norm_b[j], rel_bias, lam_init)
            k_rows, v_rows = kv_head_rows(u_p, 256)
            diff_p.append((k_rows.reshape(BATCH, SEQ, H, 128), v_rows.reshape(BATCH, SEQ, H, 128)))
            diff_s.append((sample_rows(u_s[:, D:2 * D]).reshape(DEC_BATCH, DEC_SEQ, H, 128),
                           sample_rows(u_s[:, 2 * D:3 * D]).reshape(DEC_BATCH, DEC_SEQ, H, 128)))
            mq_col = 6
        else:
            w = w_in_c[j]
            w_main = jnp.concatenate([w[:, :4 * D], w[:, 4 * D + 2 * H:]], axis=1)
            w_g = pad_gates(w[:, 4 * D:4 * D + 2 * H])
            u_p, u_s = matmul(x_p, w_main, 512, 512), matmul(x_s, w_main, ms, 512)
            g_p, g_s = matmul(x_p, w_g, 512, LANES), matmul(x_s, w_g, ms, LANES)
            zs = jnp.zeros((BATCH, H, 128, 128), F32)
            zv = jnp.zeros((BATCH, 3, 3 * D), F32)
            mix_p, s_p = gdn(u_p, g_p, conv_c[j], a_log_c[j], dt_bias_c[j], norm_c[j], zs, zv,
                             BATCH, SEQ, CHUNK, 256, CHUNK)
            mix_s, s_s = gdn(u_s, g_s, conv_c[j], a_log_c[j], dt_bias_c[j], norm_c[j], state_gdn_S[j],
                             state_gdn_conv[j], DEC_BATCH, DEC_PAD, DEC_PAD, DEC_PAD, DEC_SEQ)
            conv_p = u_p[:, :3 * D].reshape(BATCH, SEQ, 3 * D)[:, SEQ - 3:]
            conv_s = sample_rows(u_s[:, :3 * D])[:, DEC_SEQ - 3:]
            gdn_p.append((s_p, conv_p))
            gdn_s.append((s_s, conv_s))
            mq_col = 8
        n_main = u_p.shape[1]
        mo_p = mem_attention(u_p.reshape(BATCH, SEQ, n_main), mq_col, mem_kv, 0, mem_kv, 1, 512).reshape(mp, MEM_Q)
        cmk = cache_mem_k[i].reshape(DEC_BATCH, MEM_TOKENS, MEM_Q)
        cmv = cache_mem_v[i].reshape(DEC_BATCH, MEM_TOKENS, MEM_Q)
        mo_s = mem_attention(u_s.reshape(DEC_BATCH, DEC_PAD, n_main), mq_col, cmk, 0, cmv, 0, DEC_PAD).reshape(ms, MEM_Q)
        x_p = outproj_ln(mix_p, mo_p, x_p, w_out[i], ln_g[i, 0], ln_b[i, 0], 512)
        x_s = outproj_ln(mix_s, mo_s, x_s, w_out[i], ln_g[i, 0], ln_b[i, 0], ms)
        k_f = i // 2
        if i % 2 == 0:
            x_p = ffn_ln(x_p, w_ffn_gu[k_f], w_ffn_down[k_f], ln_g[i, 1], ln_b[i, 1], 1024)
            x_s = ffn_ln(x_s, w_ffn_gu[k_f], w_ffn_down[k_f], ln_g[i, 1], ln_b[i, 1], ms)
        else:
            x_all = moe_ln(jnp.concatenate([x_p, x_s], axis=0), w_router[k_f], w_exp_gu[k_f], w_exp_down[k_f],
                           ln_g[i, 1], ln_b[i, 1])
            x_p, x_s = x_all[:mp], x_all[mp:]

    def stack(lst, k):
        return jnp.stack([t[k] for t in lst])

    return (x_p.reshape(BATCH, SEQ, D), sample_rows(x_s),
            stack(mlstm_p, 0), stack(mlstm_p, 1), stack(mlstm_p, 2),
            stack(mlstm_s, 0), stack(mlstm_s, 1), stack(mlstm_s, 2),
            stack(diff_p, 0), stack(diff_p, 1), stack(diff_s, 0), stack(diff_s, 1),
            stack(gdn_p, 0), stack(gdn_p, 1), stack(gdn_s, 0), stack(gdn_s, 1),
            jnp.stack(mem_k_new), jnp.stack(mem_v_new))
```

```python
import functools
import math

import jax
import jax.numpy as jnp
from jax import lax
from jax.experimental import pallas as pl
from jax.experimental.pallas import tpu as pltpu

F32 = jnp.float32
BF16 = jnp.bfloat16
HI = lax.Precision.HIGHEST
NT = (((1,), (1,)), ((), ()))
TN = (((0,), (0,)), ((), ()))

D = 1024
H = 8
DEPTH = 4
SEQ = 4096
BATCH = 4
DEC_BATCH = 32
DEC_SEQ = 4
DEC_PAD = 8
PAST_LEN = 8192
PAGE = 128
N_PAGES = PAST_LEN // PAGE
PAGES_PER_STEP = 4
MEM_TOKENS = 256
MEM_Q = 512
MEM_HEADS = 4
CHUNK = 64
N_EXPERTS = 8
D_FF = 2752
D_FF_PAD = 2816
D_FF_EXPERT = 3584
EPS = 1e-6
ALPHA = (2 * DEPTH) ** 0.25
NEG = -1e30
LANES = 128
VMEM_LIMIT = 56 * 1024 * 1024


def _cp(n_axes, vmem=VMEM_LIMIT):
    return pltpu.CompilerParams(dimension_semantics=("arbitrary",) * n_axes, vmem_limit_bytes=vmem)


def _dot(a, b):
    return jnp.dot(a.astype(BF16), b.astype(BF16), preferred_element_type=F32)


def _dot_nt(a, b):
    return lax.dot_general(a.astype(BF16), b.astype(BF16), NT, preferred_element_type=F32)


def _dot_tn(a, b):
    return lax.dot_general(a.astype(BF16), b.astype(BF16), TN, preferred_element_type=F32)


def _dot_hi(a, b):
    return jnp.dot(a, b, preferred_element_type=F32, precision=HI)


def _split_bf16(a):
    hi = a.astype(BF16)
    return hi, (a - hi.astype(F32)).astype(BF16)


def _dot_split(a_hi, a_lo, b_hi, b_lo):
    return (jnp.dot(a_hi, b_hi, preferred_element_type=F32)
            + (jnp.dot(a_lo, b_hi, preferred_element_type=F32) + jnp.dot(a_hi, b_lo, preferred_element_type=F32)))


def _sigmoid(x):
    return 1.0 / (1.0 + jnp.exp(-x))


def _silu(x):
    return x * _sigmoid(x)


def _softplus(x):
    return jnp.maximum(x, 0.0) + jnp.log1p(jnp.exp(-jnp.abs(x)))


def _log_sigmoid(x):
    return -_softplus(-x)


def _layer_norm(v, g, b):
    mu = jnp.mean(v, axis=-1, keepdims=True)
    d = v - mu
    var = jnp.mean(d * d, axis=-1, keepdims=True)
    return d * lax.rsqrt(var + EPS) * g + b


def _rms(h):
    return h * lax.rsqrt(jnp.mean(h * h, axis=-1, keepdims=True) + EPS)


def _mm_kernel(x_ref, w_ref, o_ref, *, tn):
    xb = x_ref[...].astype(BF16)
    for n0 in range(0, o_ref.shape[1], tn):
        o_ref[:, n0:n0 + tn] = jnp.dot(xb, w_ref[:, n0:n0 + tn], preferred_element_type=F32)


def matmul(x, w, tm, tn):
    m, k = x.shape
    n = w.shape[1]
    return pl.pallas_call(
        functools.partial(_mm_kernel, tn=min(tn, n)),
        grid=(m // tm,),
        in_specs=[pl.BlockSpec((tm, k), lambda i: (i, 0)),
                  pl.BlockSpec((k, n), lambda i: (0, 0), pipeline_mode=pl.Buffered(1))],
        out_specs=pl.BlockSpec((tm, n), lambda i: (i, 0)),
        out_shape=jax.ShapeDtypeStruct((m, n), F32),
        compiler_params=_cp(1),
        name="matmul",
    )(x, w.astype(BF16))


def _headrows_kernel(k_ref, v_ref, ko_ref, vo_ref):
    tm = k_ref.shape[0]
    for h in range(H):
        sl = slice(h * LANES, (h + 1) * LANES)
        ko_ref[pl.ds(h, tm, stride=H), :] = k_ref[:, sl]
        vo_ref[pl.ds(h, tm, stride=H), :] = v_ref[:, sl]


def kv_head_rows(u, tm):
    m = u.shape[0]
    return pl.pallas_call(
        _headrows_kernel,
        grid=(m // tm,),
        in_specs=[pl.BlockSpec((tm, D), lambda i: (i, 1)),
                  pl.BlockSpec((tm, D), lambda i: (i, 2))],
        out_specs=[pl.BlockSpec((tm * H, LANES), lambda i: (i, 0)),
                   pl.BlockSpec((tm * H, LANES), lambda i: (i, 0))],
        out_shape=[jax.ShapeDtypeStruct((m * H, LANES), F32), jax.ShapeDtypeStruct((m * H, LANES), F32)],
        compiler_params=_cp(1),
        name="kv_head_rows",
    )(u, u)


def _outproj_kernel(mix_ref, mo_ref, x_ref, w1_ref, w2_ref, g_ref, b_ref, o_ref):
    y = _dot(mix_ref[...], w1_ref[...]) + _dot(mo_ref[...], w2_ref[...])
    o_ref[...] = _layer_norm(ALPHA * x_ref[...] + y, g_ref[...], b_ref[...])


def outproj_ln(mix, mo, x, w_out, g, b, tm):
    m = x.shape[0]
    w1 = w_out[:D].astype(BF16)
    w2 = w_out[D:].astype(BF16)
    return pl.pallas_call(
        _outproj_kernel,
        grid=(m // tm,),
        in_specs=[pl.BlockSpec((tm, D), lambda i: (i, 0)),
                  pl.BlockSpec((tm, MEM_Q), lambda i: (i, 0)),
                  pl.BlockSpec((tm, D), lambda i: (i, 0)),
                  pl.BlockSpec((D, D), lambda i: (0, 0)),
                  pl.BlockSpec((MEM_Q, D), lambda i: (0, 0)),
                  pl.BlockSpec((1, D), lambda i: (0, 0)),
                  pl.BlockSpec((1, D), lambda i: (0, 0))],
        out_specs=pl.BlockSpec((tm, D), lambda i: (i, 0)),
        out_shape=jax.ShapeDtypeStruct((m, D), F32),
        compiler_params=_cp(1),
        name="outproj_ln",
    )(mix, mo, x, w1, w2, g.reshape(1, D), b.reshape(1, D))


def _ffn_kernel(x_ref, wg_ref, wu_ref, wd_ref, g_ref, b_ref, o_ref, xb_scr, acc_scr):
    j = pl.program_id(1)

    @pl.when(j == 0)
    def _():
        xb_scr[...] = x_ref[...].astype(BF16)
        acc_scr[...] = jnp.zeros_like(acc_scr)

    xb = xb_scr[...]
    gate = jnp.dot(xb, wg_ref[...], preferred_element_type=F32)
    up = jnp.dot(xb, wu_ref[...], preferred_element_type=F32)
    acc_scr[...] += _dot(_silu(gate) * up, wd_ref[...])

    @pl.when(j == pl.num_programs(1) - 1)
    def _():
        o_ref[...] = _layer_norm(ALPHA * x_ref[...] + acc_scr[...], g_ref[...], b_ref[...])


def ffn_ln(x, w_gu, w_down, g, b, tm, tf=256):
    m = x.shape[0]
    pad = D_FF_PAD - D_FF
    wg = jnp.pad(w_gu[:, :D_FF], ((0, 0), (0, pad))).astype(BF16)
    wu = jnp.pad(w_gu[:, D_FF:], ((0, 0), (0, pad))).astype(BF16)
    wd = jnp.pad(w_down, ((0, pad), (0, 0))).astype(BF16)
    return pl.pallas_call(
        _ffn_kernel,
        grid=(m // tm, D_FF_PAD // tf),
        in_specs=[pl.BlockSpec((tm, D), lambda i, j: (i, 0)),
                  pl.BlockSpec((D, tf), lambda i, j: (0, j)),
                  pl.BlockSpec((D, tf), lambda i, j: (0, j)),
                  pl.BlockSpec((tf, D), lambda i, j: (j, 0)),
                  pl.BlockSpec((1, D), lambda i, j: (0, 0)),
                  pl.BlockSpec((1, D), lambda i, j: (0, 0))],
        out_specs=pl.BlockSpec((tm, D), lambda i, j: (i, 0)),
        out_shape=jax.ShapeDtypeStruct((m, D), F32),
        scratch_shapes=[pltpu.VMEM((tm, D), BF16), pltpu.VMEM((tm, D), F32)],
        compiler_params=_cp(2),
        name="ffn_ln",
    )(x, wg, wu, wd, g.reshape(1, D), b.reshape(1, D))


def _memattn_kernel(q_ref, k_ref, v_ref, o_ref, *, head_rows):
    q = q_ref[...]
    for h in range(MEM_HEADS):
        sl = slice(h * LANES, (h + 1) * LANES)
        if head_rows:
            k = k_ref[pl.ds(h, MEM_TOKENS, stride=MEM_HEADS), :]
            v = v_ref[pl.ds(h, MEM_TOKENS, stride=MEM_HEADS), :]
        else:
            k = k_ref[:, sl]
            v = v_ref[:, sl]
        s = _dot_nt(q[:, sl], k) * LANES ** -0.5
        s = s - jnp.max(s, axis=-1, keepdims=True)
        p = jnp.exp(s)
        p = p / jnp.sum(p, axis=-1, keepdims=True)
        o_ref[:, sl] = _dot(p, v)


def mem_attention(u, q_col, mem_k, k_col, mem_v, v_col, tq, seq0=0, head_rows=False):
    b, l, _ = u.shape
    if head_rows:
        kv_block = (None, MEM_TOKENS * MEM_HEADS, LANES)
    else:
        kv_block = (None, MEM_TOKENS, MEM_Q)
    return pl.pallas_call(
        functools.partial(_memattn_kernel, head_rows=head_rows),
        grid=(b, l // tq),
        in_specs=[pl.BlockSpec((None, tq, MEM_Q), lambda bi, i: (bi, i, q_col)),
                  pl.BlockSpec(kv_block, lambda bi, i: (bi + seq0, 0, k_col)),
                  pl.BlockSpec(kv_block, lambda bi, i: (bi + seq0, 0, v_col))],
        out_specs=pl.BlockSpec((None, tq, MEM_Q), lambda bi, i: (bi, i, 0)),
        out_shape=jax.ShapeDtypeStruct((b, l, MEM_Q), F32),
        compiler_params=_cp(2),
        name="mem_attention",
    )(u, mem_k, mem_v)


def _mlstm_kernel(qk_ref, v_ref, og_ref, gc_ref, gt_ref, bgc_ref, bgr_ref, ng_ref, c0_ref, n0_ref, m0_ref,
                  mix_ref, co_ref, no_ref, mo_ref, c_scr, n_scr, m_scr, *, T, nc, n_valid, sps):
    s = pl.program_id(0)

    @pl.when(s % sps == 0)
    def _():
        c_scr[...] = c0_ref[...]
        n_scr[...] = n0_ref[...]
        m_scr[...] = m0_ref[...]

    row = lax.broadcasted_iota(jnp.int32, (T, T), 0)
    col = lax.broadcasted_iota(jnp.int32, (T, T), 1)
    causal = col <= row
    tril = causal.astype(F32)
    triu = (row <= col).astype(F32)

    cs = range(nc)
    hs = range(H)
    ch = [(c, h) for c in cs for h in hs]
    rows = [slice(c * T, (c + 1) * T) for c in cs]
    g, bcol, brow, igr_all = [], [], [], []
    for c in cs:
        g_c = gc_ref[rows[c], :] + bgc_ref[...]
        lfc = _log_sigmoid(g_c)
        gt = gt_ref[c] + bgr_ref[...]
        igr = gt[0:H, :]
        lfr = _log_sigmoid(gt[H:2 * H, :])
        if n_valid < T:
            rid = lax.broadcasted_iota(jnp.int32, (T, LANES), 0)
            g_c = jnp.where(rid < n_valid, g_c, NEG)
            lfc = jnp.where(rid < n_valid, lfc, 0.0)
            cid = lax.broadcasted_iota(jnp.int32, (H, T), 1)
            igr = jnp.where(cid < n_valid, igr, NEG)
            lfr = jnp.where(cid < n_valid, lfr, 0.0)
        g.append(g_c)
        igr_all.append(igr)
        bcol.append(_dot_hi(tril, lfc))
        brow.append(_dot_hi(lfr, triu))
    bc = {(c, h): bcol[c][:, H + h:H + h + 1] for c, h in ch}
    log_d = {(c, h): jnp.where(causal, bc[c, h] - brow[c][h:h + 1, :] + igr_all[c][h:h + 1, :], -jnp.inf)
             for c, h in ch}
    rmax = {(c, h): jnp.max(log_d[c, h], axis=-1, keepdims=True) for c, h in ch}
    m_all = m_scr[...]
    m_prev = [m_all[:, h:h + 1] for h in hs]
    inter, m_t = {}, {}
    for c, h in ch:
        inter[c, h] = bc[c, h] + m_prev[h]
        m_t[c, h] = jnp.maximum(inter[c, h], rmax[c, h])
        m_prev[h] = m_t[c, h][T - 1:T, :]
    qk = [qk_ref[rows[c], :] for c in cs]
    vv = [v_ref[rows[c], :] for c in cs]
    q = {(c, h): (qk[c][:, h * 64:(h + 1) * 64] * 0.125).astype(BF16) for c, h in ch}
    k = {(c, h): qk[c][:, 512 + h * 64:512 + (h + 1) * 64] for c, h in ch}
    kb = {(c, h): k[c, h].astype(BF16) for c, h in ch}
    v = {(c, h): vv[c][:, h * LANES:(h + 1) * LANES] for c, h in ch}
    qkt = {(c, h): lax.dot_general(q[c, h], kb[c, h], NT, preferred_element_type=F32) for c, h in ch}
    w_inter = {(c, h): jnp.exp(inter[c, h] - m_t[c, h]) for c, h in ch}
    sc = {(c, h): qkt[c, h] * jnp.exp(log_d[c, h] - m_t[c, h]) for c, h in ch}
    sv = {(c, h): jnp.dot(sc[c, h].astype(BF16), v[c, h].astype(BF16), preferred_element_type=F32) for c, h in ch}
    den0 = {(c, h): jnp.sum(sc[c, h], axis=-1, keepdims=True) for c, h in ch}
    w_end = {(c, h): jnp.exp(bc[c, h][T - 1:T, :] - bc[c, h] + g[c][:, h:h + 1] - m_t[c, h][T - 1:T, :])
             for c, h in ch}
    vk = {(c, h): lax.dot_general((w_end[c, h] * v[c, h]).astype(BF16), kb[c, h], TN, preferred_element_type=F32)
          for c, h in ch}
    kw = {(c, h): jnp.sum(w_end[c, h] * k[c, h], axis=0, keepdims=True) for c, h in ch}
    c_in = {(0, h): c_scr[h] for h in hs}
    n_in = {(0, h): n_scr[h:h + 1, :] for h in hs}
    for c, h in ch:
        decay = w_inter[c, h][T - 1:T, :]
        c_in[c + 1, h] = decay * c_in[c, h] + vk[c, h]
        n_in[c + 1, h] = decay * n_in[c, h] + kw[c, h]
    qc = {(c, h): lax.dot_general(q[c, h], c_in[c, h].astype(BF16), NT, preferred_element_type=F32) for c, h in ch}
    for c, h in ch:
        num = sv[c, h] + w_inter[c, h] * qc[c, h]
        den = den0[c, h] + w_inter[c, h] * jnp.sum(q[c, h].astype(F32) * n_in[c, h], axis=-1, keepdims=True)
        hh = num / jnp.maximum(jnp.abs(den), jnp.exp(-m_t[c, h]))
        sl = slice(h * LANES, (h + 1) * LANES)
        mix_ref[rows[c], sl] = _sigmoid(og_ref[rows[c], sl]) * (_rms(hh) * ng_ref[:, sl])
    for h in hs:
        c_scr[h] = c_in[nc, h]
        n_scr[h:h + 1, :] = n_in[nc, h]
        m_scr[:, h:h + 1] = m_prev[h]

    @pl.when(s % sps == sps - 1)
    def _():
        co_ref[...] = c_scr[...]
        no_ref[...] = n_scr[...]
        mo_ref[...] = m_scr[...]


def mlstm(u, gates, b_gate, norm_g, c0, n0, m0, nseq, seqlen, T, rows_per_step, n_valid):
    m = u.shape[0]
    nc = rows_per_step // T
    sps = seqlen // rows_per_step
    gt = jnp.transpose(gates[:, :2 * H].reshape(m // T, T, 2 * H), (0, 2, 1))
    bgc = jnp.zeros((1, LANES), F32).at[0, :2 * H].set(b_gate.reshape(-1))
    bgr = b_gate.reshape(2 * H, 1)
    r = rows_per_step
    kern = functools.partial(_mlstm_kernel, T=T, nc=nc, n_valid=n_valid, sps=sps)
    return pl.pallas_call(
        kern,
        grid=(m // r,),
        in_specs=[pl.BlockSpec((r, D), lambda s: (s, 0)),
                  pl.BlockSpec((r, D), lambda s: (s, 1)),
                  pl.BlockSpec((r, D), lambda s: (s, 2)),
                  pl.BlockSpec((r, LANES), lambda s: (s, 0)),
                  pl.BlockSpec((nc, 2 * H, T), lambda s: (s, 0, 0)),
                  pl.BlockSpec((1, LANES), lambda s: (0, 0)),
                  pl.BlockSpec((2 * H, 1), lambda s: (0, 0)),
                  pl.BlockSpec((1, D), lambda s: (0, 0)),
                  pl.BlockSpec((None, H, 128, 64), lambda s: (s // sps, 0, 0, 0)),
                  pl.BlockSpec((None, H, 64), lambda s: (s // sps, 0, 0)),
                  pl.BlockSpec((None, 1, H), lambda s: (s // sps, 0, 0))],
        out_specs=[pl.BlockSpec((r, D), lambda s: (s, 0)),
                   pl.BlockSpec((None, H, 128, 64), lambda s: (s // sps, 0, 0, 0)),
                   pl.BlockSpec((None, H, 64), lambda s: (s // sps, 0, 0)),
                   pl.BlockSpec((None, 1, H), lambda s: (s // sps, 0, 0))],
        out_shape=[jax.ShapeDtypeStruct((m, D), F32),
                   jax.ShapeDtypeStruct((nseq, H, 128, 64), F32),
                   jax.ShapeDtypeStruct((nseq, H, 64), F32),
                   jax.ShapeDtypeStruct((nseq, 1, H), F32)],
        scratch_shapes=[pltpu.VMEM((H, 128, 64), F32), pltpu.VMEM((H, 64), F32), pltpu.VMEM((1, H), F32)],
        compiler_params=_cp(1),
        name="mlstm",
    )(u, u, u, gates, gt, bgc, bgr, norm_g.reshape(1, D), c0, n0, m0.reshape(nseq, 1, H))


def _gdn_kernel(x_ref, z_ref, gc_ref, gt_ref, cw_ref, pc_ref, pr_ref, ng_ref, s0_ref, cv0_ref,
                mix_ref, so_ref, xbuf, cv_scr, s_scr, *, T, nc, n_valid, sps):
    s = pl.program_id(0)
    R = T * nc

    @pl.when(s % sps == 0)
    def _():
        s_scr[...] = s0_ref[...]
        xbuf[5:8, :] = cv0_ref[...]

    xbuf[8:8 + R, :] = x_ref[...]
    conv = (cw_ref[0:1, :] * xbuf[5:5 + R, :] + cw_ref[1:2, :] * xbuf[6:6 + R, :]
            + cw_ref[2:3, :] * xbuf[7:7 + R, :] + cw_ref[3:4, :] * xbuf[8:8 + R, :])
    cv_scr[...] = _silu(conv)
    xbuf[0:8, :] = xbuf[R:R + 8, :]

    row = lax.broadcasted_iota(jnp.int32, (T, T), 0)
    col = lax.broadcasted_iota(jnp.int32, (T, T), 1)
    causal = col <= row
    strict = col < row
    tril = causal.astype(F32)
    triu = (row <= col).astype(F32)

    def chunk(c, carry):
        r0 = pl.multiple_of(c * T, T)
        rows = pl.ds(r0, T)
        gpre = gc_ref[rows, :]
        beta_all = _sigmoid(gpre)
        g_all = -jnp.exp(pc_ref[0:1, :]) * _softplus(gpre + pc_ref[1:2, :])
        gt = gt_ref[c]
        gr_all = -jnp.exp(pr_ref[:, 0:1]) * _softplus(gt + pr_ref[:, 1:2])
        gr = gr_all[H:2 * H, :]
        if n_valid < T:
            rid = lax.broadcasted_iota(jnp.int32, (T, LANES), 0)
            beta_all = jnp.where(rid < n_valid, beta_all, 0.0)
            g_all = jnp.where(rid < n_valid, g_all, 0.0)
            cid = lax.broadcasted_iota(jnp.int32, (H, T), 1)
            gr = jnp.where(cid < n_valid, gr, 0.0)
        gamc_all = _dot_hi(tril, g_all)
        gamr_all = _dot_hi(gr, triu)
        hs = range(H)
        qf = [cv_scr[rows, h * LANES:(h + 1) * LANES] for h in hs]
        kf = [cv_scr[rows, D + h * LANES:D + (h + 1) * LANES] for h in hs]
        vf = [cv_scr[rows, 2 * D + h * LANES:2 * D + (h + 1) * LANES] for h in hs]
        qb = [(qf[h] * lax.rsqrt(jnp.sum(qf[h] * qf[h], axis=-1, keepdims=True) + EPS) * LANES ** -0.5).astype(BF16)
              for h in hs]
        kn = [kf[h] * lax.rsqrt(jnp.sum(kf[h] * kf[h], axis=-1, keepdims=True) + EPS) for h in hs]
        kb = [kn[h].astype(BF16) for h in hs]
        beta = [beta_all[:, h:h + 1] for h in hs]
        gamc = [gamc_all[:, H + h:H + h + 1] for h in hs]
        egam = [jnp.exp(gamc[h]) for h in hs]
        dec = [jnp.exp(jnp.where(causal, gamc[h] - gamr_all[h:h + 1, :], NEG)) for h in hs]
        s_old = [s_scr[h] for h in hs]
        sb = [s_old[h].astype(BF16) for h in hs]
        qkk = [lax.dot_general(jnp.concatenate([qb[h], kb[h]], axis=0), kb[h], NT, preferred_element_type=F32)
               for h in hs]
        qs = [lax.dot_general(qb[h], sb[h], NT, preferred_element_type=F32) for h in hs]
        nmat = [jnp.where(strict, beta[h] * qkk[h][T:2 * T] * dec[h], 0.0) for h in hs]
        y = [-nmat[h] for h in hs]
        nsp = [_split_bf16(nmat[h]) for h in hs]
        span = 2
        while span < T:
            npow = [_dot_split(nsp[h][0], nsp[h][1], nsp[h][0], nsp[h][1]) for h in hs]
            nsp = [_split_bf16(npow[h]) for h in hs]
            ysp = [_split_bf16(y[h]) for h in hs]
            y = [y[h] + npow[h] + _dot_split(ysp[h][0], ysp[h][1], nsp[h][0], nsp[h][1]) for h in hs]
            span *= 2
        rhs = [jnp.concatenate([beta[h] * vf[h], (beta[h] * egam[h]) * kn[h]], axis=1) for h in hs]
        sol = [rhs[h] + _dot(y[h], rhs[h]) for h in hs]
        delta = [sol[h][:, 0:LANES] - _dot_nt(sol[h][:, LANES:2 * LANES], sb[h]) for h in hs]
        o = [egam[h] * qs[h] + _dot(qkk[h][0:T] * dec[h], delta[h]) for h in hs]
        g_end = [gamc[h][T - 1:T, :] for h in hs]
        upd = [_dot_tn(jnp.exp(g_end[h] - gamc[h]) * delta[h], kb[h]) for h in hs]
        for h in hs:
            sl = slice(h * LANES, (h + 1) * LANES)
            s_scr[h] = jnp.exp(g_end[h]) * s_old[h] + upd[h]
            mix_ref[rows, sl] = _rms(o[h]) * ng_ref[:, sl] * _silu(z_ref[rows, sl])
        return carry

    lax.fori_loop(0, nc, chunk, 0, unroll=min(nc, 2))

    @pl.when(s % sps == sps - 1)
    def _():
        so_ref[...] = s_scr[...]


def gdn(u, gates, conv_w, a_log, dt_bias, norm_g, s0, conv0, nseq, seqlen, T, rows_per_step, n_valid):
    m = u.shape[0]
    nc = rows_per_step // T
    sps = seqlen // rows_per_step
    r = rows_per_step
    gt = jnp.transpose(gates[:, :2 * H].reshape(m // T, T, 2 * H), (0, 2, 1))
    pc = jnp.zeros((2, LANES), F32).at[0, H:2 * H].set(a_log).at[1, H:2 * H].set(dt_bias)
    pr = jnp.zeros((2 * H, 2), F32).at[H:, 0].set(a_log).at[H:, 1].set(dt_bias)
    kern = functools.partial(_gdn_kernel, T=T, nc=nc, n_valid=n_valid, sps=sps)
    return pl.pallas_call(
        kern,
        grid=(m // r,),
        in_specs=[pl.BlockSpec((r, 3 * D), lambda s: (s, 0)),
                  pl.BlockSpec((r, D), lambda s: (s, 3)),
                  pl.BlockSpec((r, LANES), lambda s: (s, 0)),
                  pl.BlockSpec((nc, 2 * H, T), lambda s: (s, 0, 0)),
                  pl.BlockSpec((4, 3 * D), lambda s: (0, 0)),
                  pl.BlockSpec((2, LANES), lambda s: (0, 0)),
                  pl.BlockSpec((2 * H, 2), lambda s: (0, 0)),
                  pl.BlockSpec((1, D), lambda s: (0, 0)),
                  pl.BlockSpec((None, H, 128, 128), lambda s: (s // sps, 0, 0, 0)),
                  pl.BlockSpec((None, 3, 3 * D), lambda s: (s // sps, 0, 0))],
        out_specs=[pl.BlockSpec((r, D), lambda s: (s, 0)),
                   pl.BlockSpec((None, H, 128, 128), lambda s: (s // sps, 0, 0, 0))],
        out_shape=[jax.ShapeDtypeStruct((m, D), F32),
                   jax.ShapeDtypeStruct((nseq, H, 128, 128), F32)],
        scratch_shapes=[pltpu.VMEM((r + 8, 3 * D), F32), pltpu.VMEM((r, 3 * D), F32),
                        pltpu.VMEM((H, 128, 128), F32)],
        compiler_params=_cp(1),
        name="gdn",
    )(u, u, gates, gt, conv_w, pc, pr, norm_g.reshape(1, D), s0, conv0)


def _bucket(rel):
    n = jnp.maximum(rel, 0)
    nf = jnp.maximum(n, 1).astype(F32)
    large = 16 + (jnp.log(nf / 16) / math.log(128 / 16) * 16).astype(jnp.int32)
    return jnp.where(n < 16, n, jnp.minimum(large, 31))


def _diff_lambda(lw_ref, lam_init):
    lw = lw_ref[...]
    a = jnp.sum(lw[0:1, :] * lw[1:2, :], axis=-1, keepdims=True)
    b = jnp.sum(lw[2:3, :] * lw[3:4, :], axis=-1, keepdims=True)
    return jnp.exp(a) - jnp.exp(b) + lam_init


def _dattn_p_kernel(q_ref, k_ref, v_ref, bn_ref, lw_ref, ng_ref, o_ref, m_scr, acc_scr, *, TB, lam_init):
    i = pl.program_id(2)
    W = 2 * TB
    lane = lax.broadcasted_iota(jnp.int32, (TB, LANES), 1)
    q = q_ref[...] * 0.125
    qq = jnp.concatenate([jnp.where(lane < 64, q, 0.0), jnp.where(lane >= 64, q, 0.0)], axis=0).astype(BF16)
    m_scr[...] = jnp.full_like(m_scr, NEG)
    acc_scr[...] = jnp.zeros_like(acc_scr)

    def step(k0, width, bias):
        rows = pl.ds(pl.multiple_of(k0, TB), width)
        kb = k_ref[rows, :].astype(BF16)
        vext = jnp.concatenate([v_ref[rows, :].astype(BF16), jnp.ones((width, LANES), BF16)], axis=1)
        sc = lax.dot_general(qq, kb, NT, preferred_element_type=F32)
        if bias is not None:
            sc = sc + bias
        ng = width // LANES
        mloc = sc[:, 0:LANES]
        for g in range(1, ng):
            mloc = jnp.maximum(mloc, sc[:, g * LANES:(g + 1) * LANES])
        m_old = m_scr[...]
        m_new = jnp.maximum(m_old, jnp.max(mloc, axis=-1, keepdims=True))
        alpha = jnp.exp(m_old - m_new)
        p = jnp.exp(sc - jnp.concatenate([m_new] * ng, axis=1)).astype(BF16)
        acc_scr[...] = (jnp.concatenate([alpha, alpha], axis=1) * acc_scr[...]
                        + jnp.dot(p, vext, preferred_element_type=F32))
        m_scr[...] = m_new

    n_far = jnp.maximum(i - 1, 0)
    n_big = n_far // 4
    rem = n_far % 4

    def body(c, carry):
        step(c * 2 * W, W, None)
        step(c * 2 * W + W, W, None)
        return carry

    lax.fori_loop(0, n_big, body, 0)

    @pl.when(rem >= 2)
    def _():
        step(n_big * 2 * W, W, None)

    @pl.when(rem % 2 == 1)
    def _():
        step((n_far - 1) * TB, TB, None)

    step(n_far * TB, W, bn_ref[...])
    lam = _diff_lambda(lw_ref, lam_init)
    acc = acc_scr[...]
    o = (acc[0:TB, 0:LANES] / acc[0:TB, LANES:2 * LANES]
         - lam * (acc[TB:2 * TB, 0:LANES] / acc[TB:2 * TB, LANES:2 * LANES]))
    o_ref[...] = _rms(o) * ng_ref[...] * (1.0 - lam_init)


def _bias_table(rel_bias, rel):
    onehot = (_bucket(rel)[..., None] == jnp.arange(32)).astype(F32)
    return jnp.einsum('...k,kh->h...', onehot, rel_bias - rel_bias[31:32], precision=HI)


def diff_attn_prompt(u, lam_w, norm_g, rel_bias, lam_init, TB=256):
    nq = SEQ // TB
    ii = jnp.arange(TB)
    rel_d = ii[:, None] - ii[None, :]
    bd = jnp.where((rel_d >= 0)[None], _bias_table(rel_bias, rel_d), NEG)
    bs = _bias_table(rel_bias, rel_d + TB)
    masked = jnp.full((H, TB, TB), NEG, F32)
    bn = jnp.stack([jnp.concatenate([bd, masked], axis=2), jnp.concatenate([bs, bd], axis=2)], axis=1)
    bn = jnp.concatenate([bn, bn], axis=2)
    kern = functools.partial(_dattn_p_kernel, TB=TB, lam_init=lam_init)
    return pl.pallas_call(
        kern,
        grid=(BATCH, H, nq),
        in_specs=[pl.BlockSpec((TB, LANES), lambda b, h, i: (b * nq + i, h)),
                  pl.BlockSpec((SEQ, LANES), lambda b, h, i: (b, H + h)),
                  pl.BlockSpec((SEQ, LANES), lambda b, h, i: (b, 2 * H + h)),
                  pl.BlockSpec((None, None, 2 * TB, 2 * TB), lambda b, h, i: (h, jnp.minimum(i, 1), 0, 0)),
                  pl.BlockSpec((4, 64), lambda b, h, i: (0, 0)),
                  pl.BlockSpec((1, LANES), lambda b, h, i: (0, h))],
        out_specs=pl.BlockSpec((TB, LANES), lambda b, h, i: (b * nq + i, h)),
        out_shape=jax.ShapeDtypeStruct((BATCH * SEQ, D), F32),
        scratch_shapes=[pltpu.VMEM((2 * TB, LANES), F32), pltpu.VMEM((2 * TB, 2 * LANES), F32)],
        compiler_params=_cp(3),
        name="diff_attn_prompt",
    )(u, u, u, bn, lam_w, norm_g.reshape(1, D))


def _dattn_s_kernel(pt_ref, qm_ref, qb_ref, *refs, lam_init):
    del pt_ref
    kp_refs = refs[0:PAGES_PER_STEP]
    vp_refs = refs[PAGES_PER_STEP:2 * PAGES_PER_STEP]
    kn_ref, vn_ref, bl_ref, bn_ref, lw_ref, ng_ref, o_ref, m_scr, l_scr, acc_scr = refs[2 * PAGES_PER_STEP:]
    p = pl.program_id(1)
    last = pl.num_programs(1) - 1
    rows_h = 2 * DEC_PAD

    @pl.when(p == 0)
    def _():
        m_scr[...] = jnp.full_like(m_scr, NEG)
        l_scr[...] = jnp.zeros_like(l_scr)
        acc_scr[...] = jnp.zeros_like(acc_scr)

    def head_rows(page_refs, h):
        return jnp.concatenate([r[pl.ds(h, PAGE, stride=H), :].astype(BF16) for r in page_refs], axis=0)

    qm = qm_ref[...].astype(BF16)
    sc = jnp.concatenate(
        [lax.dot_general(qm[h * rows_h:(h + 1) * rows_h], head_rows(kp_refs, h), NT, preferred_element_type=F32)
         for h in range(H)], axis=0)
    sc = sc + jnp.where(p == last, bl_ref[...], 0.0)
    m_old = m_scr[...]
    m_new = jnp.maximum(m_old, jnp.max(sc, axis=-1, keepdims=True))
    alpha = jnp.exp(m_old - m_new)
    pr = jnp.exp(sc - jnp.concatenate([m_new] * PAGES_PER_STEP, axis=1))
    l_scr[...] = alpha * l_scr[...] + jnp.sum(pr, axis=-1, keepdims=True)
    prb = pr.astype(BF16)
    pv = jnp.concatenate(
        [jnp.dot(prb[h * rows_h:(h + 1) * rows_h], head_rows(vp_refs, h), preferred_element_type=F32)
         for h in range(H)], axis=0)
    acc_scr[...] = alpha * acc_scr[...] + pv
    m_scr[...] = m_new

    @pl.when(p == last)
    def _():
        scn = lax.dot_general(qb_ref[...], kn_ref[...], NT, preferred_element_type=F32) + bn_ref[...]
        m_o = m_scr[...]
        m_n = jnp.maximum(m_o, jnp.max(scn, axis=-1, keepdims=True))
        al = jnp.exp(m_o - m_n)
        pn = jnp.exp(scn - m_n[:, 0:DEC_PAD])
        l_n = al * l_scr[...] + jnp.sum(pn, axis=-1, keepdims=True)
        pvn = jnp.dot(pn, vn_ref[...], preferred_element_type=F32)
        lam = _diff_lambda(lw_ref, lam_init)
        for h in range(H):
            sl = slice(h * LANES, (h + 1) * LANES)
            r0 = slice(h * rows_h, h * rows_h + DEC_PAD)
            r1 = slice(h * rows_h + DEC_PAD, (h + 1) * rows_h)
            a0 = (al[r0] * acc_scr[r0, :] + pvn[r0, sl]) / l_n[r0]
            a1 = (al[r1] * acc_scr[r1, :] + pvn[r1, sl]) / l_n[r1]
            o_ref[:, sl] = _rms(a0 - lam * a1) * ng_ref[:, sl] * (1.0 - lam_init)


def diff_attn_sample(u, cache_k, cache_v, page_table, lam_w, norm_g, rel_bias, lam_init):
    nb = DEC_BATCH
    nr = 2 * H * DEC_PAD
    q = u[:, :D].reshape(nb, DEC_PAD, D) * 0.125
    grp_row = jnp.arange(2 * H)[:, None, None]
    grp_col = (jnp.arange(D) // 64)[None, None, :]
    qbig = jnp.where(grp_row == grp_col, q[:, None, :, :], 0.0).reshape(nb, nr, D)
    q4 = jnp.transpose(q.reshape(nb, DEC_PAD, H, 1, LANES), (0, 2, 3, 1, 4))
    comp = (jnp.arange(LANES) // 64)[None, None, None, None, :]
    qm = jnp.where(comp == jnp.arange(2)[None, None, :, None, None], q4, 0.0).reshape(nb, nr, LANES)
    qi = jnp.arange(DEC_PAD)
    rel_last = PAGE + qi[:, None] - jnp.arange(PAGE)[None, :]
    b_last = jnp.broadcast_to(_bias_table(rel_bias, rel_last)[:, None], (H, 2, DEC_PAD, PAGE)).reshape(nr, PAGE)
    b_last = jnp.pad(b_last, ((0, 0), ((PAGES_PER_STEP - 1) * PAGE, 0)))
    rel_new = qi[:, None] - qi[None, :]
    vis = (rel_new >= 0) & (qi[None, :] < DEC_SEQ)
    b_new = jnp.where(vis[None], _bias_table(rel_bias, rel_new), NEG)
    b_new = jnp.broadcast_to(b_new[:, None], (H, 2, DEC_PAD, DEC_PAD)).reshape(nr, DEC_PAD)
    kern = functools.partial(_dattn_s_kernel, lam_init=lam_init)
    def page_spec(slot):
        return pl.BlockSpec((None, PAGE * H, LANES), lambda b, p, pt: (pt[b, p * PAGES_PER_STEP + slot], 0, 0))

    pages = [page_spec(slot) for slot in range(PAGES_PER_STEP)]
    grid_spec = pltpu.PrefetchScalarGridSpec(
        num_scalar_prefetch=1,
        grid=(nb, N_PAGES // PAGES_PER_STEP),
        in_specs=[pl.BlockSpec((None, nr, LANES), lambda b, p, pt: (b, 0, 0)),
                  pl.BlockSpec((None, nr, D), lambda b, p, pt: (b, 0, 0))] + pages + pages + [
                  pl.BlockSpec((DEC_PAD, D), lambda b, p, pt: (b, 1)),
                  pl.BlockSpec((DEC_PAD, D), lambda b, p, pt: (b, 2)),
                  pl.BlockSpec((nr, PAGES_PER_STEP * PAGE), lambda b, p, pt: (0, 0)),
                  pl.BlockSpec((nr, DEC_PAD), lambda b, p, pt: (0, 0)),
                  pl.BlockSpec((4, 64), lambda b, p, pt: (0, 0)),
                  pl.BlockSpec((1, D), lambda b, p, pt: (0, 0))],
        out_specs=pl.BlockSpec((DEC_PAD, D), lambda b, p, pt: (b, 0)),
        scratch_shapes=[pltpu.VMEM((nr, LANES), F32), pltpu.VMEM((nr, LANES), F32), pltpu.VMEM((nr, LANES), F32)],
    )
    return pl.pallas_call(
        kern,
        grid_spec=grid_spec,
        out_shape=jax.ShapeDtypeStruct((nb * DEC_PAD, D), F32),
        compiler_params=_cp(2),
        name="diff_attn_sample",
    )(page_table, qm, qbig, *([cache_k] * PAGES_PER_STEP), *([cache_v] * PAGES_PER_STEP), u, u, b_last, b_new,
      lam_w, norm_g.reshape(1, D))


def _router_kernel(x_ref, w_ref, o_ref):
    logits = _dot(x_ref[...], w_ref[...])
    lane = lax.broadcasted_iota(jnp.int32, logits.shape, 1)
    logits = jnp.where(lane < N_EXPERTS, logits, -jnp.inf)
    m1 = jnp.max(logits, axis=-1, keepdims=True)
    i1 = jnp.min(jnp.where(logits == m1, lane, LANES), axis=-1, keepdims=True)
    rest = jnp.where(lane == i1, -jnp.inf, logits)
    m2 = jnp.max(rest, axis=-1, keepdims=True)
    i2 = jnp.min(jnp.where(rest == m2, lane, LANES), axis=-1, keepdims=True)
    e2 = jnp.exp(m2 - m1)
    g1 = 1.0 / (1.0 + e2)
    g2 = e2 / (1.0 + e2)
    o_ref[...] = jnp.where(lane == 0, i1.astype(F32),
                           jnp.where(lane == 1, i2.astype(F32),
                                     jnp.where(lane == 2, g1, jnp.where(lane == 3, g2, 0.0))))


def router(x, w_router, tm):
    m = x.shape[0]
    w = jnp.pad(w_router, ((0, 0), (0, LANES - N_EXPERTS)))
    return pl.pallas_call(
        _router_kernel,
        grid=(m // tm,),
        in_specs=[pl.BlockSpec((tm, D), lambda i: (i, 0)),
                  pl.BlockSpec((D, LANES), lambda i: (0, 0))],
        out_specs=pl.BlockSpec((tm, LANES), lambda i: (i, 0)),
        out_shape=jax.ShapeDtypeStruct((m, LANES), F32),
        compiler_params=_cp(1),
        name="router",
    )(x, w)


def _gather_kernel(src_ref, x_hbm, o_ref, sem, *, G):
    base = pl.program_id(0) * G

    def issue(r, carry):
        pltpu.make_async_copy(x_hbm.at[pl.ds(src_ref[base + r], 1), :], o_ref.at[pl.ds(r, 1), :], sem).start()
        return carry

    lax.fori_loop(0, G, issue, 0, unroll=8)

    def drain(r, carry):
        pltpu.make_async_copy(x_hbm.at[pl.ds(0, 1), :], o_ref.at[pl.ds(r, 1), :], sem).wait()
        return carry

    lax.fori_loop(0, G, drain, 0, unroll=8)


def gather_rows(x, src, G):
    n = src.shape[0]
    grid_spec = pltpu.PrefetchScalarGridSpec(
        num_scalar_prefetch=1,
        grid=(n // G,),
        in_specs=[pl.BlockSpec(memory_space=pl.ANY)],
        out_specs=pl.BlockSpec((G, D), lambda i, s: (i, 0)),
        scratch_shapes=[pltpu.SemaphoreType.DMA(())],
    )
    return pl.pallas_call(
        functools.partial(_gather_kernel, G=G),
        grid_spec=grid_spec,
        out_shape=jax.ShapeDtypeStruct((n, D), x.dtype),
        compiler_params=_cp(1),
        name="gather_rows",
    )(src, x)


def _expert_kernel(be_ref, nu_ref, x_ref, wgu_ref, wd_ref, o_ref, *, tf):
    i = pl.program_id(0)
    valid = i < nu_ref[0]

    @pl.when(valid)
    def _():
        xb = x_ref[...].astype(BF16)
        for f0 in range(0, D_FF_EXPERT, tf):
            gate = jnp.dot(xb, wgu_ref[:, f0:f0 + tf], preferred_element_type=F32)
            up = jnp.dot(xb, wgu_ref[:, D_FF_EXPERT + f0:D_FF_EXPERT + f0 + tf], preferred_element_type=F32)
            part = jnp.dot((_silu(gate) * up).astype(BF16), wd_ref[f0:f0 + tf, :], preferred_element_type=F32)
            if f0 == 0:
                o_ref[...] = part
            else:
                o_ref[...] += part

    @pl.when(jnp.logical_not(valid))
    def _():
        o_ref[...] = jnp.zeros_like(o_ref)


def expert_ffn(xs, w_gu, w_down, layer, blk_e, n_used, tm, tf=512):
    n = xs.shape[0]
    grid_spec = pltpu.PrefetchScalarGridSpec(
        num_scalar_prefetch=2,
        grid=(n // tm,),
        in_specs=[pl.BlockSpec((tm, D), lambda i, be, nu: (i, 0)),
                  pl.BlockSpec((None, None, D, 2 * D_FF_EXPERT), lambda i, be, nu: (layer, be[i], 0, 0),
                               pipeline_mode=pl.Buffered(1)),
                  pl.BlockSpec((None, None, D_FF_EXPERT, D), lambda i, be, nu: (layer, be[i], 0, 0),
                               pipeline_mode=pl.Buffered(1))],
        out_specs=pl.BlockSpec((tm, D), lambda i, be, nu: (i, 0)),
    )
    return pl.pallas_call(
        functools.partial(_expert_kernel, tf=tf),
        grid_spec=grid_spec,
        out_shape=jax.ShapeDtypeStruct((n, D), F32),
        compiler_params=_cp(1),
        name="expert_ffn",
    )(blk_e, n_used, xs, w_gu, w_down)


def _combine_kernel(d0_ref, d1_ref, yb_hbm, x_ref, rt_ref, g_ref, b_ref, o_ref, buf0, buf1, sem, *, G):
    base = pl.program_id(0) * G

    def issue(r, carry):
        pltpu.make_async_copy(yb_hbm.at[pl.ds(d0_ref[base + r], 1), :], buf0.at[pl.ds(r, 1), :], sem).start()
        pltpu.make_async_copy(yb_hbm.at[pl.ds(d1_ref[base + r], 1), :], buf1.at[pl.ds(r, 1), :], sem).start()
        return carry

    lax.fori_loop(0, G, issue, 0, unroll=4)

    def drain(r, carry):
        pltpu.make_async_copy(yb_hbm.at[pl.ds(0, 1), :], buf0.at[pl.ds(r, 1), :], sem).wait()
        pltpu.make_async_copy(yb_hbm.at[pl.ds(0, 1), :], buf1.at[pl.ds(r, 1), :], sem).wait()
        return carry

    lax.fori_loop(0, G, drain, 0, unroll=4)
    rt = rt_ref[...]
    y = rt[:, 2:3] * buf0[...] + rt[:, 3:4] * buf1[...]
    o_ref[...] = _layer_norm(ALPHA * x_ref[...] + y, g_ref[...], b_ref[...])


def combine_ln(yb, x, route, d0, d1, g, b, G):
    m = x.shape[0]
    grid_spec = pltpu.PrefetchScalarGridSpec(
        num_scalar_prefetch=2,
        grid=(m // G,),
        in_specs=[pl.BlockSpec(memory_space=pl.ANY),
                  pl.BlockSpec((G, D), lambda i, a, c: (i, 0)),
                  pl.BlockSpec((G, LANES), lambda i, a, c: (i, 0)),
                  pl.BlockSpec((1, D), lambda i, a, c: (0, 0)),
                  pl.BlockSpec((1, D), lambda i, a, c: (0, 0))],
        out_specs=pl.BlockSpec((G, D), lambda i, a, c: (i, 0)),
        scratch_shapes=[pltpu.VMEM((G, D), F32), pltpu.VMEM((G, D), F32), pltpu.SemaphoreType.DMA(())],
    )
    return pl.pallas_call(
        functools.partial(_combine_kernel, G=G),
        grid_spec=grid_spec,
        out_shape=jax.ShapeDtypeStruct((m, D), F32),
        compiler_params=_cp(1),
        name="combine_ln",
    )(d0, d1, yb, x, route, g.reshape(1, D), b.reshape(1, D))


def moe_ln(x, w_router, w_gu, w_down, layer, g, b, tm=512):
    m = x.shape[0]
    route = router(x, w_router, 640)
    flat_e = route[:, :2].astype(jnp.int32).reshape(-1)
    onehot = (flat_e[:, None] == jnp.arange(N_EXPERTS)[None, :]).astype(jnp.int32)
    csum = jnp.cumsum(onehot, axis=0)
    counts = csum[-1]
    padded = (counts + tm - 1) // tm * tm
    p_end = jnp.cumsum(padded)
    dest = jnp.sum(onehot * ((p_end - padded)[None, :] + csum - 1), axis=1)
    n_blocks = -(-(2 * m + N_EXPERTS * (tm - 1)) // tm)
    n_rows = n_blocks * tm
    n_used = (p_end[-1] // tm).astype(jnp.int32).reshape(1)
    blk = jnp.minimum(jnp.arange(n_blocks), n_used[0] - 1) * tm
    blk_e = jnp.minimum(jnp.sum((blk[:, None] >= p_end[None, :]).astype(jnp.int32), axis=1), N_EXPERTS - 1)
    x_z = jnp.concatenate([x, jnp.zeros((8, D), x.dtype)], axis=0)
    src = jnp.full((n_rows,), m, jnp.int32).at[dest].set(jnp.arange(2 * m, dtype=jnp.int32) // 2)
    xs = gather_rows(x_z, src, tm)
    yb = expert_ffn(xs, w_gu, w_down, layer, blk_e, n_used, tm)
    d = dest.reshape(m, 2).astype(jnp.int32)
    return combine_ln(yb, x, route, d[:, 0], d[:, 1], g, b, 640)


def kernel(x_prompt, x_sample, mem_prompt, page_table, cache_diff_k, cache_diff_v, cache_mem_k, cache_mem_v,
           state_mlstm_C, state_mlstm_n, state_mlstm_m, state_gdn_S, state_gdn_conv, rel_bias,
           w_in_a, b_gate_a, norm_a, w_in_b, lambda_b, norm_b, w_in_c, conv_c, a_log_c, dt_bias_c, norm_c,
           w_mem_kv, w_out, ln_g, ln_b, w_ffn_gu, w_ffn_down, w_router, w_exp_gu, w_exp_down):
    mp = BATCH * SEQ
    ms = DEC_BATCH * DEC_PAD
    x_p = x_prompt.reshape(mp, D)
    x_s = jnp.pad(x_sample, ((0, 0), (0, DEC_PAD - DEC_SEQ), (0, 0))).reshape(ms, D)
    mem2 = mem_prompt.reshape(BATCH * MEM_TOKENS, D)
    n_phys = cache_diff_k.shape[1]
    w_exp_gu_b = w_exp_gu.astype(BF16)
    w_exp_down_b = w_exp_down.astype(BF16)
    cmk = cache_mem_k.reshape(DEPTH * DEC_BATCH, MEM_TOKENS * MEM_HEADS, LANES)
    cmv = cache_mem_v.reshape(DEPTH * DEC_BATCH, MEM_TOKENS * MEM_HEADS, LANES)

    def sample_rows(t):
        return t.reshape(DEC_BATCH, DEC_PAD, -1)[:, :DEC_SEQ]

    def pad_gates(w):
        return jnp.pad(w, ((0, 0), (0, LANES - w.shape[1])))

    mlstm_p, mlstm_s, diff_p, diff_s, gdn_p, gdn_s, mem_k_new, mem_v_new = [], [], [], [], [], [], [], []
    for i in range(DEPTH):
        kind, j = i % 3, i // 3
        mem_kv = matmul(mem2, w_mem_kv[i], MEM_TOKENS, 512).reshape(BATCH, MEM_TOKENS, 2 * MEM_Q)
        mem_k_new.append(mem_kv[:, :, :MEM_Q].reshape(BATCH, MEM_TOKENS, MEM_HEADS, 128))
        mem_v_new.append(mem_kv[:, :, MEM_Q:].reshape(BATCH, MEM_TOKENS, MEM_HEADS, 128))
        if kind == 0:
            w = w_in_a[j]
            w_main = jnp.concatenate([w[:, :3 * D], w[:, 3 * D + 2 * H:]], axis=1)
            w_g = pad_gates(w[:, 3 * D:3 * D + 2 * H])
            u_p, u_s = matmul(x_p, w_main, 512, 512), matmul(x_s, w_main, ms, 512)
            g_p, g_s = matmul(x_p, w_g, 512, LANES), matmul(x_s, w_g, ms, LANES)
            zc = jnp.zeros((BATCH, H, 128, 64), F32)
            zn = jnp.zeros((BATCH, H, 64), F32)
            zm = jnp.zeros((BATCH, H), F32)
            mix_p, c_p, n_p, m_p = mlstm(u_p, g_p, b_gate_a[j], norm_a[j], zc, zn, zm,
                                         BATCH, SEQ, CHUNK, 256, CHUNK)
            mix_s, c_s, n_s, m_s = mlstm(u_s, g_s, b_gate_a[j], norm_a[j], state_mlstm_C[j], state_mlstm_n[j],
                                         state_mlstm_m[j], DEC_BATCH, DEC_PAD, DEC_PAD, DEC_PAD, DEC_SEQ)
            mlstm_p.append((c_p, n_p, m_p.reshape(BATCH, H)))
            mlstm_s.append((c_s, n_s, m_s.reshape(DEC_BATCH, H)))
            mq_col = 6
        elif kind == 1:
            lam_init = 0.8 - 0.6 * math.exp(-0.3 * i)
            w_main = w_in_b[j]
            u_p, u_s = matmul(x_p, w_main, 512, 512), matmul(x_s, w_main, ms, 512)
            mix_p = diff_attn_prompt(u_p, lambda_b[j], norm_b[j], rel_bias, lam_init)
            mix_s = diff_attn_sample(u_s, cache_diff_k[j].reshape(n_phys, PAGE * H, LANES),
                                     cache_diff_v[j].reshape(n_phys, PAGE * H, LANES), page_table,
                                     lambda_b[j], norm_b[j], rel_bias, lam_init)
            k_rows, v_rows = kv_head_rows(u_p, 256)
            diff_p.append((k_rows.reshape(BATCH, SEQ, H, 128), v_rows.reshape(BATCH, SEQ, H, 128)))
            diff_s.append((sample_rows(u_s[:, D:2 * D]).reshape(DEC_BATCH, DEC_SEQ, H, 128),
                           sample_rows(u_s[:, 2 * D:3 * D]).reshape(DEC_BATCH, DEC_SEQ, H, 128)))
            mq_col = 6
        else:
            w = w_in_c[j]
            w_main = jnp.concatenate([w[:, :4 * D], w[:, 4 * D + 2 * H:]], axis=1)
            w_g = pad_gates(w[:, 4 * D:4 * D + 2 * H])
            u_p, u_s = matmul(x_p, w_main, 512, 512), matmul(x_s, w_main, ms, 512)
            g_p, g_s = matmul(x_p, w_g, 512, LANES), matmul(x_s, w_g, ms, LANES)
            zs = jnp.zeros((BATCH, H, 128, 128), F32)
            zv = jnp.zeros((BATCH, 3, 3 * D), F32)
            mix_p, s_p = gdn(u_p, g_p, conv_c[j], a_log_c[j], dt_bias_c[j], norm_c[j], zs, zv,
                             BATCH, SEQ, CHUNK, 256, CHUNK)
            mix_s, s_s = gdn(u_s, g_s, conv_c[j], a_log_c[j], dt_bias_c[j], norm_c[j], state_gdn_S[j],
                             state_gdn_conv[j], DEC_BATCH, DEC_PAD, DEC_PAD, DEC_PAD, DEC_SEQ)
            conv_p = u_p[:, :3 * D].reshape(BATCH, SEQ, 3 * D)[:, SEQ - 3:]
            conv_s = sample_rows(u_s[:, :3 * D])[:, DEC_SEQ - 3:]
            gdn_p.append((s_p, conv_p))
            gdn_s.append((s_s, conv_s))
            mq_col = 8
        n_main = u_p.shape[1]
        mo_p = mem_attention(u_p.reshape(BATCH, SEQ, n_main), mq_col, mem_kv, 0, mem_kv, 1, 512).reshape(mp, MEM_Q)
        mo_s = mem_attention(u_s.reshape(DEC_BATCH, DEC_PAD, n_main), mq_col, cmk, 0, cmv, 0, DEC_PAD,
                             seq0=i * DEC_BATCH, head_rows=True).reshape(ms, MEM_Q)
        x_p = outproj_ln(mix_p, mo_p, x_p, w_out[i], ln_g[i, 0], ln_b[i, 0], 512)
        x_s = outproj_ln(mix_s, mo_s, x_s, w_out[i], ln_g[i, 0], ln_b[i, 0], ms)
        k_f = i // 2
        if i % 2 == 0:
            x_p = ffn_ln(x_p, w_ffn_gu[k_f], w_ffn_down[k_f], ln_g[i, 1], ln_b[i, 1], 1024)
            x_s = ffn_ln(x_s, w_ffn_gu[k_f], w_ffn_down[k_f], ln_g[i, 1], ln_b[i, 1], ms)
        else:
            x_all = moe_ln(jnp.concatenate([x_p, x_s], axis=0), w_router[k_f], w_exp_gu_b, w_exp_down_b, k_f,
                           ln_g[i, 1], ln_b[i, 1])
            x_p, x_s = x_all[:mp], x_all[mp:]

    def stack(lst, k):
        return jnp.stack([t[k] for t in lst])

    return (x_p.reshape(BATCH, SEQ, D), sample_rows(x_s),
            stack(mlstm_p, 0), stack(mlstm_p, 1), stack(mlstm_p, 2),
            stack(mlstm_s, 0), stack(mlstm_s, 1), stack(mlstm_s, 2),
            stack(diff_p, 0), stack(diff_p, 1), stack(diff_s, 0), stack(diff_s, 1),
            stack(gdn_p, 0), stack(gdn_p, 1), stack(gdn_s, 0), stack(gdn_s, 1),
            jnp.stack(mem_k_new), jnp.stack(mem_v_new))
```

```python
import functools
import math

import jax
import jax.numpy as jnp
from jax import lax
from jax.experimental import pallas as pl
from jax.experimental.pallas import tpu as pltpu

F32 = jnp.float32
BF16 = jnp.bfloat16
HI = lax.Precision.HIGHEST
NT = (((1,), (1,)), ((), ()))
TN = (((0,), (0,)), ((), ()))

D = 1024
H = 8
DEPTH = 4
SEQ = 4096
BATCH = 4
DEC_BATCH = 32
DEC_SEQ = 4
DEC_PAD = 8
PAST_LEN = 8192
PAGE = 128
N_PAGES = PAST_LEN // PAGE
PAGES_PER_STEP = 4
MEM_TOKENS = 256
MEM_Q = 512
MEM_HEADS = 4
CHUNK = 64
N_EXPERTS = 8
D_FF = 2752
D_FF_PAD = 2816
D_FF_EXPERT = 3584
EPS = 1e-6
ALPHA = (2 * DEPTH) ** 0.25
NEG = -1e30
LANES = 128
VMEM_LIMIT = 56 * 1024 * 1024


def _cp(n_axes, vmem=VMEM_LIMIT):
    return pltpu.CompilerParams(dimension_semantics=("arbitrary",) * n_axes, vmem_limit_bytes=vmem)


def _dot(a, b):
    return jnp.dot(a.astype(BF16), b.astype(BF16), preferred_element_type=F32)


def _dot_nt(a, b):
    return lax.dot_general(a.astype(BF16), b.astype(BF16), NT, preferred_element_type=F32)


def _dot_tn(a, b):
    return lax.dot_general(a.astype(BF16), b.astype(BF16), TN, preferred_element_type=F32)


def _dot_hi(a, b):
    return jnp.dot(a, b, preferred_element_type=F32, precision=HI)


def _split_bf16(a):
    hi = a.astype(BF16)
    return hi, (a - hi.astype(F32)).astype(BF16)


def _dot_split(a_hi, a_lo, b_hi, b_lo):
    return (jnp.dot(a_hi, b_hi, preferred_element_type=F32)
            + (jnp.dot(a_lo, b_hi, preferred_element_type=F32) + jnp.dot(a_hi, b_lo, preferred_element_type=F32)))


def _sigmoid(x):
    return 1.0 / (1.0 + jnp.exp(-x))


def _silu(x):
    return x * _sigmoid(x)


def _softplus(x):
    return jnp.maximum(x, 0.0) + jnp.log1p(jnp.exp(-jnp.abs(x)))


def _log_sigmoid(x):
    return -_softplus(-x)


def _layer_norm(v, g, b):
    mu = jnp.mean(v, axis=-1, keepdims=True)
    d = v - mu
    var = jnp.mean(d * d, axis=-1, keepdims=True)
    return d * lax.rsqrt(var + EPS) * g + b


def _rms(h):
    return h * lax.rsqrt(jnp.mean(h * h, axis=-1, keepdims=True) + EPS)


def _mm_kernel(x_ref, w_ref, o_ref, *, tn):
    xb = x_ref[...].astype(BF16)
    for n0 in range(0, o_ref.shape[1], tn):
        o_ref[:, n0:n0 + tn] = jnp.dot(xb, w_ref[:, n0:n0 + tn], preferred_element_type=F32)


def matmul(x, w, tm, tn):
    m, k = x.shape
    n = w.shape[1]
    return pl.pallas_call(
        functools.partial(_mm_kernel, tn=min(tn, n)),
        grid=(m // tm,),
        in_specs=[pl.BlockSpec((tm, k), lambda i: (i, 0)),
                  pl.BlockSpec((k, n), lambda i: (0, 0), pipeline_mode=pl.Buffered(1))],
        out_specs=pl.BlockSpec((tm, n), lambda i: (i, 0)),
        out_shape=jax.ShapeDtypeStruct((m, n), F32),
        compiler_params=_cp(1),
        name="matmul",
    )(x, w.astype(BF16))


def _headrows_kernel(k_ref, v_ref, ko_ref, vo_ref):
    tm = k_ref.shape[0]
    for h in range(H):
        sl = slice(h * LANES, (h + 1) * LANES)
        ko_ref[pl.ds(h, tm, stride=H), :] = k_ref[:, sl]
        vo_ref[pl.ds(h, tm, stride=H), :] = v_ref[:, sl]


def kv_head_rows(u, tm):
    m = u.shape[0]
    return pl.pallas_call(
        _headrows_kernel,
        grid=(m // tm,),
        in_specs=[pl.BlockSpec((tm, D), lambda i: (i, 1)),
                  pl.BlockSpec((tm, D), lambda i: (i, 2))],
        out_specs=[pl.BlockSpec((tm * H, LANES), lambda i: (i, 0)),
                   pl.BlockSpec((tm * H, LANES), lambda i: (i, 0))],
        out_shape=[jax.ShapeDtypeStruct((m * H, LANES), F32), jax.ShapeDtypeStruct((m * H, LANES), F32)],
        compiler_params=_cp(1),
        name="kv_head_rows",
    )(u, u)


def _outproj_kernel(mix_ref, mo_ref, x_ref, w1_ref, w2_ref, g_ref, b_ref, o_ref):
    y = _dot(mix_ref[...], w1_ref[...]) + _dot(mo_ref[...], w2_ref[...])
    o_ref[...] = _layer_norm(ALPHA * x_ref[...] + y, g_ref[...], b_ref[...])


def outproj_ln(mix, mo, x, w_out, g, b, tm):
    m = x.shape[0]
    w1 = w_out[:D].astype(BF16)
    w2 = w_out[D:].astype(BF16)
    return pl.pallas_call(
        _outproj_kernel,
        grid=(m // tm,),
        in_specs=[pl.BlockSpec((tm, D), lambda i: (i, 0)),
                  pl.BlockSpec((tm, MEM_Q), lambda i: (i, 0)),
                  pl.BlockSpec((tm, D), lambda i: (i, 0)),
                  pl.BlockSpec((D, D), lambda i: (0, 0)),
                  pl.BlockSpec((MEM_Q, D), lambda i: (0, 0)),
                  pl.BlockSpec((1, D), lambda i: (0, 0)),
                  pl.BlockSpec((1, D), lambda i: (0, 0))],
        out_specs=pl.BlockSpec((tm, D), lambda i: (i, 0)),
        out_shape=jax.ShapeDtypeStruct((m, D), F32),
        compiler_params=_cp(1),
        name="outproj_ln",
    )(mix, mo, x, w1, w2, g.reshape(1, D), b.reshape(1, D))


def _ffn_kernel(x_ref, wgu_ref, wd_ref, g_ref, b_ref, o_ref, *, tf):
    x = x_ref[...]
    xb = x.astype(BF16)
    for f0 in range(0, D_FF_PAD, tf):
        gate = jnp.dot(xb, wgu_ref[:, f0:f0 + tf], preferred_element_type=F32)
        up = jnp.dot(xb, wgu_ref[:, D_FF_PAD + f0:D_FF_PAD + f0 + tf], preferred_element_type=F32)
        part = jnp.dot((_silu(gate) * up).astype(BF16), wd_ref[f0:f0 + tf, :], preferred_element_type=F32)
        if f0 == 0:
            o_ref[...] = part
        else:
            o_ref[...] += part
    o_ref[...] = _layer_norm(ALPHA * x + o_ref[...], g_ref[...], b_ref[...])


def ffn_ln(x, w_gu, w_down, g, b, tm, tf=256):
    m = x.shape[0]
    pad = D_FF_PAD - D_FF
    wgu = jnp.concatenate([jnp.pad(w_gu[:, :D_FF], ((0, 0), (0, pad))),
                           jnp.pad(w_gu[:, D_FF:], ((0, 0), (0, pad)))], axis=1).astype(BF16)
    wd = jnp.pad(w_down, ((0, pad), (0, 0))).astype(BF16)
    return pl.pallas_call(
        functools.partial(_ffn_kernel, tf=tf),
        grid=(m // tm,),
        in_specs=[pl.BlockSpec((tm, D), lambda i: (i, 0)),
                  pl.BlockSpec((D, 2 * D_FF_PAD), lambda i: (0, 0), pipeline_mode=pl.Buffered(1)),
                  pl.BlockSpec((D_FF_PAD, D), lambda i: (0, 0), pipeline_mode=pl.Buffered(1)),
                  pl.BlockSpec((1, D), lambda i: (0, 0)),
                  pl.BlockSpec((1, D), lambda i: (0, 0))],
        out_specs=pl.BlockSpec((tm, D), lambda i: (i, 0)),
        out_shape=jax.ShapeDtypeStruct((m, D), F32),
        compiler_params=_cp(1),
        name="ffn_ln",
    )(x, wgu, wd, g.reshape(1, D), b.reshape(1, D))


def _memattn_kernel(q_ref, k_ref, v_ref, o_ref, *, head_rows):
    q = q_ref[...]
    for h in range(MEM_HEADS):
        sl = slice(h * LANES, (h + 1) * LANES)
        if head_rows:
            k = k_ref[pl.ds(h, MEM_TOKENS, stride=MEM_HEADS), :]
            v = v_ref[pl.ds(h, MEM_TOKENS, stride=MEM_HEADS), :]
        else:
            k = k_ref[:, sl]
            v = v_ref[:, sl]
        s = _dot_nt(q[:, sl], k) * LANES ** -0.5
        s = s - jnp.max(s, axis=-1, keepdims=True)
        p = jnp.exp(s)
        p = p / jnp.sum(p, axis=-1, keepdims=True)
        o_ref[:, sl] = _dot(p, v)


def mem_attention(u, q_col, mem_k, k_col, mem_v, v_col, tq, seq0=0, head_rows=False):
    b, l, _ = u.shape
    if head_rows:
        kv_block = (None, MEM_TOKENS * MEM_HEADS, LANES)
    else:
        kv_block = (None, MEM_TOKENS, MEM_Q)
    return pl.pallas_call(
        functools.partial(_memattn_kernel, head_rows=head_rows),
        grid=(b, l // tq),
        in_specs=[pl.BlockSpec((None, tq, MEM_Q), lambda bi, i: (bi, i, q_col)),
                  pl.BlockSpec(kv_block, lambda bi, i: (bi + seq0, 0, k_col)),
                  pl.BlockSpec(kv_block, lambda bi, i: (bi + seq0, 0, v_col))],
        out_specs=pl.BlockSpec((None, tq, MEM_Q), lambda bi, i: (bi, i, 0)),
        out_shape=jax.ShapeDtypeStruct((b, l, MEM_Q), F32),
        compiler_params=_cp(2),
        name="mem_attention",
    )(u, mem_k, mem_v)


def _mlstm_kernel(qk_ref, v_ref, og_ref, gc_ref, gt_ref, bgc_ref, bgr_ref, ng_ref, c0_ref, n0_ref, m0_ref,
                  mix_ref, co_ref, no_ref, mo_ref, c_scr, n_scr, m_scr, *, T, nc, n_valid, sps):
    s = pl.program_id(0)

    @pl.when(s % sps == 0)
    def _():
        c_scr[...] = c0_ref[...]
        n_scr[...] = n0_ref[...]
        m_scr[...] = m0_ref[...]

    row = lax.broadcasted_iota(jnp.int32, (T, T), 0)
    col = lax.broadcasted_iota(jnp.int32, (T, T), 1)
    causal = col <= row
    tril = causal.astype(F32)
    triu = (row <= col).astype(F32)

    cs = range(nc)
    hs = range(H)
    ch = [(c, h) for c in cs for h in hs]
    rows = [slice(c * T, (c + 1) * T) for c in cs]
    g, bcol, brow, igr_all = [], [], [], []
    for c in cs:
        g_c = gc_ref[rows[c], :] + bgc_ref[...]
        lfc = _log_sigmoid(g_c)
        gt = gt_ref[c] + bgr_ref[...]
        igr = gt[0:H, :]
        lfr = _log_sigmoid(gt[H:2 * H, :])
        if n_valid < T:
            rid = lax.broadcasted_iota(jnp.int32, (T, LANES), 0)
            g_c = jnp.where(rid < n_valid, g_c, NEG)
            lfc = jnp.where(rid < n_valid, lfc, 0.0)
            cid = lax.broadcasted_iota(jnp.int32, (H, T), 1)
            igr = jnp.where(cid < n_valid, igr, NEG)
            lfr = jnp.where(cid < n_valid, lfr, 0.0)
        g.append(g_c)
        igr_all.append(igr)
        bcol.append(_dot_hi(tril, lfc))
        brow.append(_dot_hi(lfr, triu))
    bc = {(c, h): bcol[c][:, H + h:H + h + 1] for c, h in ch}
    log_d = {(c, h): jnp.where(causal, bc[c, h] - brow[c][h:h + 1, :] + igr_all[c][h:h + 1, :], -jnp.inf)
             for c, h in ch}
    rmax = {(c, h): jnp.max(log_d[c, h], axis=-1, keepdims=True) for c, h in ch}
    m_all = m_scr[...]
    m_prev = [m_all[:, h:h + 1] for h in hs]
    inter, m_t = {}, {}
    for c, h in ch:
        inter[c, h] = bc[c, h] + m_prev[h]
        m_t[c, h] = jnp.maximum(inter[c, h], rmax[c, h])
        m_prev[h] = m_t[c, h][T - 1:T, :]
    qk = [qk_ref[rows[c], :] for c in cs]
    vv = [v_ref[rows[c], :] for c in cs]
    q = {(c, h): (qk[c][:, h * 64:(h + 1) * 64] * 0.125).astype(BF16) for c, h in ch}
    k = {(c, h): qk[c][:, 512 + h * 64:512 + (h + 1) * 64] for c, h in ch}
    kb = {(c, h): k[c, h].astype(BF16) for c, h in ch}
    v = {(c, h): vv[c][:, h * LANES:(h + 1) * LANES] for c, h in ch}
    qkt = {(c, h): lax.dot_general(q[c, h], kb[c, h], NT, preferred_element_type=F32) for c, h in ch}
    w_inter = {(c, h): jnp.exp(inter[c, h] - m_t[c, h]) for c, h in ch}
    sc = {(c, h): qkt[c, h] * jnp.exp(log_d[c, h] - m_t[c, h]) for c, h in ch}
    sv = {(c, h): jnp.dot(sc[c, h].astype(BF16), v[c, h].astype(BF16), preferred_element_type=F32) for c, h in ch}
    den0 = {(c, h): jnp.sum(sc[c, h], axis=-1, keepdims=True) for c, h in ch}
    w_end = {(c, h): jnp.exp(bc[c, h][T - 1:T, :] - bc[c, h] + g[c][:, h:h + 1] - m_t[c, h][T - 1:T, :])
             for c, h in ch}
    vk = {(c, h): lax.dot_general((w_end[c, h] * v[c, h]).astype(BF16), kb[c, h], TN, preferred_element_type=F32)
          for c, h in ch}
    kw = {(c, h): jnp.sum(w_end[c, h] * k[c, h], axis=0, keepdims=True) for c, h in ch}
    c_in = {(0, h): c_scr[h] for h in hs}
    n_in = {(0, h): n_scr[h:h + 1, :] for h in hs}
    for c, h in ch:
        decay = w_inter[c, h][T - 1:T, :]
        c_in[c + 1, h] = decay * c_in[c, h] + vk[c, h]
        n_in[c + 1, h] = decay * n_in[c, h] + kw[c, h]
    qc = {(c, h): lax.dot_general(q[c, h], c_in[c, h].astype(BF16), NT, preferred_element_type=F32) for c, h in ch}
    for c, h in ch:
        num = sv[c, h] + w_inter[c, h] * qc[c, h]
        den = den0[c, h] + w_inter[c, h] * jnp.sum(q[c, h].astype(F32) * n_in[c, h], axis=-1, keepdims=True)
        hh = num / jnp.maximum(jnp.abs(den), jnp.exp(-m_t[c, h]))
        sl = slice(h * LANES, (h + 1) * LANES)
        mix_ref[rows[c], sl] = _sigmoid(og_ref[rows[c], sl]) * (_rms(hh) * ng_ref[:, sl])
    for h in hs:
        c_scr[h] = c_in[nc, h]
        n_scr[h:h + 1, :] = n_in[nc, h]
        m_scr[:, h:h + 1] = m_prev[h]

    @pl.when(s % sps == sps - 1)
    def _():
        co_ref[...] = c_scr[...]
        no_ref[...] = n_scr[...]
        mo_ref[...] = m_scr[...]


def mlstm(u, gates, b_gate, norm_g, c0, n0, m0, nseq, seqlen, T, rows_per_step, n_valid):
    m = u.shape[0]
    nc = rows_per_step // T
    sps = seqlen // rows_per_step
    gt = jnp.transpose(gates[:, :2 * H].reshape(m // T, T, 2 * H), (0, 2, 1))
    bgc = jnp.zeros((1, LANES), F32).at[0, :2 * H].set(b_gate.reshape(-1))
    bgr = b_gate.reshape(2 * H, 1)
    r = rows_per_step
    kern = functools.partial(_mlstm_kernel, T=T, nc=nc, n_valid=n_valid, sps=sps)
    return pl.pallas_call(
        kern,
        grid=(m // r,),
        in_specs=[pl.BlockSpec((r, D), lambda s: (s, 0)),
                  pl.BlockSpec((r, D), lambda s: (s, 1)),
                  pl.BlockSpec((r, D), lambda s: (s, 2)),
                  pl.BlockSpec((r, LANES), lambda s: (s, 0)),
                  pl.BlockSpec((nc, 2 * H, T), lambda s: (s, 0, 0)),
                  pl.BlockSpec((1, LANES), lambda s: (0, 0)),
                  pl.BlockSpec((2 * H, 1), lambda s: (0, 0)),
                  pl.BlockSpec((1, D), lambda s: (0, 0)),
                  pl.BlockSpec((None, H, 128, 64), lambda s: (s // sps, 0, 0, 0)),
                  pl.BlockSpec((None, H, 64), lambda s: (s // sps, 0, 0)),
                  pl.BlockSpec((None, 1, H), lambda s: (s // sps, 0, 0))],
        out_specs=[pl.BlockSpec((r, D), lambda s: (s, 0)),
                   pl.BlockSpec((None, H, 128, 64), lambda s: (s // sps, 0, 0, 0)),
                   pl.BlockSpec((None, H, 64), lambda s: (s // sps, 0, 0)),
                   pl.BlockSpec((None, 1, H), lambda s: (s // sps, 0, 0))],
        out_shape=[jax.ShapeDtypeStruct((m, D), F32),
                   jax.ShapeDtypeStruct((nseq, H, 128, 64), F32),
                   jax.ShapeDtypeStruct((nseq, H, 64), F32),
                   jax.ShapeDtypeStruct((nseq, 1, H), F32)],
        scratch_shapes=[pltpu.VMEM((H, 128, 64), F32), pltpu.VMEM((H, 64), F32), pltpu.VMEM((1, H), F32)],
        compiler_params=_cp(1),
        name="mlstm",
    )(u, u, u, gates, gt, bgc, bgr, norm_g.reshape(1, D), c0, n0, m0.reshape(nseq, 1, H))


def _gdn_kernel(x_ref, z_ref, gc_ref, gt_ref, cw_ref, pc_ref, pr_ref, ng_ref, s0_ref, cv0_ref,
                mix_ref, so_ref, xbuf, cv_scr, s_scr, *, T, nc, n_valid, sps):
    s = pl.program_id(0)
    R = T * nc

    @pl.when(s % sps == 0)
    def _():
        s_scr[...] = s0_ref[...]
        xbuf[5:8, :] = cv0_ref[...]

    xbuf[8:8 + R, :] = x_ref[...]
    conv = (cw_ref[0:1, :] * xbuf[5:5 + R, :] + cw_ref[1:2, :] * xbuf[6:6 + R, :]
            + cw_ref[2:3, :] * xbuf[7:7 + R, :] + cw_ref[3:4, :] * xbuf[8:8 + R, :])
    cv_scr[...] = _silu(conv)
    xbuf[0:8, :] = xbuf[R:R + 8, :]

    row = lax.broadcasted_iota(jnp.int32, (T, T), 0)
    col = lax.broadcasted_iota(jnp.int32, (T, T), 1)
    causal = col <= row
    strict = col < row
    tril = causal.astype(F32)
    triu = (row <= col).astype(F32)

    def chunk(c, carry):
        r0 = pl.multiple_of(c * T, T)
        rows = pl.ds(r0, T)
        gpre = gc_ref[rows, :]
        beta_all = _sigmoid(gpre)
        g_all = -jnp.exp(pc_ref[0:1, :]) * _softplus(gpre + pc_ref[1:2, :])
        gt = gt_ref[c]
        gr_all = -jnp.exp(pr_ref[:, 0:1]) * _softplus(gt + pr_ref[:, 1:2])
        gr = gr_all[H:2 * H, :]
        if n_valid < T:
            rid = lax.broadcasted_iota(jnp.int32, (T, LANES), 0)
            beta_all = jnp.where(rid < n_valid, beta_all, 0.0)
            g_all = jnp.where(rid < n_valid, g_all, 0.0)
            cid = lax.broadcasted_iota(jnp.int32, (H, T), 1)
            gr = jnp.where(cid < n_valid, gr, 0.0)
        gamc_all = _dot_hi(tril, g_all)
        gamr_all = _dot_hi(gr, triu)
        hs = range(H)
        qf = [cv_scr[rows, h * LANES:(h + 1) * LANES] for h in hs]
        kf = [cv_scr[rows, D + h * LANES:D + (h + 1) * LANES] for h in hs]
        vf = [cv_scr[rows, 2 * D + h * LANES:2 * D + (h + 1) * LANES] for h in hs]
        qb = [(qf[h] * lax.rsqrt(jnp.sum(qf[h] * qf[h], axis=-1, keepdims=True) + EPS) * LANES ** -0.5).astype(BF16)
              for h in hs]
        kn = [kf[h] * lax.rsqrt(jnp.sum(kf[h] * kf[h], axis=-1, keepdims=True) + EPS) for h in hs]
        kb = [kn[h].astype(BF16) for h in hs]
        beta = [beta_all[:, h:h + 1] for h in hs]
        gamc = [gamc_all[:, H + h:H + h + 1] for h in hs]
        egam = [jnp.exp(gamc[h]) for h in hs]
        dec = [jnp.exp(jnp.where(causal, gamc[h] - gamr_all[h:h + 1, :], NEG)) for h in hs]
        s_old = [s_scr[h] for h in hs]
        sb = [s_old[h].astype(BF16) for h in hs]
        qkk = [lax.dot_general(jnp.concatenate([qb[h], kb[h]], axis=0), kb[h], NT, preferred_element_type=F32)
               for h in hs]
        qs = [lax.dot_general(qb[h], sb[h], NT, preferred_element_type=F32) for h in hs]
        nmat = [jnp.where(strict, beta[h] * qkk[h][T:2 * T] * dec[h], 0.0) for h in hs]
        y = [-nmat[h] for h in hs]
        nsp = [_split_bf16(nmat[h]) for h in hs]
        span = 2
        while span < T:
            npow = [_dot_split(nsp[h][0], nsp[h][1], nsp[h][0], nsp[h][1]) for h in hs]
            nsp = [_split_bf16(npow[h]) for h in hs]
            ysp = [_split_bf16(y[h]) for h in hs]
            y = [y[h] + npow[h] + _dot_split(ysp[h][0], ysp[h][1], nsp[h][0], nsp[h][1]) for h in hs]
            span *= 2
        rhs = [jnp.concatenate([beta[h] * vf[h], (beta[h] * egam[h]) * kn[h]], axis=1) for h in hs]
        sol = [rhs[h] + _dot(y[h], rhs[h]) for h in hs]
        delta = [sol[h][:, 0:LANES] - _dot_nt(sol[h][:, LANES:2 * LANES], sb[h]) for h in hs]
        o = [egam[h] * qs[h] + _dot(qkk[h][0:T] * dec[h], delta[h]) for h in hs]
        g_end = [gamc[h][T - 1:T, :] for h in hs]
        upd = [_dot_tn(jnp.exp(g_end[h] - gamc[h]) * delta[h], kb[h]) for h in hs]
        for h in hs:
            sl = slice(h * LANES, (h + 1) * LANES)
            s_scr[h] = jnp.exp(g_end[h]) * s_old[h] + upd[h]
            mix_ref[rows, sl] = _rms(o[h]) * ng_ref[:, sl] * _silu(z_ref[rows, sl])
        return carry

    lax.fori_loop(0, nc, chunk, 0, unroll=min(nc, 2))

    @pl.when(s % sps == sps - 1)
    def _():
        so_ref[...] = s_scr[...]


def gdn(u, gates, conv_w, a_log, dt_bias, norm_g, s0, conv0, nseq, seqlen, T, rows_per_step, n_valid):
    m = u.shape[0]
    nc = rows_per_step // T
    sps = seqlen // rows_per_step
    r = rows_per_step
    gt = jnp.transpose(gates[:, :2 * H].reshape(m // T, T, 2 * H), (0, 2, 1))
    pc = jnp.zeros((2, LANES), F32).at[0, H:2 * H].set(a_log).at[1, H:2 * H].set(dt_bias)
    pr = jnp.zeros((2 * H, 2), F32).at[H:, 0].set(a_log).at[H:, 1].set(dt_bias)
    kern = functools.partial(_gdn_kernel, T=T, nc=nc, n_valid=n_valid, sps=sps)
    return pl.pallas_call(
        kern,
        grid=(m // r,),
        in_specs=[pl.BlockSpec((r, 3 * D), lambda s: (s, 0)),
                  pl.BlockSpec((r, D), lambda s: (s, 3)),
                  pl.BlockSpec((r, LANES), lambda s: (s, 0)),
                  pl.BlockSpec((nc, 2 * H, T), lambda s: (s, 0, 0)),
                  pl.BlockSpec((4, 3 * D), lambda s: (0, 0)),
                  pl.BlockSpec((2, LANES), lambda s: (0, 0)),
                  pl.BlockSpec((2 * H, 2), lambda s: (0, 0)),
                  pl.BlockSpec((1, D), lambda s: (0, 0)),
                  pl.BlockSpec((None, H, 128, 128), lambda s: (s // sps, 0, 0, 0)),
                  pl.BlockSpec((None, 3, 3 * D), lambda s: (s // sps, 0, 0))],
        out_specs=[pl.BlockSpec((r, D), lambda s: (s, 0)),
                   pl.BlockSpec((None, H, 128, 128), lambda s: (s // sps, 0, 0, 0))],
        out_shape=[jax.ShapeDtypeStruct((m, D), F32),
                   jax.ShapeDtypeStruct((nseq, H, 128, 128), F32)],
        scratch_shapes=[pltpu.VMEM((r + 8, 3 * D), F32), pltpu.VMEM((r, 3 * D), F32),
                        pltpu.VMEM((H, 128, 128), F32)],
        compiler_params=_cp(1),
        name="gdn",
    )(u, u, gates, gt, conv_w, pc, pr, norm_g.reshape(1, D), s0, conv0)


def _bucket(rel):
    n = jnp.maximum(rel, 0)
    nf = jnp.maximum(n, 1).astype(F32)
    large = 16 + (jnp.log(nf / 16) / math.log(128 / 16) * 16).astype(jnp.int32)
    return jnp.where(n < 16, n, jnp.minimum(large, 31))


def _diff_lambda(lw_ref, lam_init):
    lw = lw_ref[...]
    a = jnp.sum(lw[0:1, :] * lw[1:2, :], axis=-1, keepdims=True)
    b = jnp.sum(lw[2:3, :] * lw[3:4, :], axis=-1, keepdims=True)
    return jnp.exp(a) - jnp.exp(b) + lam_init


def _dattn_p_kernel(q_ref, k_ref, v_ref, bn_ref, lw_ref, ng_ref, o_ref, m_scr, acc_scr, *, TB, lam_init):
    i = pl.program_id(2)
    W = 2 * TB
    lane = lax.broadcasted_iota(jnp.int32, (TB, LANES), 1)
    q = q_ref[...] * 0.125
    qq = jnp.concatenate([jnp.where(lane < 64, q, 0.0), jnp.where(lane >= 64, q, 0.0)], axis=0).astype(BF16)
    m_scr[...] = jnp.full_like(m_scr, NEG)
    acc_scr[...] = jnp.zeros_like(acc_scr)

    def step(k0, width, bias):
        rows = pl.ds(pl.multiple_of(k0, TB), width)
        kb = k_ref[rows, :].astype(BF16)
        vext = jnp.concatenate([v_ref[rows, :].astype(BF16), jnp.ones((width, LANES), BF16)], axis=1)
        sc = lax.dot_general(qq, kb, NT, preferred_element_type=F32)
        if bias is not None:
            sc = sc + bias
        ng = width // LANES
        mloc = sc[:, 0:LANES]
        for g in range(1, ng):
            mloc = jnp.maximum(mloc, sc[:, g * LANES:(g + 1) * LANES])
        m_old = m_scr[...]
        m_new = jnp.maximum(m_old, jnp.max(mloc, axis=-1, keepdims=True))
        alpha = jnp.exp(m_old - m_new)
        p = jnp.exp(sc - jnp.concatenate([m_new] * ng, axis=1)).astype(BF16)
        acc_scr[...] = (jnp.concatenate([alpha, alpha], axis=1) * acc_scr[...]
                        + jnp.dot(p, vext, preferred_element_type=F32))
        m_scr[...] = m_new

    n_far = jnp.maximum(i - 1, 0)
    n_big = n_far // 4
    rem = n_far % 4

    def body(c, carry):
        step(c * 2 * W, W, None)
        step(c * 2 * W + W, W, None)
        return carry

    lax.fori_loop(0, n_big, body, 0)

    @pl.when(rem >= 2)
    def _():
        step(n_big * 2 * W, W, None)

    @pl.when(rem % 2 == 1)
    def _():
        step((n_far - 1) * TB, TB, None)

    step(n_far * TB, W, bn_ref[...])
    lam = _diff_lambda(lw_ref, lam_init)
    acc = acc_scr[...]
    o = (acc[0:TB, 0:LANES] / acc[0:TB, LANES:2 * LANES]
         - lam * (acc[TB:2 * TB, 0:LANES] / acc[TB:2 * TB, LANES:2 * LANES]))
    o_ref[...] = _rms(o) * ng_ref[...] * (1.0 - lam_init)


def _bias_table(rel_bias, rel):
    onehot = (_bucket(rel)[..., None] == jnp.arange(32)).astype(F32)
    return jnp.einsum('...k,kh->h...', onehot, rel_bias - rel_bias[31:32], precision=HI)


def diff_attn_prompt(u, lam_w, norm_g, rel_bias, lam_init, TB=256):
    nq = SEQ // TB
    ii = jnp.arange(TB)
    rel_d = ii[:, None] - ii[None, :]
    bd = jnp.where((rel_d >= 0)[None], _bias_table(rel_bias, rel_d), NEG)
    bs = _bias_table(rel_bias, rel_d + TB)
    masked = jnp.full((H, TB, TB), NEG, F32)
    bn = jnp.stack([jnp.concatenate([bd, masked], axis=2), jnp.concatenate([bs, bd], axis=2)], axis=1)
    bn = jnp.concatenate([bn, bn], axis=2)
    kern = functools.partial(_dattn_p_kernel, TB=TB, lam_init=lam_init)
    return pl.pallas_call(
        kern,
        grid=(BATCH, H, nq),
        in_specs=[pl.BlockSpec((TB, LANES), lambda b, h, i: (b * nq + i, h)),
                  pl.BlockSpec((SEQ, LANES), lambda b, h, i: (b, H + h)),
                  pl.BlockSpec((SEQ, LANES), lambda b, h, i: (b, 2 * H + h)),
                  pl.BlockSpec((None, None, 2 * TB, 2 * TB), lambda b, h, i: (h, jnp.minimum(i, 1), 0, 0)),
                  pl.BlockSpec((4, 64), lambda b, h, i: (0, 0)),
                  pl.BlockSpec((1, LANES), lambda b, h, i: (0, h))],
        out_specs=pl.BlockSpec((TB, LANES), lambda b, h, i: (b * nq + i, h)),
        out_shape=jax.ShapeDtypeStruct((BATCH * SEQ, D), F32),
        scratch_shapes=[pltpu.VMEM((2 * TB, LANES), F32), pltpu.VMEM((2 * TB, 2 * LANES), F32)],
        compiler_params=_cp(3),
        name="diff_attn_prompt",
    )(u, u, u, bn, lam_w, norm_g.reshape(1, D))


def _dattn_s_kernel(pt_ref, qm_ref, qb_ref, *refs, lam_init):
    del pt_ref
    kp_refs = refs[0:PAGES_PER_STEP]
    vp_refs = refs[PAGES_PER_STEP:2 * PAGES_PER_STEP]
    kn_ref, vn_ref, bl_ref, bn_ref, lw_ref, ng_ref, o_ref, m_scr, l_scr, acc_scr = refs[2 * PAGES_PER_STEP:]
    p = pl.program_id(1)
    last = pl.num_programs(1) - 1
    rows_h = 2 * DEC_PAD

    @pl.when(p == 0)
    def _():
        m_scr[...] = jnp.full_like(m_scr, NEG)
        l_scr[...] = jnp.zeros_like(l_scr)
        acc_scr[...] = jnp.zeros_like(acc_scr)

    def head_rows(page_refs, h):
        return jnp.concatenate([r[pl.ds(h, PAGE, stride=H), :].astype(BF16) for r in page_refs], axis=0)

    qm = qm_ref[...].astype(BF16)
    sc = jnp.concatenate(
        [lax.dot_general(qm[h * rows_h:(h + 1) * rows_h], head_rows(kp_refs, h), NT, preferred_element_type=F32)
         for h in range(H)], axis=0)
    sc = sc + jnp.where(p == last, bl_ref[...], 0.0)
    m_old = m_scr[...]
    m_new = jnp.maximum(m_old, jnp.max(sc, axis=-1, keepdims=True))
    alpha = jnp.exp(m_old - m_new)
    pr = jnp.exp(sc - jnp.concatenate([m_new] * PAGES_PER_STEP, axis=1))
    l_scr[...] = alpha * l_scr[...] + jnp.sum(pr, axis=-1, keepdims=True)
    prb = pr.astype(BF16)
    pv = jnp.concatenate(
        [jnp.dot(prb[h * rows_h:(h + 1) * rows_h], head_rows(vp_refs, h), preferred_element_type=F32)
         for h in range(H)], axis=0)
    acc_scr[...] = alpha * acc_scr[...] + pv
    m_scr[...] = m_new

    @pl.when(p == last)
    def _():
        scn = lax.dot_general(qb_ref[...], kn_ref[...], NT, preferred_element_type=F32) + bn_ref[...]
        m_o = m_scr[...]
        m_n = jnp.maximum(m_o, jnp.max(scn, axis=-1, keepdims=True))
        al = jnp.exp(m_o - m_n)
        pn = jnp.exp(scn - m_n[:, 0:DEC_PAD])
        l_n = al * l_scr[...] + jnp.sum(pn, axis=-1, keepdims=True)
        pvn = jnp.dot(pn, vn_ref[...], preferred_element_type=F32)
        lam = _diff_lambda(lw_ref, lam_init)
        for h in range(H):
            sl = slice(h * LANES, (h + 1) * LANES)
            r0 = slice(h * rows_h, h * rows_h + DEC_PAD)
            r1 = slice(h * rows_h + DEC_PAD, (h + 1) * rows_h)
            a0 = (al[r0] * acc_scr[r0, :] + pvn[r0, sl]) / l_n[r0]
            a1 = (al[r1] * acc_scr[r1, :] + pvn[r1, sl]) / l_n[r1]
            o_ref[:, sl] = _rms(a0 - lam * a1) * ng_ref[:, sl] * (1.0 - lam_init)


def diff_attn_sample(u, cache_k, cache_v, page_table, lam_w, norm_g, rel_bias, lam_init):
    nb = DEC_BATCH
    nr = 2 * H * DEC_PAD
    q = u[:, :D].reshape(nb, DEC_PAD, D) * 0.125
    grp_row = jnp.arange(2 * H)[:, None, None]
    grp_col = (jnp.arange(D) // 64)[None, None, :]
    qbig = jnp.where(grp_row == grp_col, q[:, None, :, :], 0.0).reshape(nb, nr, D)
    q4 = jnp.transpose(q.reshape(nb, DEC_PAD, H, 1, LANES), (0, 2, 3, 1, 4))
    comp = (jnp.arange(LANES) // 64)[None, None, None, None, :]
    qm = jnp.where(comp == jnp.arange(2)[None, None, :, None, None], q4, 0.0).reshape(nb, nr, LANES)
    qi = jnp.arange(DEC_PAD)
    rel_last = PAGE + qi[:, None] - jnp.arange(PAGE)[None, :]
    b_last = jnp.broadcast_to(_bias_table(rel_bias, rel_last)[:, None], (H, 2, DEC_PAD, PAGE)).reshape(nr, PAGE)
    b_last = jnp.pad(b_last, ((0, 0), ((PAGES_PER_STEP - 1) * PAGE, 0)))
    rel_new = qi[:, None] - qi[None, :]
    vis = (rel_new >= 0) & (qi[None, :] < DEC_SEQ)
    b_new = jnp.where(vis[None], _bias_table(rel_bias, rel_new), NEG)
    b_new = jnp.broadcast_to(b_new[:, None], (H, 2, DEC_PAD, DEC_PAD)).reshape(nr, DEC_PAD)
    kern = functools.partial(_dattn_s_kernel, lam_init=lam_init)
    def page_spec(slot):
        return pl.BlockSpec((None, PAGE * H, LANES), lambda b, p, pt: (pt[b, p * PAGES_PER_STEP + slot], 0, 0))

    pages = [page_spec(slot) for slot in range(PAGES_PER_STEP)]
    grid_spec = pltpu.PrefetchScalarGridSpec(
        num_scalar_prefetch=1,
        grid=(nb, N_PAGES // PAGES_PER_STEP),
        in_specs=[pl.BlockSpec((None, nr, LANES), lambda b, p, pt: (b, 0, 0)),
                  pl.BlockSpec((None, nr, D), lambda b, p, pt: (b, 0, 0))] + pages + pages + [
                  pl.BlockSpec((DEC_PAD, D), lambda b, p, pt: (b, 1)),
                  pl.BlockSpec((DEC_PAD, D), lambda b, p, pt: (b, 2)),
                  pl.BlockSpec((nr, PAGES_PER_STEP * PAGE), lambda b, p, pt: (0, 0)),
                  pl.BlockSpec((nr, DEC_PAD), lambda b, p, pt: (0, 0)),
                  pl.BlockSpec((4, 64), lambda b, p, pt: (0, 0)),
                  pl.BlockSpec((1, D), lambda b, p, pt: (0, 0))],
        out_specs=pl.BlockSpec((DEC_PAD, D), lambda b, p, pt: (b, 0)),
        scratch_shapes=[pltpu.VMEM((nr, LANES), F32), pltpu.VMEM((nr, LANES), F32), pltpu.VMEM((nr, LANES), F32)],
    )
    return pl.pallas_call(
        kern,
        grid_spec=grid_spec,
        out_shape=jax.ShapeDtypeStruct((nb * DEC_PAD, D), F32),
        compiler_params=_cp(2),
        name="diff_attn_sample",
    )(page_table, qm, qbig, *([cache_k] * PAGES_PER_STEP), *([cache_v] * PAGES_PER_STEP), u, u, b_last, b_new,
      lam_w, norm_g.reshape(1, D))


def _router_kernel(x_ref, w_ref, o_ref):
    logits = _dot(x_ref[...], w_ref[...])
    lane = lax.broadcasted_iota(jnp.int32, logits.shape, 1)
    logits = jnp.where(lane < N_EXPERTS, logits, -jnp.inf)
    m1 = jnp.max(logits, axis=-1, keepdims=True)
    i1 = jnp.min(jnp.where(logits == m1, lane, LANES), axis=-1, keepdims=True)
    rest = jnp.where(lane == i1, -jnp.inf, logits)
    m2 = jnp.max(rest, axis=-1, keepdims=True)
    i2 = jnp.min(jnp.where(rest == m2, lane, LANES), axis=-1, keepdims=True)
    e2 = jnp.exp(m2 - m1)
    g1 = 1.0 / (1.0 + e2)
    g2 = e2 / (1.0 + e2)
    o_ref[...] = jnp.where(lane == 0, i1.astype(F32),
                           jnp.where(lane == 1, i2.astype(F32),
                                     jnp.where(lane == 2, g1, jnp.where(lane == 3, g2, 0.0))))


def router(x, w_router, tm):
    m = x.shape[0]
    w = jnp.pad(w_router, ((0, 0), (0, LANES - N_EXPERTS)))
    return pl.pallas_call(
        _router_kernel,
        grid=(m // tm,),
        in_specs=[pl.BlockSpec((tm, D), lambda i: (i, 0)),
                  pl.BlockSpec((D, LANES), lambda i: (0, 0))],
        out_specs=pl.BlockSpec((tm, LANES), lambda i: (i, 0)),
        out_shape=jax.ShapeDtypeStruct((m, LANES), F32),
        compiler_params=_cp(1),
        name="router",
    )(x, w)


def _expert_kernel(be_ref, src_ref, x_hbm, wgu_ref, wd_ref, o_ref, xbuf, sem, *, tf, tm):
    del be_ref
    i = pl.program_id(0)
    nb = pl.num_programs(0)
    slot = i % 2
    nslot = 1 - slot
    nxt = jnp.minimum(i + 1, nb - 1)

    def row_copy(block, r, sl):
        return pltpu.make_async_copy(x_hbm.at[pl.ds(src_ref[block * tm + r], 1), :],
                                     xbuf.at[sl, pl.ds(r, 1), :], sem.at[sl])

    def wait_rows(sl):
        def drain(r, carry):
            row_copy(0, r, sl).wait()
            return carry
        lax.fori_loop(0, tm, drain, 0, unroll=8)

    @pl.when(i == 0)
    def _():
        def issue(r, carry):
            row_copy(0, r, 0).start()
            return carry
        lax.fori_loop(0, tm, issue, 0, unroll=8)

    wait_rows(slot)
    xb = xbuf[slot].astype(BF16)
    chunks = list(range(0, D_FF_EXPERT, tf))
    per = -(-tm // len(chunks))
    for ci, f0 in enumerate(chunks):
        gate = jnp.dot(xb, wgu_ref[:, f0:f0 + tf], preferred_element_type=F32)
        up = jnp.dot(xb, wgu_ref[:, D_FF_EXPERT + f0:D_FF_EXPERT + f0 + tf], preferred_element_type=F32)
        part = jnp.dot((_silu(gate) * up).astype(BF16), wd_ref[f0:f0 + tf, :], preferred_element_type=F32)
        if ci == 0:
            o_ref[...] = part
        else:
            o_ref[...] += part
        for r in range(ci * per, min((ci + 1) * per, tm)):
            row_copy(nxt, r, nslot).start()

    @pl.when(i == nb - 1)
    def _():
        wait_rows(nslot)


def expert_ffn(x_z, src, w_gu, w_down, layer, blk_e, tm, tf=512):
    n = src.shape[0]
    grid_spec = pltpu.PrefetchScalarGridSpec(
        num_scalar_prefetch=2,
        grid=(n // tm,),
        in_specs=[pl.BlockSpec(memory_space=pl.ANY),
                  pl.BlockSpec((None, None, D, 2 * D_FF_EXPERT), lambda i, be, sr: (layer, be[i], 0, 0),
                               pipeline_mode=pl.Buffered(1)),
                  pl.BlockSpec((None, None, D_FF_EXPERT, D), lambda i, be, sr: (layer, be[i], 0, 0),
                               pipeline_mode=pl.Buffered(1))],
        out_specs=pl.BlockSpec((tm, D), lambda i, be, sr: (i, 0)),
        scratch_shapes=[pltpu.VMEM((2, tm, D), F32), pltpu.SemaphoreType.DMA((2,))],
    )
    return pl.pallas_call(
        functools.partial(_expert_kernel, tf=tf, tm=tm),
        grid_spec=grid_spec,
        out_shape=jax.ShapeDtypeStruct((n, D), F32),
        compiler_params=_cp(1),
        name="expert_ffn",
    )(blk_e, src, x_z, w_gu, w_down)


def _combine_kernel(d0_ref, d1_ref, yb_hbm, x_ref, rt_ref, g_ref, b_ref, o_ref, buf0, buf1, sem, *, G):
    base = pl.program_id(0) * G

    def issue(r, carry):
        pltpu.make_async_copy(yb_hbm.at[pl.ds(d0_ref[base + r], 1), :], buf0.at[pl.ds(r, 1), :], sem).start()
        pltpu.make_async_copy(yb_hbm.at[pl.ds(d1_ref[base + r], 1), :], buf1.at[pl.ds(r, 1), :], sem).start()
        return carry

    lax.fori_loop(0, G, issue, 0, unroll=4)

    def drain(r, carry):
        pltpu.make_async_copy(yb_hbm.at[pl.ds(0, 1), :], buf0.at[pl.ds(r, 1), :], sem).wait()
        pltpu.make_async_copy(yb_hbm.at[pl.ds(0, 1), :], buf1.at[pl.ds(r, 1), :], sem).wait()
        return carry

    lax.fori_loop(0, G, drain, 0, unroll=4)
    rt = rt_ref[...]
    y = rt[:, 2:3] * buf0[...] + rt[:, 3:4] * buf1[...]
    o_ref[...] = _layer_norm(ALPHA * x_ref[...] + y, g_ref[...], b_ref[...])


def combine_ln(yb, x, route, d0, d1, g, b, G):
    m = x.shape[0]
    grid_spec = pltpu.PrefetchScalarGridSpec(
        num_scalar_prefetch=2,
        grid=(m // G,),
        in_specs=[pl.BlockSpec(memory_space=pl.ANY),
                  pl.BlockSpec((G, D), lambda i, a, c: (i, 0)),
                  pl.BlockSpec((G, LANES), lambda i, a, c: (i, 0)),
                  pl.BlockSpec((1, D), lambda i, a, c: (0, 0)),
                  pl.BlockSpec((1, D), lambda i, a, c: (0, 0))],
        out_specs=pl.BlockSpec((G, D), lambda i, a, c: (i, 0)),
        scratch_shapes=[pltpu.VMEM((G, D), F32), pltpu.VMEM((G, D), F32), pltpu.SemaphoreType.DMA(())],
    )
    return pl.pallas_call(
        functools.partial(_combine_kernel, G=G),
        grid_spec=grid_spec,
        out_shape=jax.ShapeDtypeStruct((m, D), F32),
        compiler_params=_cp(1),
        name="combine_ln",
    )(d0, d1, yb, x, route, g.reshape(1, D), b.reshape(1, D))


def moe_ln(x, w_router, w_gu, w_down, layer, g, b, tm=256):
    m = x.shape[0]
    route = router(x, w_router, 640)
    flat_e = route[:, :2].astype(jnp.int32).reshape(-1)
    onehot = (flat_e[:, None] == jnp.arange(N_EXPERTS)[None, :]).astype(jnp.int32)
    csum = jnp.cumsum(onehot, axis=0)
    counts = csum[-1]
    padded = (counts + tm - 1) // tm * tm
    p_end = jnp.cumsum(padded)
    dest = jnp.sum(onehot * ((p_end - padded)[None, :] + csum - 1), axis=1)
    n_blocks = -(-(2 * m + N_EXPERTS * (tm - 1)) // tm)
    n_rows = n_blocks * tm
    n_used = (p_end[-1] // tm).astype(jnp.int32).reshape(1)
    blk = jnp.minimum(jnp.arange(n_blocks), n_used[0] - 1) * tm
    blk_e = jnp.minimum(jnp.sum((blk[:, None] >= p_end[None, :]).astype(jnp.int32), axis=1), N_EXPERTS - 1)
    x_z = jnp.concatenate([x, jnp.zeros((8, D), x.dtype)], axis=0)
    src = jnp.full((n_rows,), m, jnp.int32).at[dest].set(jnp.arange(2 * m, dtype=jnp.int32) // 2)
    yb = expert_ffn(x_z, src, w_gu, w_down, layer, blk_e, tm)
    d = dest.reshape(m, 2).astype(jnp.int32)
    return combine_ln(yb, x, route, d[:, 0], d[:, 1], g, b, 640)


def kernel(x_prompt, x_sample, mem_prompt, page_table, cache_diff_k, cache_diff_v, cache_mem_k, cache_mem_v,
           state_mlstm_C, state_mlstm_n, state_mlstm_m, state_gdn_S, state_gdn_conv, rel_bias,
           w_in_a, b_gate_a, norm_a, w_in_b, lambda_b, norm_b, w_in_c, conv_c, a_log_c, dt_bias_c, norm_c,
           w_mem_kv, w_out, ln_g, ln_b, w_ffn_gu, w_ffn_down, w_router, w_exp_gu, w_exp_down):
    mp = BATCH * SEQ
    ms = DEC_BATCH * DEC_PAD
    x_p = x_prompt.reshape(mp, D)
    x_s = jnp.pad(x_sample, ((0, 0), (0, DEC_PAD - DEC_SEQ), (0, 0))).reshape(ms, D)
    mem2 = mem_prompt.reshape(BATCH * MEM_TOKENS, D)
    n_phys = cache_diff_k.shape[1]
    w_exp_gu_b = w_exp_gu.astype(BF16)
    w_exp_down_b = w_exp_down.astype(BF16)
    cmk = cache_mem_k.reshape(DEPTH * DEC_BATCH, MEM_TOKENS * MEM_HEADS, LANES)
    cmv = cache_mem_v.reshape(DEPTH * DEC_BATCH, MEM_TOKENS * MEM_HEADS, LANES)

    def sample_rows(t):
        return t.reshape(DEC_BATCH, DEC_PAD, -1)[:, :DEC_SEQ]

    def pad_gates(w):
        return jnp.pad(w, ((0, 0), (0, LANES - w.shape[1])))

    mlstm_p, mlstm_s, diff_p, diff_s, gdn_p, gdn_s, mem_k_new, mem_v_new = [], [], [], [], [], [], [], []
    for i in range(DEPTH):
        kind, j = i % 3, i // 3
        mem_kv = matmul(mem2, w_mem_kv[i], MEM_TOKENS, 512).reshape(BATCH, MEM_TOKENS, 2 * MEM_Q)
        mem_k_new.append(mem_kv[:, :, :MEM_Q].reshape(BATCH, MEM_TOKENS, MEM_HEADS, 128))
        mem_v_new.append(mem_kv[:, :, MEM_Q:].reshape(BATCH, MEM_TOKENS, MEM_HEADS, 128))
        if kind == 0:
            w = w_in_a[j]
            w_main = jnp.concatenate([w[:, :3 * D], w[:, 3 * D + 2 * H:]], axis=1)
            w_g = pad_gates(w[:, 3 * D:3 * D + 2 * H])
            u_p, u_s = matmul(x_p, w_main, 512, 512), matmul(x_s, w_main, ms, 512)
            g_p, g_s = matmul(x_p, w_g, 512, LANES), matmul(x_s, w_g, ms, LANES)
            zc = jnp.zeros((BATCH, H, 128, 64), F32)
            zn = jnp.zeros((BATCH, H, 64), F32)
            zm = jnp.zeros((BATCH, H), F32)
            mix_p, c_p, n_p, m_p = mlstm(u_p, g_p, b_gate_a[j], norm_a[j], zc, zn, zm,
                                         BATCH, SEQ, CHUNK, 256, CHUNK)
            mix_s, c_s, n_s, m_s = mlstm(u_s, g_s, b_gate_a[j], norm_a[j], state_mlstm_C[j], state_mlstm_n[j],
                                         state_mlstm_m[j], DEC_BATCH, DEC_PAD, DEC_PAD, DEC_PAD, DEC_SEQ)
            mlstm_p.append((c_p, n_p, m_p.reshape(BATCH, H)))
            mlstm_s.append((c_s, n_s, m_s.reshape(DEC_BATCH, H)))
            mq_col = 6
        elif kind == 1:
            lam_init = 0.8 - 0.6 * math.exp(-0.3 * i)
            w_main = w_in_b[j]
            u_p, u_s = matmul(x_p, w_main, 512, 512), matmul(x_s, w_main, ms, 512)
            mix_p = diff_attn_prompt(u_p, lambda_b[j], norm_b[j], rel_bias, lam_init)
            mix_s = diff_attn_sample(u_s, cache_diff_k[j].reshape(n_phys, PAGE * H, LANES),
                                     cache_diff_v[j].reshape(n_phys, PAGE * H, LANES), page_table,
                                     lambda_b[j], norm_b[j], rel_bias, lam_init)
            k_rows, v_rows = kv_head_rows(u_p, 256)
            diff_p.append((k_rows.reshape(BATCH, SEQ, H, 128), v_rows.reshape(BATCH, SEQ, H, 128)))
            diff_s.append((sample_rows(u_s[:, D:2 * D]).reshape(DEC_BATCH, DEC_SEQ, H, 128),
                           sample_rows(u_s[:, 2 * D:3 * D]).reshape(DEC_BATCH, DEC_SEQ, H, 128)))
            mq_col = 6
        else:
            w = w_in_c[j]
            w_main = jnp.concatenate([w[:, :4 * D], w[:, 4 * D + 2 * H:]], axis=1)
            w_g = pad_gates(w[:, 4 * D:4 * D + 2 * H])
            u_p, u_s = matmul(x_p, w_main, 512, 512), matmul(x_s, w_main, ms, 512)
            g_p, g_s = matmul(x_p, w_g, 512, LANES), matmul(x_s, w_g, ms, LANES)
            zs = jnp.zeros((BATCH, H, 128, 128), F32)
            zv = jnp.zeros((BATCH, 3, 3 * D), F32)
            mix_p, s_p = gdn(u_p, g_p, conv_c[j], a_log_c[j], dt_bias_c[j], norm_c[j], zs, zv,
                             BATCH, SEQ, CHUNK, 256, CHUNK)
            mix_s, s_s = gdn(u_s, g_s, conv_c[j], a_log_c[j], dt_bias_c[j], norm_c[j], state_gdn_S[j],
                             state_gdn_conv[j], DEC_BATCH, DEC_PAD, DEC_PAD, DEC_PAD, DEC_SEQ)
            conv_p = u_p.reshape(BATCH, SEQ, -1)[:, SEQ - 3:][:, :, :3 * D]
            conv_s = sample_rows(u_s[:, :3 * D])[:, DEC_SEQ - 3:]
            gdn_p.append((s_p, conv_p))
            gdn_s.append((s_s, conv_s))
            mq_col = 8
        n_main = u_p.shape[1]
        mo_p = mem_attention(u_p.reshape(BATCH, SEQ, n_main), mq_col, mem_kv, 0, mem_kv, 1, 512).reshape(mp, MEM_Q)
        mo_s = mem_attention(u_s.reshape(DEC_BATCH, DEC_PAD, n_main), mq_col, cmk, 0, cmv, 0, DEC_PAD,
                             seq0=i * DEC_BATCH, head_rows=True).reshape(ms, MEM_Q)
        x_p = outproj_ln(mix_p, mo_p, x_p, w_out[i], ln_g[i, 0], ln_b[i, 0], 512)
        x_s = outproj_ln(mix_s, mo_s, x_s, w_out[i], ln_g[i, 0], ln_b[i, 0], ms)
        k_f = i // 2
        if i % 2 == 0:
            x_p = ffn_ln(x_p, w_ffn_gu[k_f], w_ffn_down[k_f], ln_g[i, 1], ln_b[i, 1], 512)
            x_s = ffn_ln(x_s, w_ffn_gu[k_f], w_ffn_down[k_f], ln_g[i, 1], ln_b[i, 1], ms)
        else:
            x_all = moe_ln(jnp.concatenate([x_p, x_s], axis=0), w_router[k_f], w_exp_gu_b, w_exp_down_b, k_f,
                           ln_g[i, 1], ln_b[i, 1])
            x_p, x_s = x_all[:mp], x_all[mp:]

    def stack(lst, k):
        return jnp.stack([t[k] for t in lst])

    return (x_p.reshape(BATCH, SEQ, D), sample_rows(x_s),
            stack(mlstm_p, 0), stack(mlstm_p, 1), stack(mlstm_p, 2),
            stack(mlstm_s, 0), stack(mlstm_s, 1), stack(mlstm_s, 2),
            stack(diff_p, 0), stack(diff_p, 1), stack(diff_s, 0), stack(diff_s, 1),
            stack(gdn_p, 0), stack(gdn_p, 1), stack(gdn_s, 0), stack(gdn_s, 1),
            jnp.stack(mem_k_new), jnp.stack(mem_v_new))
```

```python
import functools
import math

import jax
import jax.numpy as jnp
from jax import lax
from jax.experimental import pallas as pl
from jax.experimental.pallas import tpu as pltpu

F32 = jnp.float32
BF16 = jnp.bfloat16
HI = lax.Precision.HIGHEST
NT = (((1,), (1,)), ((), ()))
TN = (((0,), (0,)), ((), ()))

D = 1024
H = 8
DEPTH = 4
SEQ = 4096
BATCH = 4
DEC_BATCH = 32
DEC_SEQ = 4
DEC_PAD = 8
PAST_LEN = 8192
PAGE = 128
N_PAGES = PAST_LEN // PAGE
PAGES_PER_STEP = 4
ROW_TILE = 640
MEM_TOKENS = 256
MEM_Q = 512
MEM_HEADS = 4
CHUNK = 64
N_EXPERTS = 8
D_FF = 2752
D_FF_PAD = 2816
D_FF_EXPERT = 3584
EPS = 1e-6
ALPHA = (2 * DEPTH) ** 0.25
NEG = -1e30
LANES = 128
VMEM_LIMIT = 56 * 1024 * 1024


def _cp(n_axes, vmem=VMEM_LIMIT):
    return pltpu.CompilerParams(dimension_semantics=("arbitrary",) * n_axes, vmem_limit_bytes=vmem)


def _dot(a, b):
    return jnp.dot(a.astype(BF16), b.astype(BF16), preferred_element_type=F32)


def _dot_nt(a, b):
    return lax.dot_general(a.astype(BF16), b.astype(BF16), NT, preferred_element_type=F32)


def _dot_tn(a, b):
    return lax.dot_general(a.astype(BF16), b.astype(BF16), TN, preferred_element_type=F32)


def _dot_hi(a, b):
    return jnp.dot(a, b, preferred_element_type=F32, precision=HI)


def _split_bf16(a):
    hi = a.astype(BF16)
    return hi, (a - hi.astype(F32)).astype(BF16)


def _dot_split(a_hi, a_lo, b_hi, b_lo):
    return (jnp.dot(a_hi, b_hi, preferred_element_type=F32)
            + (jnp.dot(a_lo, b_hi, preferred_element_type=F32) + jnp.dot(a_hi, b_lo, preferred_element_type=F32)))


def _sigmoid(x):
    return 1.0 / (1.0 + jnp.exp(-x))


def _silu(x):
    return x * _sigmoid(x)


def _softplus(x):
    return jnp.maximum(x, 0.0) + jnp.log1p(jnp.exp(-jnp.abs(x)))


def _log_sigmoid(x):
    return -_softplus(-x)


def _layer_norm(v, g, b):
    mu = jnp.mean(v, axis=-1, keepdims=True)
    d = v - mu
    var = jnp.mean(d * d, axis=-1, keepdims=True)
    return d * lax.rsqrt(var + EPS) * g + b


def _rms(h):
    return h * lax.rsqrt(jnp.mean(h * h, axis=-1, keepdims=True) + EPS)


def _mm_kernel(x_ref, w_ref, o_ref, *, tn):
    xb = x_ref[...].astype(BF16)
    for n0 in range(0, o_ref.shape[1], tn):
        o_ref[:, n0:n0 + tn] = jnp.dot(xb, w_ref[:, n0:n0 + tn], preferred_element_type=F32)


def matmul(x, w, tm, tn):
    m, k = x.shape
    n = w.shape[1]
    return pl.pallas_call(
        functools.partial(_mm_kernel, tn=min(tn, n)),
        grid=(m // tm,),
        in_specs=[pl.BlockSpec((tm, k), lambda i: (i, 0)),
                  pl.BlockSpec((k, n), lambda i: (0, 0), pipeline_mode=pl.Buffered(1))],
        out_specs=pl.BlockSpec((tm, n), lambda i: (i, 0)),
        out_shape=jax.ShapeDtypeStruct((m, n), F32),
        compiler_params=_cp(1),
        name="matmul",
    )(x, w.astype(BF16))


def _headrows_kernel(k_ref, v_ref, ko_ref, vo_ref):
    tm = k_ref.shape[0]
    for h in range(H):
        sl = slice(h * LANES, (h + 1) * LANES)
        ko_ref[pl.ds(h, tm, stride=H), :] = k_ref[:, sl]
        vo_ref[pl.ds(h, tm, stride=H), :] = v_ref[:, sl]


def kv_head_rows(u, m, tm):
    return pl.pallas_call(
        _headrows_kernel,
        grid=(m // tm,),
        in_specs=[pl.BlockSpec((tm, D), lambda i: (i, 1)),
                  pl.BlockSpec((tm, D), lambda i: (i, 2))],
        out_specs=[pl.BlockSpec((tm * H, LANES), lambda i: (i, 0)),
                   pl.BlockSpec((tm * H, LANES), lambda i: (i, 0))],
        out_shape=[jax.ShapeDtypeStruct((m * H, LANES), F32), jax.ShapeDtypeStruct((m * H, LANES), F32)],
        compiler_params=_cp(1),
        name="kv_head_rows",
    )(u, u)


def _outproj_kernel(mixp_ref, mop_ref, mixs_ref, mos_ref, x_ref, w1_ref, w2_ref, g_ref, b_ref, o_ref, *, n_prompt):
    is_p = pl.program_id(0) < n_prompt
    mix = jnp.where(is_p, mixp_ref[...], mixs_ref[...])
    mo = jnp.where(is_p, mop_ref[...], mos_ref[...])
    y = _dot(mix, w1_ref[...]) + _dot(mo, w2_ref[...])
    o_ref[...] = _layer_norm(ALPHA * x_ref[...] + y, g_ref[...], b_ref[...])


def outproj_ln(mix_p, mo_p, mix_s, mo_s, x, w_out, g, b):
    m = x.shape[0]
    tm = mix_s.shape[0]
    n_prompt = mix_p.shape[0] // tm
    w1 = w_out[:D].astype(BF16)
    w2 = w_out[D:].astype(BF16)
    return pl.pallas_call(
        functools.partial(_outproj_kernel, n_prompt=n_prompt),
        grid=(m // tm,),
        in_specs=[pl.BlockSpec((tm, D), lambda i: (jnp.minimum(i, n_prompt - 1), 0)),
                  pl.BlockSpec((tm, MEM_Q), lambda i: (jnp.minimum(i, n_prompt - 1), 0)),
                  pl.BlockSpec((tm, D), lambda i: (0, 0)),
                  pl.BlockSpec((tm, MEM_Q), lambda i: (0, 0)),
                  pl.BlockSpec((tm, D), lambda i: (i, 0)),
                  pl.BlockSpec((D, D), lambda i: (0, 0)),
                  pl.BlockSpec((MEM_Q, D), lambda i: (0, 0)),
                  pl.BlockSpec((1, D), lambda i: (0, 0)),
                  pl.BlockSpec((1, D), lambda i: (0, 0))],
        out_specs=pl.BlockSpec((tm, D), lambda i: (i, 0)),
        out_shape=jax.ShapeDtypeStruct((m, D), F32),
        compiler_params=_cp(1),
        name="outproj_ln",
    )(mix_p, mo_p, mix_s, mo_s, x, w1, w2, g.reshape(1, D), b.reshape(1, D))


def _ffn_kernel(x_ref, wgu_ref, wd_ref, g_ref, b_ref, o_ref, *, tf):
    x = x_ref[...]
    xb = x.astype(BF16)
    for f0 in range(0, D_FF_PAD, tf):
        gate = jnp.dot(xb, wgu_ref[:, f0:f0 + tf], preferred_element_type=F32)
        up = jnp.dot(xb, wgu_ref[:, D_FF_PAD + f0:D_FF_PAD + f0 + tf], preferred_element_type=F32)
        part = jnp.dot((_silu(gate) * up).astype(BF16), wd_ref[f0:f0 + tf, :], preferred_element_type=F32)
        if f0 == 0:
            o_ref[...] = part
        else:
            o_ref[...] += part
    o_ref[...] = _layer_norm(ALPHA * x + o_ref[...], g_ref[...], b_ref[...])


def ffn_ln(x, w_gu, w_down, g, b, tm, tf=256):
    m = x.shape[0]
    pad = D_FF_PAD - D_FF
    wgu = jnp.concatenate([jnp.pad(w_gu[:, :D_FF], ((0, 0), (0, pad))),
                           jnp.pad(w_gu[:, D_FF:], ((0, 0), (0, pad)))], axis=1).astype(BF16)
    wd = jnp.pad(w_down, ((0, pad), (0, 0))).astype(BF16)
    return pl.pallas_call(
        functools.partial(_ffn_kernel, tf=tf),
        grid=(m // tm,),
        in_specs=[pl.BlockSpec((tm, D), lambda i: (i, 0)),
                  pl.BlockSpec((D, 2 * D_FF_PAD), lambda i: (0, 0), pipeline_mode=pl.Buffered(1)),
                  pl.BlockSpec((D_FF_PAD, D), lambda i: (0, 0), pipeline_mode=pl.Buffered(1)),
                  pl.BlockSpec((1, D), lambda i: (0, 0)),
                  pl.BlockSpec((1, D), lambda i: (0, 0))],
        out_specs=pl.BlockSpec((tm, D), lambda i: (i, 0)),
        out_shape=jax.ShapeDtypeStruct((m, D), F32),
        compiler_params=_cp(1),
        name="ffn_ln",
    )(x, wgu, wd, g.reshape(1, D), b.reshape(1, D))


def _memattn_kernel(q_ref, k_ref, v_ref, o_ref, *, head_rows):
    q = q_ref[...]
    for h in range(MEM_HEADS):
        sl = slice(h * LANES, (h + 1) * LANES)
        if head_rows:
            k = k_ref[pl.ds(h, MEM_TOKENS, stride=MEM_HEADS), :]
            v = v_ref[pl.ds(h, MEM_TOKENS, stride=MEM_HEADS), :]
        else:
            k = k_ref[:, sl]
            v = v_ref[:, sl]
        s = _dot_nt(q[:, sl], k) * LANES ** -0.5
        s = s - jnp.max(s, axis=-1, keepdims=True)
        p = jnp.exp(s)
        p = p / jnp.sum(p, axis=-1, keepdims=True)
        o_ref[:, sl] = _dot(p, v)


def mem_attention(u, q_col, mem_k, k_col, mem_v, v_col, tq, nseq, seqlen, row0=0, seq0=0, head_rows=False):
    nq = seqlen // tq
    off = row0 // tq
    if head_rows:
        kv_block = (None, MEM_TOKENS * MEM_HEADS, LANES)
    else:
        kv_block = (None, MEM_TOKENS, MEM_Q)
    kern = functools.partial(_memattn_kernel, head_rows=head_rows)
    in_specs = [pl.BlockSpec((tq, MEM_Q), lambda bi, i: (off + bi * nq + i, q_col)),
                pl.BlockSpec(kv_block, lambda bi, i: (bi + seq0, 0, k_col)),
                pl.BlockSpec(kv_block, lambda bi, i: (bi + seq0, 0, v_col))]
    return pl.pallas_call(
        kern,
        grid=(nseq, nq),
        in_specs=in_specs,
        out_specs=pl.BlockSpec((tq, MEM_Q), lambda bi, i: (bi * nq + i, 0)),
        out_shape=jax.ShapeDtypeStruct((nseq * seqlen, MEM_Q), F32),
        compiler_params=_cp(2),
        name="mem_attention",
    )(u, mem_k, mem_v)


def _mlstm_kernel(qk_ref, v_ref, og_ref, gc_ref, gt_ref, bgc_ref, bgr_ref, ng_ref, c0_ref, n0_ref, m0_ref,
                  mix_ref, co_ref, no_ref, mo_ref, c_scr, n_scr, m_scr, *, T, nc, n_valid, sps):
    s = pl.program_id(0)

    @pl.when(s % sps == 0)
    def _():
        c_scr[...] = c0_ref[...]
        n_scr[...] = n0_ref[...]
        m_scr[...] = m0_ref[...]

    row = lax.broadcasted_iota(jnp.int32, (T, T), 0)
    col = lax.broadcasted_iota(jnp.int32, (T, T), 1)
    causal = col <= row
    tril = causal.astype(F32)
    triu = (row <= col).astype(F32)

    cs = range(nc)
    hs = range(H)
    ch = [(c, h) for c in cs for h in hs]
    rows = [slice(c * T, (c + 1) * T) for c in cs]
    g, bcol, brow, igr_all = [], [], [], []
    for c in cs:
        g_c = gc_ref[rows[c], :] + bgc_ref[...]
        lfc = _log_sigmoid(g_c)
        gt = gt_ref[c] + bgr_ref[...]
        igr = gt[0:H, :]
        lfr = _log_sigmoid(gt[H:2 * H, :])
        if n_valid < T:
            rid = lax.broadcasted_iota(jnp.int32, (T, LANES), 0)
            g_c = jnp.where(rid < n_valid, g_c, NEG)
            lfc = jnp.where(rid < n_valid, lfc, 0.0)
            cid = lax.broadcasted_iota(jnp.int32, (H, T), 1)
            igr = jnp.where(cid < n_valid, igr, NEG)
            lfr = jnp.where(cid < n_valid, lfr, 0.0)
        g.append(g_c)
        igr_all.append(igr)
        bcol.append(_dot_hi(tril, lfc))
        brow.append(_dot_hi(lfr, triu))
    bc = {(c, h): bcol[c][:, H + h:H + h + 1] for c, h in ch}
    log_d = {(c, h): jnp.where(causal, bc[c, h] - brow[c][h:h + 1, :] + igr_all[c][h:h + 1, :], -jnp.inf)
             for c, h in ch}
    rmax = {(c, h): jnp.max(log_d[c, h], axis=-1, keepdims=True) for c, h in ch}
    m_all = m_scr[...]
    m_prev = [m_all[:, h:h + 1] for h in hs]
    inter, m_t = {}, {}
    for c, h in ch:
        inter[c, h] = bc[c, h] + m_prev[h]
        m_t[c, h] = jnp.maximum(inter[c, h], rmax[c, h])
        m_prev[h] = m_t[c, h][T - 1:T, :]
    qk = [qk_ref[rows[c], :] for c in cs]
    vv = [v_ref[rows[c], :] for c in cs]
    q = {(c, h): (qk[c][:, h * 64:(h + 1) * 64] * 0.125).astype(BF16) for c, h in ch}
    k = {(c, h): qk[c][:, 512 + h * 64:512 + (h + 1) * 64] for c, h in ch}
    kb = {(c, h): k[c, h].astype(BF16) for c, h in ch}
    v = {(c, h): vv[c][:, h * LANES:(h + 1) * LANES] for c, h in ch}
    qkt = {(c, h): lax.dot_general(q[c, h], kb[c, h], NT, preferred_element_type=F32) for c, h in ch}
    w_inter = {(c, h): jnp.exp(inter[c, h] - m_t[c, h]) for c, h in ch}
    sc = {(c, h): qkt[c, h] * jnp.exp(log_d[c, h] - m_t[c, h]) for c, h in ch}
    sv = {(c, h): jnp.dot(sc[c, h].astype(BF16), v[c, h].astype(BF16), preferred_element_type=F32) for c, h in ch}
    den0 = {(c, h): jnp.sum(sc[c, h], axis=-1, keepdims=True) for c, h in ch}
    w_end = {(c, h): jnp.exp(bc[c, h][T - 1:T, :] - bc[c, h] + g[c][:, h:h + 1] - m_t[c, h][T - 1:T, :])
             for c, h in ch}
    vk = {(c, h): lax.dot_general((w_end[c, h] * v[c, h]).astype(BF16), kb[c, h], TN, preferred_element_type=F32)
          for c, h in ch}
    kw = {(c, h): jnp.sum(w_end[c, h] * k[c, h], axis=0, keepdims=True) for c, h in ch}
    c_in = {(0, h): c_scr[h] for h in hs}
    n_in = {(0, h): n_scr[h:h + 1, :] for h in hs}
    for c, h in ch:
        decay = w_inter[c, h][T - 1:T, :]
        c_in[c + 1, h] = decay * c_in[c, h] + vk[c, h]
        n_in[c + 1, h] = decay * n_in[c, h] + kw[c, h]
    qc = {(c, h): lax.dot_general(q[c, h], c_in[c, h].astype(BF16), NT, preferred_element_type=F32) for c, h in ch}
    for c, h in ch:
        num = sv[c, h] + w_inter[c, h] * qc[c, h]
        n_b = n_in[c, h].astype(BF16).astype(F32)
        den = den0[c, h] + w_inter[c, h] * jnp.sum(q[c, h].astype(F32) * n_b, axis=-1, keepdims=True)
        hh = num / jnp.maximum(jnp.abs(den), jnp.exp(-m_t[c, h]))
        sl = slice(h * LANES, (h + 1) * LANES)
        mix_ref[rows[c], sl] = _sigmoid(og_ref[rows[c], sl]) * (_rms(hh) * ng_ref[:, sl])
    for h in hs:
        c_scr[h] = c_in[nc, h]
        n_scr[h:h + 1, :] = n_in[nc, h]
        m_scr[:, h:h + 1] = m_prev[h]

    @pl.when(s % sps == sps - 1)
    def _():
        co_ref[...] = c_scr[...]
        no_ref[...] = n_scr[...]
        mo_ref[...] = m_scr[...]


def mlstm(u, gates, b_gate, norm_g, c0, n0, m0, nseq, seqlen, T, rows_per_step, n_valid, row0=0):
    m = nseq * seqlen
    nc = rows_per_step // T
    sps = seqlen // rows_per_step
    gt = jnp.transpose(gates[row0:row0 + m, :2 * H].reshape(m // T, T, 2 * H), (0, 2, 1))
    bgc = jnp.zeros((1, LANES), F32).at[0, :2 * H].set(b_gate.reshape(-1))
    bgr = b_gate.reshape(2 * H, 1)
    r = rows_per_step
    off = row0 // r
    kern = functools.partial(_mlstm_kernel, T=T, nc=nc, n_valid=n_valid, sps=sps)
    in_specs = [pl.BlockSpec((r, D), lambda s: (s + off, 0)),
                pl.BlockSpec((r, D), lambda s: (s + off, 1)),
                pl.BlockSpec((r, D), lambda s: (s + off, 2)),
                pl.BlockSpec((r, LANES), lambda s: (s + off, 0)),
                pl.BlockSpec((nc, 2 * H, T), lambda s: (s, 0, 0)),
                pl.BlockSpec((1, LANES), lambda s: (0, 0)),
                pl.BlockSpec((2 * H, 1), lambda s: (0, 0)),
                pl.BlockSpec((1, D), lambda s: (0, 0)),
                pl.BlockSpec((None, H, 128, 64), lambda s: (s // sps, 0, 0, 0)),
                pl.BlockSpec((None, H, 64), lambda s: (s // sps, 0, 0)),
                pl.BlockSpec((None, 1, H), lambda s: (s // sps, 0, 0))]
    args = [u, u, u, gates, gt, bgc, bgr, norm_g.reshape(1, D), c0, n0, m0.reshape(nseq, 1, H)]
    return pl.pallas_call(
        kern,
        grid=(m // r,),
        in_specs=in_specs,
        out_specs=[pl.BlockSpec((r, D), lambda s: (s, 0)),
                   pl.BlockSpec((None, H, 128, 64), lambda s: (s // sps, 0, 0, 0)),
                   pl.BlockSpec((None, H, 64), lambda s: (s // sps, 0, 0)),
                   pl.BlockSpec((None, 1, H), lambda s: (s // sps, 0, 0))],
        out_shape=[jax.ShapeDtypeStruct((m, D), F32),
                   jax.ShapeDtypeStruct((nseq, H, 128, 64), F32),
                   jax.ShapeDtypeStruct((nseq, H, 64), F32),
                   jax.ShapeDtypeStruct((nseq, 1, H), F32)],
        scratch_shapes=[pltpu.VMEM((H, 128, 64), F32), pltpu.VMEM((H, 64), F32), pltpu.VMEM((1, H), F32)],
        compiler_params=_cp(1),
        name="mlstm",
    )(*args)


def _gdn_kernel(x_ref, z_ref, gc_ref, gt_ref, cw_ref, pc_ref, pr_ref, ng_ref, s0_ref, cv0_ref,
                mix_ref, so_ref, xbuf, cv_scr, s_scr, *, T, nc, n_valid, sps):
    s = pl.program_id(0)
    R = T * nc

    @pl.when(s % sps == 0)
    def _():
        s_scr[...] = s0_ref[...]
        xbuf[5:8, :] = cv0_ref[...]

    xbuf[8:8 + R, :] = x_ref[...]
    conv = (cw_ref[0:1, :] * xbuf[5:5 + R, :] + cw_ref[1:2, :] * xbuf[6:6 + R, :]
            + cw_ref[2:3, :] * xbuf[7:7 + R, :] + cw_ref[3:4, :] * xbuf[8:8 + R, :])
    cv_scr[...] = _silu(conv)
    xbuf[0:8, :] = xbuf[R:R + 8, :]

    row = lax.broadcasted_iota(jnp.int32, (T, T), 0)
    col = lax.broadcasted_iota(jnp.int32, (T, T), 1)
    causal = col <= row
    strict = col < row
    tril = causal.astype(F32)
    triu = (row <= col).astype(F32)

    def chunk(c, carry):
        r0 = pl.multiple_of(c * T, T)
        rows = pl.ds(r0, T)
        gpre = gc_ref[rows, :]
        beta_all = _sigmoid(gpre)
        g_all = -jnp.exp(pc_ref[0:1, :]) * _softplus(gpre + pc_ref[1:2, :])
        gt = gt_ref[c]
        gr_all = -jnp.exp(pr_ref[:, 0:1]) * _softplus(gt + pr_ref[:, 1:2])
        gr = gr_all[H:2 * H, :]
        if n_valid < T:
            rid = lax.broadcasted_iota(jnp.int32, (T, LANES), 0)
            beta_all = jnp.where(rid < n_valid, beta_all, 0.0)
            g_all = jnp.where(rid < n_valid, g_all, 0.0)
            cid = lax.broadcasted_iota(jnp.int32, (H, T), 1)
            gr = jnp.where(cid < n_valid, gr, 0.0)
        gamc_all = _dot_hi(tril, g_all)
        gamr_all = _dot_hi(gr, triu)
        hs = range(H)
        qf = [cv_scr[rows, h * LANES:(h + 1) * LANES] for h in hs]
        kf = [cv_scr[rows, D + h * LANES:D + (h + 1) * LANES] for h in hs]
        vf = [cv_scr[rows, 2 * D + h * LANES:2 * D + (h + 1) * LANES] for h in hs]
        qb = [(qf[h] * lax.rsqrt(jnp.sum(qf[h] * qf[h], axis=-1, keepdims=True) + EPS) * LANES ** -0.5).astype(BF16)
              for h in hs]
        kn = [kf[h] * lax.rsqrt(jnp.sum(kf[h] * kf[h], axis=-1, keepdims=True) + EPS) for h in hs]
        kb = [kn[h].astype(BF16) for h in hs]
        beta = [beta_all[:, h:h + 1] for h in hs]
        gamc = [gamc_all[:, H + h:H + h + 1] for h in hs]
        egam = [jnp.exp(gamc[h]) for h in hs]
        dec = [jnp.exp(jnp.where(causal, gamc[h] - gamr_all[h:h + 1, :], NEG)) for h in hs]
        s_old = [s_scr[h] for h in hs]
        sb = [s_old[h].astype(BF16) for h in hs]
        qkk = [lax.dot_general(jnp.concatenate([qb[h], kb[h]], axis=0), kb[h], NT, preferred_element_type=F32)
               for h in hs]
        qs = [lax.dot_general(qb[h], sb[h], NT, preferred_element_type=F32) for h in hs]
        nmat = [jnp.where(strict, beta[h] * qkk[h][T:2 * T] * dec[h], 0.0) for h in hs]
        y = [-nmat[h] for h in hs]
        npow = nmat
        nsp = [_split_bf16(nmat[h]) for h in hs]
        span = 2
        while span < T:
            if T < 16:
                npow = [_dot_hi(npow[h], npow[h]) for h in hs]
                y = [y[h] + npow[h] + _dot_hi(y[h], npow[h]) for h in hs]
            else:
                npow = [_dot_split(nsp[h][0], nsp[h][1], nsp[h][0], nsp[h][1]) for h in hs]
                nsp = [_split_bf16(npow[h]) for h in hs]
                ysp = [_split_bf16(y[h]) for h in hs]
                y = [y[h] + npow[h] + _dot_split(ysp[h][0], ysp[h][1], nsp[h][0], nsp[h][1]) for h in hs]
            span *= 2
        rhs = [jnp.concatenate([beta[h] * vf[h], (beta[h] * egam[h]) * kn[h]], axis=1) for h in hs]
        if T < 16:
            sol = [rhs[h] + _dot_hi(y[h], rhs[h]) for h in hs]
        else:
            sol = [rhs[h] + _dot(y[h], rhs[h]) for h in hs]
        delta = [sol[h][:, 0:LANES] - _dot_nt(sol[h][:, LANES:2 * LANES], sb[h]) for h in hs]
        o = [egam[h] * qs[h] + _dot(qkk[h][0:T] * dec[h], delta[h]) for h in hs]
        g_end = [gamc[h][T - 1:T, :] for h in hs]
        upd = [_dot_tn(jnp.exp(g_end[h] - gamc[h]) * delta[h], kb[h]) for h in hs]
        for h in hs:
            sl = slice(h * LANES, (h + 1) * LANES)
            s_scr[h] = jnp.exp(g_end[h]) * s_old[h] + upd[h]
            mix_ref[rows, sl] = _rms(o[h]) * ng_ref[:, sl] * _silu(z_ref[rows, sl])
        return carry

    lax.fori_loop(0, nc, chunk, 0, unroll=min(nc, 2))

    @pl.when(s % sps == sps - 1)
    def _():
        so_ref[...] = s_scr[...]


def gdn(u, gates, conv_w, a_log, dt_bias, norm_g, s0, conv0, nseq, seqlen, T, rows_per_step, n_valid, row0=0):
    m = nseq * seqlen
    nc = rows_per_step // T
    sps = seqlen // rows_per_step
    r = rows_per_step
    off = row0 // r
    gt = jnp.transpose(gates[row0:row0 + m, :2 * H].reshape(m // T, T, 2 * H), (0, 2, 1))
    pc = jnp.zeros((2, LANES), F32).at[0, H:2 * H].set(a_log).at[1, H:2 * H].set(dt_bias)
    pr = jnp.zeros((2 * H, 2), F32).at[H:, 0].set(a_log).at[H:, 1].set(dt_bias)
    kern = functools.partial(_gdn_kernel, T=T, nc=nc, n_valid=n_valid, sps=sps)
    in_specs = [pl.BlockSpec((r, 3 * D), lambda s: (s + off, 0)),
                pl.BlockSpec((r, D), lambda s: (s + off, 3)),
                pl.BlockSpec((r, LANES), lambda s: (s + off, 0)),
                pl.BlockSpec((nc, 2 * H, T), lambda s: (s, 0, 0)),
                pl.BlockSpec((4, 3 * D), lambda s: (0, 0)),
                pl.BlockSpec((2, LANES), lambda s: (0, 0)),
                pl.BlockSpec((2 * H, 2), lambda s: (0, 0)),
                pl.BlockSpec((1, D), lambda s: (0, 0)),
                pl.BlockSpec((None, H, 128, 128), lambda s: (s // sps, 0, 0, 0)),
                pl.BlockSpec((None, 3, 3 * D), lambda s: (s // sps, 0, 0))]
    args = [u, u, gates, gt, conv_w, pc, pr, norm_g.reshape(1, D), s0, conv0]
    return pl.pallas_call(
        kern,
        grid=(m // r,),
        in_specs=in_specs,
        out_specs=[pl.BlockSpec((r, D), lambda s: (s, 0)),
                   pl.BlockSpec((None, H, 128, 128), lambda s: (s // sps, 0, 0, 0))],
        out_shape=[jax.ShapeDtypeStruct((m, D), F32),
                   jax.ShapeDtypeStruct((nseq, H, 128, 128), F32)],
        scratch_shapes=[pltpu.VMEM((r + 8, 3 * D), F32), pltpu.VMEM((r, 3 * D), F32),
                        pltpu.VMEM((H, 128, 128), F32)],
        compiler_params=_cp(1),
        name="gdn",
    )(*args)


def _bucket(rel):
    n = jnp.maximum(rel, 0)
    nf = jnp.maximum(n, 1).astype(F32)
    large = 16 + (jnp.log(nf / 16) / math.log(128 / 16) * 16).astype(jnp.int32)
    return jnp.where(n < 16, n, jnp.minimum(large, 31))


def _diff_lambda(lw_ref, lam_init):
    lw = lw_ref[...]
    a = jnp.sum(lw[0:1, :] * lw[1:2, :], axis=-1, keepdims=True)
    b = jnp.sum(lw[2:3, :] * lw[3:4, :], axis=-1, keepdims=True)
    return jnp.exp(a) - jnp.exp(b) + lam_init


def _dattn_p_kernel(q_ref, k_ref, v_ref, bn_ref, lw_ref, ng_ref, o_ref, m_scr, acc_scr, *, TB, lam_init):
    i = pl.program_id(2)
    W = 2 * TB
    lane = lax.broadcasted_iota(jnp.int32, (TB, LANES), 1)
    q = q_ref[...] * 0.125
    qq = jnp.concatenate([jnp.where(lane < 64, q, 0.0), jnp.where(lane >= 64, q, 0.0)], axis=0).astype(BF16)
    m_scr[...] = jnp.full_like(m_scr, NEG)
    acc_scr[...] = jnp.zeros_like(acc_scr)

    def step(k0, width, bias):
        rows = pl.ds(pl.multiple_of(k0, TB), width)
        kb = k_ref[rows, :].astype(BF16)
        vext = jnp.concatenate([v_ref[rows, :].astype(BF16), jnp.ones((width, LANES), BF16)], axis=1)
        sc = lax.dot_general(qq, kb, NT, preferred_element_type=F32)
        if bias is not None:
            sc = sc + bias
        ng = width // LANES
        mloc = sc[:, 0:LANES]
        for g in range(1, ng):
            mloc = jnp.maximum(mloc, sc[:, g * LANES:(g + 1) * LANES])
        m_old = m_scr[...]
        m_new = jnp.maximum(m_old, jnp.max(mloc, axis=-1, keepdims=True))
        alpha = jnp.exp(m_old - m_new)
        p = jnp.exp(sc - jnp.concatenate([m_new] * ng, axis=1)).astype(BF16)
        acc_scr[...] = (jnp.concatenate([alpha, alpha], axis=1) * acc_scr[...]
                        + jnp.dot(p, vext, preferred_element_type=F32))
        m_scr[...] = m_new

    n_far = jnp.maximum(i - 1, 0)
    n_big = n_far // 4
    rem = n_far % 4

    def body(c, carry):
        step(c * 2 * W, W, None)
        step(c * 2 * W + W, W, None)
        return carry

    lax.fori_loop(0, n_big, body, 0)

    @pl.when(rem >= 2)
    def _():
        step(n_big * 2 * W, W, None)

    @pl.when(rem % 2 == 1)
    def _():
        step((n_far - 1) * TB, TB, None)

    step(n_far * TB, W, bn_ref[...])
    lam = _diff_lambda(lw_ref, lam_init)
    acc = acc_scr[...]
    o = (acc[0:TB, 0:LANES] / acc[0:TB, LANES:2 * LANES]
         - lam * (acc[TB:2 * TB, 0:LANES] / acc[TB:2 * TB, LANES:2 * LANES]))
    o_ref[...] = _rms(o) * ng_ref[...] * (1.0 - lam_init)


def _bias_table(rel_bias, rel):
    onehot = (_bucket(rel)[..., None] == jnp.arange(32)).astype(F32)
    return jnp.einsum('...k,kh->h...', onehot, rel_bias - rel_bias[31:32], precision=HI)


def diff_attn_prompt(u, lam_w, norm_g, rel_bias, lam_init, TB=256):
    nq = SEQ // TB
    ii = jnp.arange(TB)
    rel_d = ii[:, None] - ii[None, :]
    bd = jnp.where((rel_d >= 0)[None], _bias_table(rel_bias, rel_d), NEG)
    bs = _bias_table(rel_bias, rel_d + TB)
    masked = jnp.full((H, TB, TB), NEG, F32)
    bn = jnp.stack([jnp.concatenate([bd, masked], axis=2), jnp.concatenate([bs, bd], axis=2)], axis=1)
    bn = jnp.concatenate([bn, bn], axis=2)
    kern = functools.partial(_dattn_p_kernel, TB=TB, lam_init=lam_init)
    return pl.pallas_call(
        kern,
        grid=(BATCH, H, nq),
        in_specs=[pl.BlockSpec((TB, LANES), lambda b, h, i: (b * nq + i, h)),
                  pl.BlockSpec((SEQ, LANES), lambda b, h, i: (b, H + h)),
                  pl.BlockSpec((SEQ, LANES), lambda b, h, i: (b, 2 * H + h)),
                  pl.BlockSpec((None, None, 2 * TB, 2 * TB), lambda b, h, i: (h, jnp.minimum(i, 1), 0, 0)),
                  pl.BlockSpec((4, 64), lambda b, h, i: (0, 0)),
                  pl.BlockSpec((1, LANES), lambda b, h, i: (0, h))],
        out_specs=pl.BlockSpec((TB, LANES), lambda b, h, i: (b * nq + i, h)),
        out_shape=jax.ShapeDtypeStruct((BATCH * SEQ, D), F32),
        scratch_shapes=[pltpu.VMEM((2 * TB, LANES), F32), pltpu.VMEM((2 * TB, 2 * LANES), F32)],
        compiler_params=_cp(3),
        name="diff_attn_prompt",
    )(u, u, u, bn, lam_w, norm_g.reshape(1, D))


def _dattn_s_kernel(pt_ref, qm_ref, qb_ref, *refs, lam_init):
    del pt_ref
    kp_refs = refs[0:PAGES_PER_STEP]
    vp_refs = refs[PAGES_PER_STEP:2 * PAGES_PER_STEP]
    kn_ref, vn_ref, bl_ref, bn_ref, lw_ref, ng_ref, o_ref, m_scr, l_scr, acc_scr = refs[2 * PAGES_PER_STEP:]
    p = pl.program_id(1)
    last = pl.num_programs(1) - 1
    rows_h = 2 * DEC_PAD

    @pl.when(p == 0)
    def _():
        m_scr[...] = jnp.full_like(m_scr, NEG)
        l_scr[...] = jnp.zeros_like(l_scr)
        acc_scr[...] = jnp.zeros_like(acc_scr)

    def head_rows(page_refs, h):
        return jnp.concatenate([r[pl.ds(h, PAGE, stride=H), :].astype(BF16) for r in page_refs], axis=0)

    qm = qm_ref[...].astype(BF16)
    sc = jnp.concatenate(
        [lax.dot_general(qm[h * rows_h:(h + 1) * rows_h], head_rows(kp_refs, h), NT, preferred_element_type=F32)
         for h in range(H)], axis=0)
    sc = sc + jnp.where(p == last, bl_ref[...], 0.0)
    m_old = m_scr[...]
    m_new = jnp.maximum(m_old, jnp.max(sc, axis=-1, keepdims=True))
    alpha = jnp.exp(m_old - m_new)
    pr = jnp.exp(sc - jnp.concatenate([m_new] * PAGES_PER_STEP, axis=1))
    l_scr[...] = alpha * l_scr[...] + jnp.sum(pr, axis=-1, keepdims=True)
    prb = pr.astype(BF16)
    pv = jnp.concatenate(
        [jnp.dot(prb[h * rows_h:(h + 1) * rows_h], head_rows(vp_refs, h), preferred_element_type=F32)
         for h in range(H)], axis=0)
    acc_scr[...] = alpha * acc_scr[...] + pv
    m_scr[...] = m_new

    @pl.when(p == last)
    def _():
        scn = lax.dot_general(qb_ref[...], kn_ref[...], NT, preferred_element_type=F32) + bn_ref[...]
        m_o = m_scr[...]
        m_n = jnp.maximum(m_o, jnp.max(scn, axis=-1, keepdims=True))
        al = jnp.exp(m_o - m_n)
        pn = jnp.exp(scn - m_n[:, 0:DEC_PAD])
        l_n = al * l_scr[...] + jnp.sum(pn, axis=-1, keepdims=True)
        pvn = jnp.dot(pn, vn_ref[...], preferred_element_type=F32)
        lam = _diff_lambda(lw_ref, lam_init)
        for h in range(H):
            sl = slice(h * LANES, (h + 1) * LANES)
            r0 = slice(h * rows_h, h * rows_h + DEC_PAD)
            r1 = slice(h * rows_h + DEC_PAD, (h + 1) * rows_h)
            a0 = (al[r0] * acc_scr[r0, :] + pvn[r0, sl]) / l_n[r0]
            a1 = (al[r1] * acc_scr[r1, :] + pvn[r1, sl]) / l_n[r1]
            o_ref[:, sl] = _rms(a0 - lam * a1) * ng_ref[:, sl] * (1.0 - lam_init)


def diff_attn_sample(u, cache_k, cache_v, page_table, lam_w, norm_g, rel_bias, lam_init, row0):
    nb = DEC_BATCH
    nr = 2 * H * DEC_PAD
    off = row0 // DEC_PAD
    q = u[row0:row0 + nb * DEC_PAD, :D].reshape(nb, DEC_PAD, D) * 0.125
    grp_row = jnp.arange(2 * H)[:, None, None]
    grp_col = (jnp.arange(D) // 64)[None, None, :]
    qbig = jnp.where(grp_row == grp_col, q[:, None, :, :], 0.0).reshape(nb, nr, D)
    q4 = jnp.transpose(q.reshape(nb, DEC_PAD, H, 1, LANES), (0, 2, 3, 1, 4))
    comp = (jnp.arange(LANES) // 64)[None, None, None, None, :]
    qm = jnp.where(comp == jnp.arange(2)[None, None, :, None, None], q4, 0.0).reshape(nb, nr, LANES)
    qi = jnp.arange(DEC_PAD)
    rel_last = PAGE + qi[:, None] - jnp.arange(PAGE)[None, :]
    b_last = jnp.broadcast_to(_bias_table(rel_bias, rel_last)[:, None], (H, 2, DEC_PAD, PAGE)).reshape(nr, PAGE)
    b_last = jnp.pad(b_last, ((0, 0), ((PAGES_PER_STEP - 1) * PAGE, 0)))
    rel_new = qi[:, None] - qi[None, :]
    vis = (rel_new >= 0) & (qi[None, :] < DEC_SEQ)
    b_new = jnp.where(vis[None], _bias_table(rel_bias, rel_new), NEG)
    b_new = jnp.broadcast_to(b_new[:, None], (H, 2, DEC_PAD, DEC_PAD)).reshape(nr, DEC_PAD)
    kern = functools.partial(_dattn_s_kernel, lam_init=lam_init)
    def page_spec(slot):
        return pl.BlockSpec((None, PAGE * H, LANES), lambda b, p, pt: (pt[b, p * PAGES_PER_STEP + slot], 0, 0))

    pages = [page_spec(slot) for slot in range(PAGES_PER_STEP)]
    in_specs = [pl.BlockSpec((None, nr, LANES), lambda b, p, pt: (b, 0, 0)),
                pl.BlockSpec((None, nr, D), lambda b, p, pt: (b, 0, 0))] + pages + pages + [
                pl.BlockSpec((DEC_PAD, D), lambda b, p, pt: (b + off, 1)),
                pl.BlockSpec((DEC_PAD, D), lambda b, p, pt: (b + off, 2)),
                pl.BlockSpec((nr, PAGES_PER_STEP * PAGE), lambda b, p, pt: (0, 0)),
                pl.BlockSpec((nr, DEC_PAD), lambda b, p, pt: (0, 0)),
                pl.BlockSpec((4, 64), lambda b, p, pt: (0, 0)),
                pl.BlockSpec((1, D), lambda b, p, pt: (0, 0))]
    args = [qm, qbig, *([cache_k] * PAGES_PER_STEP), *([cache_v] * PAGES_PER_STEP), u, u, b_last, b_new,
            lam_w, norm_g.reshape(1, D)]
    grid_spec = pltpu.PrefetchScalarGridSpec(
        num_scalar_prefetch=1,
        grid=(nb, N_PAGES // PAGES_PER_STEP),
        in_specs=in_specs,
        out_specs=pl.BlockSpec((DEC_PAD, D), lambda b, p, pt: (b, 0)),
        scratch_shapes=[pltpu.VMEM((nr, LANES), F32), pltpu.VMEM((nr, LANES), F32), pltpu.VMEM((nr, LANES), F32)],
    )
    return pl.pallas_call(
        kern,
        grid_spec=grid_spec,
        out_shape=jax.ShapeDtypeStruct((nb * DEC_PAD, D), F32),
        compiler_params=_cp(2),
        name="diff_attn_sample",
    )(page_table, *args)


def _router_kernel(x_ref, w_ref, o_ref):
    logits = _dot(x_ref[...], w_ref[...])
    lane = lax.broadcasted_iota(jnp.int32, logits.shape, 1)
    logits = jnp.where(lane < N_EXPERTS, logits, -jnp.inf)
    m1 = jnp.max(logits, axis=-1, keepdims=True)
    i1 = jnp.min(jnp.where(logits == m1, lane, LANES), axis=-1, keepdims=True)
    rest = jnp.where(lane == i1, -jnp.inf, logits)
    m2 = jnp.max(rest, axis=-1, keepdims=True)
    i2 = jnp.min(jnp.where(rest == m2, lane, LANES), axis=-1, keepdims=True)
    e2 = jnp.exp(m2 - m1)
    g1 = 1.0 / (1.0 + e2)
    g2 = e2 / (1.0 + e2)
    o_ref[...] = jnp.where(lane == 0, i1.astype(F32),
                           jnp.where(lane == 1, i2.astype(F32),
                                     jnp.where(lane == 2, g1, jnp.where(lane == 3, g2, 0.0))))


def router(x, w_router, tm):
    m = x.shape[0]
    w = jnp.pad(w_router, ((0, 0), (0, LANES - N_EXPERTS)))
    return pl.pallas_call(
        _router_kernel,
        grid=(m // tm,),
        in_specs=[pl.BlockSpec((tm, D), lambda i: (i, 0)),
                  pl.BlockSpec((D, LANES), lambda i: (0, 0))],
        out_specs=pl.BlockSpec((tm, LANES), lambda i: (i, 0)),
        out_shape=jax.ShapeDtypeStruct((m, LANES), F32),
        compiler_params=_cp(1),
        name="router",
    )(x, w)


def _expert_kernel(be_ref, src_ref, x_hbm, wgu_ref, wd_ref, o_ref, xbuf, sem, *, tf, tm):
    del be_ref
    i = pl.program_id(0)
    nb = pl.num_programs(0)
    slot = i % 2
    nslot = 1 - slot
    nxt = jnp.minimum(i + 1, nb - 1)

    def row_copy(block, r, sl):
        return pltpu.make_async_copy(x_hbm.at[pl.ds(src_ref[block * tm + r], 1), :],
                                     xbuf.at[sl, pl.ds(r, 1), :], sem.at[sl])

    def wait_rows(sl):
        def drain(r, carry):
            row_copy(0, r, sl).wait()
            return carry
        lax.fori_loop(0, tm, drain, 0, unroll=8)

    @pl.when(i == 0)
    def _():
        def issue(r, carry):
            row_copy(0, r, 0).start()
            return carry
        lax.fori_loop(0, tm, issue, 0, unroll=8)

    wait_rows(slot)
    xb = xbuf[slot].astype(BF16)
    chunks = list(range(0, D_FF_EXPERT, tf))
    per = -(-tm // len(chunks))
    for ci, f0 in enumerate(chunks):
        gate = jnp.dot(xb, wgu_ref[:, f0:f0 + tf], preferred_element_type=F32)
        up = jnp.dot(xb, wgu_ref[:, D_FF_EXPERT + f0:D_FF_EXPERT + f0 + tf], preferred_element_type=F32)
        part = jnp.dot((_silu(gate) * up).astype(BF16), wd_ref[f0:f0 + tf, :], preferred_element_type=F32)
        if ci == 0:
            o_ref[...] = part
        else:
            o_ref[...] += part
        for r in range(ci * per, min((ci + 1) * per, tm)):
            row_copy(nxt, r, nslot).start()

    @pl.when(i == nb - 1)
    def _():
        wait_rows(nslot)


def expert_ffn(x_z, src, w_gu, w_down, layer, blk_e, tm, tf=512):
    n = src.shape[0]
    grid_spec = pltpu.PrefetchScalarGridSpec(
        num_scalar_prefetch=2,
        grid=(n // tm,),
        in_specs=[pl.BlockSpec(memory_space=pl.ANY),
                  pl.BlockSpec((None, None, D, 2 * D_FF_EXPERT), lambda i, be, sr: (layer, be[i], 0, 0),
                               pipeline_mode=pl.Buffered(1)),
                  pl.BlockSpec((None, None, D_FF_EXPERT, D), lambda i, be, sr: (layer, be[i], 0, 0),
                               pipeline_mode=pl.Buffered(1))],
        out_specs=pl.BlockSpec((tm, D), lambda i, be, sr: (i, 0)),
        scratch_shapes=[pltpu.VMEM((2, tm, D), F32), pltpu.SemaphoreType.DMA((2,))],
    )
    return pl.pallas_call(
        functools.partial(_expert_kernel, tf=tf, tm=tm),
        grid_spec=grid_spec,
        out_shape=jax.ShapeDtypeStruct((n, D), F32),
        compiler_params=_cp(1),
        name="expert_ffn",
    )(blk_e, src, x_z, w_gu, w_down)


def _combine_kernel(d0_ref, d1_ref, yb_hbm, x_ref, rt_ref, g_ref, b_ref, o_ref, buf0, buf1, sem, *, G):
    base = pl.program_id(0) * G

    def issue(r, carry):
        pltpu.make_async_copy(yb_hbm.at[pl.ds(d0_ref[base + r], 1), :], buf0.at[pl.ds(r, 1), :], sem).start()
        pltpu.make_async_copy(yb_hbm.at[pl.ds(d1_ref[base + r], 1), :], buf1.at[pl.ds(r, 1), :], sem).start()
        return carry

    lax.fori_loop(0, G, issue, 0, unroll=4)

    def drain(r, carry):
        pltpu.make_async_copy(yb_hbm.at[pl.ds(0, 1), :], buf0.at[pl.ds(r, 1), :], sem).wait()
        pltpu.make_async_copy(yb_hbm.at[pl.ds(0, 1), :], buf1.at[pl.ds(r, 1), :], sem).wait()
        return carry

    lax.fori_loop(0, G, drain, 0, unroll=4)
    rt = rt_ref[...]
    y = rt[:, 2:3] * buf0[...] + rt[:, 3:4] * buf1[...]
    o_ref[...] = _layer_norm(ALPHA * x_ref[...] + y, g_ref[...], b_ref[...])


def combine_ln(yb, x, route, d0, d1, g, b, G):
    m = x.shape[0]
    grid_spec = pltpu.PrefetchScalarGridSpec(
        num_scalar_prefetch=2,
        grid=(m // G,),
        in_specs=[pl.BlockSpec(memory_space=pl.ANY),
                  pl.BlockSpec((G, D), lambda i, a, c: (i, 0)),
                  pl.BlockSpec((G, LANES), lambda i, a, c: (i, 0)),
                  pl.BlockSpec((1, D), lambda i, a, c: (0, 0)),
                  pl.BlockSpec((1, D), lambda i, a, c: (0, 0))],
        out_specs=pl.BlockSpec((G, D), lambda i, a, c: (i, 0)),
        scratch_shapes=[pltpu.VMEM((G, D), F32), pltpu.VMEM((G, D), F32), pltpu.SemaphoreType.DMA(())],
    )
    return pl.pallas_call(
        functools.partial(_combine_kernel, G=G),
        grid_spec=grid_spec,
        out_shape=jax.ShapeDtypeStruct((m, D), F32),
        compiler_params=_cp(1),
        name="combine_ln",
    )(d0, d1, yb, x, route, g.reshape(1, D), b.reshape(1, D))


def moe_ln(x, w_router, w_gu, w_down, layer, g, b, tm=256):
    m = x.shape[0]
    route = router(x, w_router, 640)
    flat_e = route[:, :2].astype(jnp.int32).reshape(-1)
    onehot = (flat_e[:, None] == jnp.arange(N_EXPERTS)[None, :]).astype(jnp.int32)
    csum = jnp.cumsum(onehot, axis=0)
    counts = csum[-1]
    padded = (counts + tm - 1) // tm * tm
    p_end = jnp.cumsum(padded)
    dest = jnp.sum(onehot * ((p_end - padded)[None, :] + csum - 1), axis=1)
    n_blocks = -(-(2 * m + N_EXPERTS * (tm - 1)) // tm)
    n_rows = n_blocks * tm
    n_used = (p_end[-1] // tm).astype(jnp.int32).reshape(1)
    blk = jnp.minimum(jnp.arange(n_blocks), n_used[0] - 1) * tm
    blk_e = jnp.minimum(jnp.sum((blk[:, None] >= p_end[None, :]).astype(jnp.int32), axis=1), N_EXPERTS - 1)
    x_z = jnp.concatenate([x, jnp.zeros((8, D), x.dtype)], axis=0)
    src = jnp.full((n_rows,), m, jnp.int32).at[dest].set(jnp.arange(2 * m, dtype=jnp.int32) // 2)
    yb = expert_ffn(x_z, src, w_gu, w_down, layer, blk_e, tm)
    d = dest.reshape(m, 2).astype(jnp.int32)
    return combine_ln(yb, x, route, d[:, 0], d[:, 1], g, b, 640)


def kernel(x_prompt, x_sample, mem_prompt, page_table, cache_diff_k, cache_diff_v, cache_mem_k, cache_mem_v,
           state_mlstm_C, state_mlstm_n, state_mlstm_m, state_gdn_S, state_gdn_conv, rel_bias,
           w_in_a, b_gate_a, norm_a, w_in_b, lambda_b, norm_b, w_in_c, conv_c, a_log_c, dt_bias_c, norm_c,
           w_mem_kv, w_out, ln_g, ln_b, w_ffn_gu, w_ffn_down, w_router, w_exp_gu, w_exp_down):
    mp = BATCH * SEQ
    ms = DEC_BATCH * DEC_PAD
    tm = ROW_TILE
    x = jnp.concatenate([x_prompt.reshape(mp, D),
                         jnp.pad(x_sample, ((0, 0), (0, DEC_PAD - DEC_SEQ), (0, 0))).reshape(ms, D)], axis=0)
    mem2 = mem_prompt.reshape(BATCH * MEM_TOKENS, D)
    n_phys = cache_diff_k.shape[1]
    w_exp_gu_b = w_exp_gu.astype(BF16)
    w_exp_down_b = w_exp_down.astype(BF16)
    cmk = cache_mem_k.reshape(DEPTH * DEC_BATCH, MEM_TOKENS * MEM_HEADS, LANES)
    cmv = cache_mem_v.reshape(DEPTH * DEC_BATCH, MEM_TOKENS * MEM_HEADS, LANES)

    def sample_rows(t):
        return t[mp:].reshape(DEC_BATCH, DEC_PAD, -1)[:, :DEC_SEQ]

    def pad_gates(w):
        return jnp.pad(w, ((0, 0), (0, LANES - w.shape[1])))

    mlstm_p, mlstm_s, diff_p, diff_s, gdn_p, gdn_s, mem_k_new, mem_v_new = [], [], [], [], [], [], [], []
    for i in range(DEPTH):
        kind, j = i % 3, i // 3
        mem_kv = matmul(mem2, w_mem_kv[i], MEM_TOKENS, 512).reshape(BATCH, MEM_TOKENS, 2 * MEM_Q)
        mem_k_new.append(mem_kv[:, :, :MEM_Q].reshape(BATCH, MEM_TOKENS, MEM_HEADS, 128))
        mem_v_new.append(mem_kv[:, :, MEM_Q:].reshape(BATCH, MEM_TOKENS, MEM_HEADS, 128))
        if kind == 0:
            w = w_in_a[j]
            w_main = jnp.concatenate([w[:, :3 * D], w[:, 3 * D + 2 * H:]], axis=1)
            w_g = pad_gates(w[:, 3 * D:3 * D + 2 * H])
            u = matmul(x, w_main, tm, 512)
            gts = matmul(x, w_g, tm, LANES)
            zc = jnp.zeros((BATCH, H, 128, 64), F32)
            zn = jnp.zeros((BATCH, H, 64), F32)
            zm = jnp.zeros((BATCH, H), F32)
            mix_p, c_p, n_p, m_p = mlstm(u, gts, b_gate_a[j], norm_a[j], zc, zn, zm, BATCH, SEQ, CHUNK, 256, CHUNK)
            mix_s, c_s, n_s, m_s = mlstm(u, gts, b_gate_a[j], norm_a[j], state_mlstm_C[j], state_mlstm_n[j],
                                         state_mlstm_m[j], DEC_BATCH, DEC_PAD, DEC_PAD, DEC_PAD, DEC_SEQ, row0=mp)
            mlstm_p.append((c_p, n_p, m_p.reshape(BATCH, H)))
            mlstm_s.append((c_s, n_s, m_s.reshape(DEC_BATCH, H)))
            mq_col = 6
        elif kind == 1:
            lam_init = 0.8 - 0.6 * math.exp(-0.3 * i)
            u = matmul(x, w_in_b[j], tm, 512)
            mix_p = diff_attn_prompt(u, lambda_b[j], norm_b[j], rel_bias, lam_init)
            mix_s = diff_attn_sample(u, cache_diff_k[j].reshape(n_phys, PAGE * H, LANES),
                                     cache_diff_v[j].reshape(n_phys, PAGE * H, LANES), page_table,
                                     lambda_b[j], norm_b[j], rel_bias, lam_init, row0=mp)
            k_rows, v_rows = kv_head_rows(u, mp, 256)
            diff_p.append((k_rows.reshape(BATCH, SEQ, H, 128), v_rows.reshape(BATCH, SEQ, H, 128)))
            u_s = sample_rows(u)
            diff_s.append((u_s[:, :, D:2 * D].reshape(DEC_BATCH, DEC_SEQ, H, 128),
                           u_s[:, :, 2 * D:3 * D].reshape(DEC_BATCH, DEC_SEQ, H, 128)))
            mq_col = 6
        else:
            w = w_in_c[j]
            w_main = jnp.concatenate([w[:, :4 * D], w[:, 4 * D + 2 * H:]], axis=1)
            w_g = pad_gates(w[:, 4 * D:4 * D + 2 * H])
            u = matmul(x, w_main, tm, 512)
            gts = matmul(x, w_g, tm, LANES)
            zs = jnp.zeros((BATCH, H, 128, 128), F32)
            zv = jnp.zeros((BATCH, 3, 3 * D), F32)
            mix_p, s_p = gdn(u, gts, conv_c[j], a_log_c[j], dt_bias_c[j], norm_c[j], zs, zv,
                             BATCH, SEQ, CHUNK, 256, CHUNK)
            mix_s, s_s = gdn(u, gts, conv_c[j], a_log_c[j], dt_bias_c[j], norm_c[j], state_gdn_S[j],
                             state_gdn_conv[j], DEC_BATCH, DEC_PAD, DEC_PAD, DEC_PAD, DEC_SEQ, row0=mp)
            conv_p = jnp.stack([u[(b + 1) * SEQ - 3:(b + 1) * SEQ, :3 * D] for b in range(BATCH)])
            conv_s = sample_rows(u)[:, DEC_SEQ - 3:, :3 * D]
            gdn_p.append((s_p, conv_p))
            gdn_s.append((s_s, conv_s))
            mq_col = 8
        mo_p = mem_attention(u, mq_col, mem_kv, 0, mem_kv, 1, 512, BATCH, SEQ)
        mo_s = mem_attention(u, mq_col, cmk, 0, cmv, 0, DEC_PAD, DEC_BATCH, DEC_PAD, row0=mp, seq0=i * DEC_BATCH,
                             head_rows=True)
        x = outproj_ln(mix_p, mo_p, mix_s, mo_s, x, w_out[i], ln_g[i, 0], ln_b[i, 0])
        k_f = i // 2
        if i % 2 == 0:
            x = ffn_ln(x, w_ffn_gu[k_f], w_ffn_down[k_f], ln_g[i, 1], ln_b[i, 1], tm)
        else:
            x = moe_ln(x, w_router[k_f], w_exp_gu_b, w_exp_down_b, k_f, ln_g[i, 1], ln_b[i, 1])

    def stack(lst, k):
        return jnp.stack([t[k] for t in lst])

    return (x[:mp].reshape(BATCH, SEQ, D), sample_rows(x),
            stack(mlstm_p, 0), stack(mlstm_p, 1), stack(mlstm_p, 2),
            stack(mlstm_s, 0), stack(mlstm_s, 1), stack(mlstm_s, 2),
            stack(diff_p, 0), stack(diff_p, 1), stack(diff_s, 0), stack(diff_s, 1),
            stack(gdn_p, 0), stack(gdn_p, 1), stack(gdn_s, 0), stack(gdn_s, 1),
            jnp.stack(mem_k_new), jnp.stack(mem_v_new))
```

```python
import functools
import math

import jax
import jax.numpy as jnp
from jax import lax
from jax.experimental import pallas as pl
from jax.experimental.pallas import tpu as pltpu

F32 = jnp.float32
BF16 = jnp.bfloat16
HI = lax.Precision.HIGHEST
NT = (((1,), (1,)), ((), ()))
TN = (((0,), (0,)), ((), ()))

D = 1024
H = 8
DEPTH = 4
SEQ = 4096
BATCH = 4
DEC_BATCH = 32
DEC_SEQ = 4
DEC_PAD = 8
PAST_LEN = 8192
PAGE = 128
N_PAGES = PAST_LEN // PAGE
PAGES_PER_STEP = 8
ROW_TILE = 640
MEM_TOKENS = 256
MEM_Q = 512
MEM_HEADS = 4
CHUNK = 64
N_EXPERTS = 8
D_FF = 2752
D_FF_PAD = 2816
D_FF_EXPERT = 3584
EPS = 1e-6
ALPHA = (2 * DEPTH) ** 0.25
NEG = -1e30
LANES = 128
VMEM_LIMIT = 56 * 1024 * 1024


def _cp(n_axes, vmem=VMEM_LIMIT):
    return pltpu.CompilerParams(dimension_semantics=("arbitrary",) * n_axes, vmem_limit_bytes=vmem)


def _dot(a, b):
    return jnp.dot(a.astype(BF16), b.astype(BF16), preferred_element_type=F32)


def _dot_nt(a, b):
    return lax.dot_general(a.astype(BF16), b.astype(BF16), NT, preferred_element_type=F32)


def _dot_tn(a, b):
    return lax.dot_general(a.astype(BF16), b.astype(BF16), TN, preferred_element_type=F32)


def _dot_hi(a, b):
    return jnp.dot(a, b, preferred_element_type=F32, precision=HI)


def _split_bf16(a):
    hi = a.astype(BF16)
    return hi, (a - hi.astype(F32)).astype(BF16)


def _dot_split(a_hi, a_lo, b_hi, b_lo):
    return (jnp.dot(a_hi, b_hi, preferred_element_type=F32)
            + (jnp.dot(a_lo, b_hi, preferred_element_type=F32) + jnp.dot(a_hi, b_lo, preferred_element_type=F32)))


def _sigmoid(x):
    return 1.0 / (1.0 + jnp.exp(-x))


def _silu(x):
    return x * _sigmoid(x)


def _softplus(x):
    return jnp.maximum(x, 0.0) + jnp.log1p(jnp.exp(-jnp.abs(x)))


def _log_sigmoid(x):
    return -_softplus(-x)


def _layer_norm(v, g, b):
    mu = jnp.mean(v, axis=-1, keepdims=True)
    d = v - mu
    var = jnp.mean(d * d, axis=-1, keepdims=True)
    return d * lax.rsqrt(var + EPS) * g + b


def _rms(h):
    return h * lax.rsqrt(jnp.mean(h * h, axis=-1, keepdims=True) + EPS)


def _mm_kernel(x_ref, w_ref, o_ref, *, tn):
    xb = x_ref[...].astype(BF16)
    for n0 in range(0, o_ref.shape[1], tn):
        o_ref[:, n0:n0 + tn] = jnp.dot(xb, w_ref[:, n0:n0 + tn], preferred_element_type=F32)


def matmul(x, w, tm, tn):
    m, k = x.shape
    n = w.shape[1]
    return pl.pallas_call(
        functools.partial(_mm_kernel, tn=min(tn, n)),
        grid=(m // tm,),
        in_specs=[pl.BlockSpec((tm, k), lambda i: (i, 0)),
                  pl.BlockSpec((k, n), lambda i: (0, 0), pipeline_mode=pl.Buffered(1))],
        out_specs=pl.BlockSpec((tm, n), lambda i: (i, 0)),
        out_shape=jax.ShapeDtypeStruct((m, n), F32),
        compiler_params=_cp(1),
        name="matmul",
    )(x, w.astype(BF16))


def _headrows_kernel(k_ref, v_ref, ko_ref, vo_ref):
    tm = k_ref.shape[0]
    for h in range(H):
        sl = slice(h * LANES, (h + 1) * LANES)
        ko_ref[pl.ds(h, tm, stride=H), :] = k_ref[:, sl]
        vo_ref[pl.ds(h, tm, stride=H), :] = v_ref[:, sl]


def kv_head_rows(u, m, tm):
    return pl.pallas_call(
        _headrows_kernel,
        grid=(m // tm,),
        in_specs=[pl.BlockSpec((tm, D), lambda i: (i, 1)),
                  pl.BlockSpec((tm, D), lambda i: (i, 2))],
        out_specs=[pl.BlockSpec((tm * H, LANES), lambda i: (i, 0)),
                   pl.BlockSpec((tm * H, LANES), lambda i: (i, 0))],
        out_shape=[jax.ShapeDtypeStruct((m * H, LANES), F32), jax.ShapeDtypeStruct((m * H, LANES), F32)],
        compiler_params=_cp(1),
        name="kv_head_rows",
    )(u, u)


def _outproj_kernel(mixp_ref, q_ref, k_ref, v_ref, mixs_ref, mos_ref, x_ref, w1_ref, w2_ref, g_ref, b_ref, o_ref,
                    mo_scr, *, n_prompt):
    is_p = pl.program_id(0) < n_prompt

    @pl.when(is_p)
    def _():
        q = q_ref[...]
        for h in range(MEM_HEADS):
            sl = slice(h * LANES, (h + 1) * LANES)
            s = _dot_nt(q[:, sl], k_ref[:, sl]) * LANES ** -0.5
            s = s - jnp.max(s, axis=-1, keepdims=True)
            p = jnp.exp(s)
            p = p / jnp.sum(p, axis=-1, keepdims=True)
            mo_scr[:, sl] = _dot(p, v_ref[:, sl])

    @pl.when(jnp.logical_not(is_p))
    def _():
        mo_scr[...] = mos_ref[...]

    mix = jnp.where(is_p, mixp_ref[...], mixs_ref[...])
    y = _dot(mix, w1_ref[...]) + _dot(mo_scr[...], w2_ref[...])
    o_ref[...] = _layer_norm(ALPHA * x_ref[...] + y, g_ref[...], b_ref[...])


def outproj_ln(mix_p, u, q_col, mem_kv, mix_s, mo_s, x, w_out, g, b):
    m = x.shape[0]
    tm = mix_s.shape[0]
    n_prompt = mix_p.shape[0] // tm
    per_seq = SEQ // tm
    w1 = w_out[:D].astype(BF16)
    w2 = w_out[D:].astype(BF16)

    def prow(i):
        return jnp.minimum(i, n_prompt - 1)

    return pl.pallas_call(
        functools.partial(_outproj_kernel, n_prompt=n_prompt),
        grid=(m // tm,),
        in_specs=[pl.BlockSpec((tm, D), lambda i: (prow(i), 0)),
                  pl.BlockSpec((tm, MEM_Q), lambda i: (prow(i), q_col)),
                  pl.BlockSpec((None, MEM_TOKENS, MEM_Q), lambda i: (prow(i) // per_seq, 0, 0)),
                  pl.BlockSpec((None, MEM_TOKENS, MEM_Q), lambda i: (prow(i) // per_seq, 0, 1)),
                  pl.BlockSpec((tm, D), lambda i: (0, 0)),
                  pl.BlockSpec((tm, MEM_Q), lambda i: (0, 0)),
                  pl.BlockSpec((tm, D), lambda i: (i, 0)),
                  pl.BlockSpec((D, D), lambda i: (0, 0)),
                  pl.BlockSpec((MEM_Q, D), lambda i: (0, 0)),
                  pl.BlockSpec((1, D), lambda i: (0, 0)),
                  pl.BlockSpec((1, D), lambda i: (0, 0))],
        out_specs=pl.BlockSpec((tm, D), lambda i: (i, 0)),
        out_shape=jax.ShapeDtypeStruct((m, D), F32),
        scratch_shapes=[pltpu.VMEM((tm, MEM_Q), F32)],
        compiler_params=_cp(1),
        name="outproj_ln",
    )(mix_p, u, mem_kv, mem_kv, mix_s, mo_s, x, w1, w2, g.reshape(1, D), b.reshape(1, D))


def _ffn_kernel(x_ref, wgu_ref, wd_ref, g_ref, b_ref, o_ref, *, tf):
    x = x_ref[...]
    xb = x.astype(BF16)
    for f0 in range(0, D_FF_PAD, tf):
        gate = jnp.dot(xb, wgu_ref[:, f0:f0 + tf], preferred_element_type=F32)
        up = jnp.dot(xb, wgu_ref[:, D_FF_PAD + f0:D_FF_PAD + f0 + tf], preferred_element_type=F32)
        part = jnp.dot((_silu(gate) * up).astype(BF16), wd_ref[f0:f0 + tf, :], preferred_element_type=F32)
        if f0 == 0:
            o_ref[...] = part
        else:
            o_ref[...] += part
    o_ref[...] = _layer_norm(ALPHA * x + o_ref[...], g_ref[...], b_ref[...])


def ffn_ln(x, w_gu, w_down, g, b, tm, tf=256):
    m = x.shape[0]
    pad = D_FF_PAD - D_FF
    wgu = jnp.concatenate([jnp.pad(w_gu[:, :D_FF], ((0, 0), (0, pad))),
                           jnp.pad(w_gu[:, D_FF:], ((0, 0), (0, pad)))], axis=1).astype(BF16)
    wd = jnp.pad(w_down, ((0, pad), (0, 0))).astype(BF16)
    return pl.pallas_call(
        functools.partial(_ffn_kernel, tf=tf),
        grid=(m // tm,),
        in_specs=[pl.BlockSpec((tm, D), lambda i: (i, 0)),
                  pl.BlockSpec((D, 2 * D_FF_PAD), lambda i: (0, 0), pipeline_mode=pl.Buffered(1)),
                  pl.BlockSpec((D_FF_PAD, D), lambda i: (0, 0), pipeline_mode=pl.Buffered(1)),
                  pl.BlockSpec((1, D), lambda i: (0, 0)),
                  pl.BlockSpec((1, D), lambda i: (0, 0))],
        out_specs=pl.BlockSpec((tm, D), lambda i: (i, 0)),
        out_shape=jax.ShapeDtypeStruct((m, D), F32),
        compiler_params=_cp(1),
        name="ffn_ln",
    )(x, wgu, wd, g.reshape(1, D), b.reshape(1, D))


def _memattn_kernel(q_ref, k_ref, v_ref, o_ref, *, head_rows):
    q = q_ref[...]
    for h in range(MEM_HEADS):
        sl = slice(h * LANES, (h + 1) * LANES)
        if head_rows:
            k = k_ref[pl.ds(h, MEM_TOKENS, stride=MEM_HEADS), :]
            v = v_ref[pl.ds(h, MEM_TOKENS, stride=MEM_HEADS), :]
        else:
            k = k_ref[:, sl]
            v = v_ref[:, sl]
        s = _dot_nt(q[:, sl], k) * LANES ** -0.5
        s = s - jnp.max(s, axis=-1, keepdims=True)
        p = jnp.exp(s)
        p = p / jnp.sum(p, axis=-1, keepdims=True)
        o_ref[:, sl] = _dot(p, v)


def mem_attention(u, q_col, mem_k, k_col, mem_v, v_col, tq, nseq, seqlen, row0=0, seq0=0, head_rows=False):
    nq = seqlen // tq
    off = row0 // tq
    if head_rows:
        kv_block = (None, MEM_TOKENS * MEM_HEADS, LANES)
    else:
        kv_block = (None, MEM_TOKENS, MEM_Q)
    kern = functools.partial(_memattn_kernel, head_rows=head_rows)
    in_specs = [pl.BlockSpec((tq, MEM_Q), lambda bi, i: (off + bi * nq + i, q_col)),
                pl.BlockSpec(kv_block, lambda bi, i: (bi + seq0, 0, k_col)),
                pl.BlockSpec(kv_block, lambda bi, i: (bi + seq0, 0, v_col))]
    return pl.pallas_call(
        kern,
        grid=(nseq, nq),
        in_specs=in_specs,
        out_specs=pl.BlockSpec((tq, MEM_Q), lambda bi, i: (bi * nq + i, 0)),
        out_shape=jax.ShapeDtypeStruct((nseq * seqlen, MEM_Q), F32),
        compiler_params=_cp(2),
        name="mem_attention",
    )(u, mem_k, mem_v)


def _mlstm_kernel(qk_ref, v_ref, og_ref, gc_ref, gt_ref, bgc_ref, bgr_ref, ng_ref, c0_ref, n0_ref, m0_ref,
                  mix_ref, co_ref, no_ref, mo_ref, c_scr, n_scr, m_scr, *, T, nc, n_valid, sps):
    s = pl.program_id(0)

    @pl.when(s % sps == 0)
    def _():
        c_scr[...] = c0_ref[...]
        n_scr[...] = n0_ref[...]
        m_scr[...] = m0_ref[...]

    row = lax.broadcasted_iota(jnp.int32, (T, T), 0)
    col = lax.broadcasted_iota(jnp.int32, (T, T), 1)
    causal = col <= row
    tril = causal.astype(F32)
    triu = (row <= col).astype(F32)

    cs = range(nc)
    hs = range(H)
    ch = [(c, h) for c in cs for h in hs]
    rows = [slice(c * T, (c + 1) * T) for c in cs]
    g, bcol, brow, igr_all = [], [], [], []
    for c in cs:
        g_c = gc_ref[rows[c], :] + bgc_ref[...]
        lfc = _log_sigmoid(g_c)
        gt = gt_ref[c] + bgr_ref[...]
        igr = gt[0:H, :]
        lfr = _log_sigmoid(gt[H:2 * H, :])
        if n_valid < T:
            rid = lax.broadcasted_iota(jnp.int32, (T, LANES), 0)
            g_c = jnp.where(rid < n_valid, g_c, NEG)
            lfc = jnp.where(rid < n_valid, lfc, 0.0)
            cid = lax.broadcasted_iota(jnp.int32, (H, T), 1)
            igr = jnp.where(cid < n_valid, igr, NEG)
            lfr = jnp.where(cid < n_valid, lfr, 0.0)
        g.append(g_c)
        igr_all.append(igr)
        bcol.append(_dot_hi(tril, lfc))
        brow.append(_dot_hi(lfr, triu))
    bc = {(c, h): bcol[c][:, H + h:H + h + 1] for c, h in ch}
    log_d = {(c, h): jnp.where(causal, bc[c, h] - brow[c][h:h + 1, :] + igr_all[c][h:h + 1, :], -jnp.inf)
             for c, h in ch}
    rmax = {(c, h): jnp.max(log_d[c, h], axis=-1, keepdims=True) for c, h in ch}
    m_all = m_scr[...]
    m_prev = [m_all[:, h:h + 1] for h in hs]
    inter, m_t = {}, {}
    for c, h in ch:
        inter[c, h] = bc[c, h] + m_prev[h]
        m_t[c, h] = jnp.maximum(inter[c, h], rmax[c, h])
        m_prev[h] = m_t[c, h][T - 1:T, :]
    qk = [qk_ref[rows[c], :] for c in cs]
    vv = [v_ref[rows[c], :] for c in cs]
    q = {(c, h): (qk[c][:, h * 64:(h + 1) * 64] * 0.125).astype(BF16) for c, h in ch}
    k = {(c, h): qk[c][:, 512 + h * 64:512 + (h + 1) * 64] for c, h in ch}
    kb = {(c, h): k[c, h].astype(BF16) for c, h in ch}
    v = {(c, h): vv[c][:, h * LANES:(h + 1) * LANES] for c, h in ch}
    qkt = {(c, h): lax.dot_general(q[c, h], kb[c, h], NT, preferred_element_type=F32) for c, h in ch}
    w_inter = {(c, h): jnp.exp(inter[c, h] - m_t[c, h]) for c, h in ch}
    sc = {(c, h): qkt[c, h] * jnp.exp(log_d[c, h] - m_t[c, h]) for c, h in ch}
    sv = {(c, h): jnp.dot(sc[c, h].astype(BF16), v[c, h].astype(BF16), preferred_element_type=F32) for c, h in ch}
    den0 = {(c, h): jnp.sum(sc[c, h], axis=-1, keepdims=True) for c, h in ch}
    w_end = {(c, h): jnp.exp(bc[c, h][T - 1:T, :] - bc[c, h] + g[c][:, h:h + 1] - m_t[c, h][T - 1:T, :])
             for c, h in ch}
    vk = {(c, h): lax.dot_general((w_end[c, h] * v[c, h]).astype(BF16), kb[c, h], TN, preferred_element_type=F32)
          for c, h in ch}
    kw = {(c, h): jnp.sum(w_end[c, h] * k[c, h], axis=0, keepdims=True) for c, h in ch}
    c_in = {(0, h): c_scr[h] for h in hs}
    n_in = {(0, h): n_scr[h:h + 1, :] for h in hs}
    for c, h in ch:
        decay = w_inter[c, h][T - 1:T, :]
        c_in[c + 1, h] = decay * c_in[c, h] + vk[c, h]
        n_in[c + 1, h] = decay * n_in[c, h] + kw[c, h]
    qc = {(c, h): lax.dot_general(q[c, h], c_in[c, h].astype(BF16), NT, preferred_element_type=F32) for c, h in ch}
    for c, h in ch:
        num = sv[c, h] + w_inter[c, h] * qc[c, h]
        n_b = n_in[c, h].astype(BF16).astype(F32)
        den = den0[c, h] + w_inter[c, h] * jnp.sum(q[c, h].astype(F32) * n_b, axis=-1, keepdims=True)
        hh = num / jnp.maximum(jnp.abs(den), jnp.exp(-m_t[c, h]))
        sl = slice(h * LANES, (h + 1) * LANES)
        mix_ref[rows[c], sl] = _sigmoid(og_ref[rows[c], sl]) * (_rms(hh) * ng_ref[:, sl])
    for h in hs:
        c_scr[h] = c_in[nc, h]
        n_scr[h:h + 1, :] = n_in[nc, h]
        m_scr[:, h:h + 1] = m_prev[h]

    @pl.when(s % sps == sps - 1)
    def _():
        co_ref[...] = c_scr[...]
        no_ref[...] = n_scr[...]
        mo_ref[...] = m_scr[...]


def mlstm(u, gates, b_gate, norm_g, c0, n0, m0, nseq, seqlen, T, rows_per_step, n_valid, row0=0):
    m = nseq * seqlen
    nc = rows_per_step // T
    sps = seqlen // rows_per_step
    gt = jnp.transpose(gates[row0:row0 + m, :2 * H].reshape(m // T, T, 2 * H), (0, 2, 1))
    bgc = jnp.zeros((1, LANES), F32).at[0, :2 * H].set(b_gate.reshape(-1))
    bgr = b_gate.reshape(2 * H, 1)
    r = rows_per_step
    off = row0 // r
    kern = functools.partial(_mlstm_kernel, T=T, nc=nc, n_valid=n_valid, sps=sps)
    in_specs = [pl.BlockSpec((r, D), lambda s: (s + off, 0)),
                pl.BlockSpec((r, D), lambda s: (s + off, 1)),
                pl.BlockSpec((r, D), lambda s: (s + off, 2)),
                pl.BlockSpec((r, LANES), lambda s: (s + off, 0)),
                pl.BlockSpec((nc, 2 * H, T), lambda s: (s, 0, 0)),
                pl.BlockSpec((1, LANES), lambda s: (0, 0)),
                pl.BlockSpec((2 * H, 1), lambda s: (0, 0)),
                pl.BlockSpec((1, D), lambda s: (0, 0)),
                pl.BlockSpec((None, H, 128, 64), lambda s: (s // sps, 0, 0, 0)),
                pl.BlockSpec((None, H, 64), lambda s: (s // sps, 0, 0)),
                pl.BlockSpec((None, 1, H), lambda s: (s // sps, 0, 0))]
    args = [u, u, u, gates, gt, bgc, bgr, norm_g.reshape(1, D), c0, n0, m0.reshape(nseq, 1, H)]
    return pl.pallas_call(
        kern,
        grid=(m // r,),
        in_specs=in_specs,
        out_specs=[pl.BlockSpec((r, D), lambda s: (s, 0)),
                   pl.BlockSpec((None, H, 128, 64), lambda s: (s // sps, 0, 0, 0)),
                   pl.BlockSpec((None, H, 64), lambda s: (s // sps, 0, 0)),
                   pl.BlockSpec((None, 1, H), lambda s: (s // sps, 0, 0))],
        out_shape=[jax.ShapeDtypeStruct((m, D), F32),
                   jax.ShapeDtypeStruct((nseq, H, 128, 64), F32),
                   jax.ShapeDtypeStruct((nseq, H, 64), F32),
                   jax.ShapeDtypeStruct((nseq, 1, H), F32)],
        scratch_shapes=[pltpu.VMEM((H, 128, 64), F32), pltpu.VMEM((H, 64), F32), pltpu.VMEM((1, H), F32)],
        compiler_params=_cp(1),
        name="mlstm",
    )(*args)


def _gdn_kernel(x_ref, z_ref, gc_ref, gt_ref, cw_ref, pc_ref, pr_ref, ng_ref, s0_ref, cv0_ref,
                mix_ref, so_ref, xbuf, cv_scr, s_scr, *, T, nc, n_valid, sps):
    s = pl.program_id(0)
    R = T * nc

    @pl.when(s % sps == 0)
    def _():
        s_scr[...] = s0_ref[...]
        xbuf[5:8, :] = cv0_ref[...]

    xbuf[8:8 + R, :] = x_ref[...]
    conv = (cw_ref[0:1, :] * xbuf[5:5 + R, :] + cw_ref[1:2, :] * xbuf[6:6 + R, :]
            + cw_ref[2:3, :] * xbuf[7:7 + R, :] + cw_ref[3:4, :] * xbuf[8:8 + R, :])
    cv_scr[...] = _silu(conv)
    xbuf[0:8, :] = xbuf[R:R + 8, :]

    row = lax.broadcasted_iota(jnp.int32, (T, T), 0)
    col = lax.broadcasted_iota(jnp.int32, (T, T), 1)
    causal = col <= row
    strict = col < row
    tril = causal.astype(F32)
    triu = (row <= col).astype(F32)

    def chunk(c, carry):
        r0 = pl.multiple_of(c * T, T)
        rows = pl.ds(r0, T)
        gpre = gc_ref[rows, :]
        beta_all = _sigmoid(gpre)
        g_all = -jnp.exp(pc_ref[0:1, :]) * _softplus(gpre + pc_ref[1:2, :])
        gt = gt_ref[c]
        gr_all = -jnp.exp(pr_ref[:, 0:1]) * _softplus(gt + pr_ref[:, 1:2])
        gr = gr_all[H:2 * H, :]
        if n_valid < T:
            rid = lax.broadcasted_iota(jnp.int32, (T, LANES), 0)
            beta_all = jnp.where(rid < n_valid, beta_all, 0.0)
            g_all = jnp.where(rid < n_valid, g_all, 0.0)
            cid = lax.broadcasted_iota(jnp.int32, (H, T), 1)
            gr = jnp.where(cid < n_valid, gr, 0.0)
        gamc_all = _dot_hi(tril, g_all)
        gamr_all = _dot_hi(gr, triu)
        hs = range(H)
        qf = [cv_scr[rows, h * LANES:(h + 1) * LANES] for h in hs]
        kf = [cv_scr[rows, D + h * LANES:D + (h + 1) * LANES] for h in hs]
        vf = [cv_scr[rows, 2 * D + h * LANES:2 * D + (h + 1) * LANES] for h in hs]
        qb = [(qf[h] * lax.rsqrt(jnp.sum(qf[h] * qf[h], axis=-1, keepdims=True) + EPS) * LANES ** -0.5).astype(BF16)
              for h in hs]
        kn = [kf[h] * lax.rsqrt(jnp.sum(kf[h] * kf[h], axis=-1, keepdims=True) + EPS) for h in hs]
        kb = [kn[h].astype(BF16) for h in hs]
        beta = [beta_all[:, h:h + 1] for h in hs]
        gamc = [gamc_all[:, H + h:H + h + 1] for h in hs]
        egam = [jnp.exp(gamc[h]) for h in hs]
        dec = [jnp.exp(jnp.where(causal, gamc[h] - gamr_all[h:h + 1, :], NEG)) for h in hs]
        s_old = [s_scr[h] for h in hs]
        sb = [s_old[h].astype(BF16) for h in hs]
        qkk = [lax.dot_general(jnp.concatenate([qb[h], kb[h]], axis=0), kb[h], NT, preferred_element_type=F32)
               for h in hs]
        qs = [lax.dot_general(qb[h], sb[h], NT, preferred_element_type=F32) for h in hs]
        nmat = [jnp.where(strict, beta[h] * qkk[h][T:2 * T] * dec[h], 0.0) for h in hs]
        y = [-nmat[h] for h in hs]
        npow = nmat
        nsp = [_split_bf16(nmat[h]) for h in hs]
        span = 2
        while span < T:
            if T < 16:
                npow = [_dot_hi(npow[h], npow[h]) for h in hs]
                y = [y[h] + npow[h] + _dot_hi(y[h], npow[h]) for h in hs]
            else:
                npow = [_dot_split(nsp[h][0], nsp[h][1], nsp[h][0], nsp[h][1]) for h in hs]
                nsp = [_split_bf16(npow[h]) for h in hs]
                ysp = [_split_bf16(y[h]) for h in hs]
                y = [y[h] + npow[h] + _dot_split(ysp[h][0], ysp[h][1], nsp[h][0], nsp[h][1]) for h in hs]
            span *= 2
        rhs = [jnp.concatenate([beta[h] * vf[h], (beta[h] * egam[h]) * kn[h]], axis=1) for h in hs]
        if T < 16:
            sol = [rhs[h] + _dot_hi(y[h], rhs[h]) for h in hs]
        else:
            sol = [rhs[h] + _dot(y[h], rhs[h]) for h in hs]
        delta = [sol[h][:, 0:LANES] - _dot_nt(sol[h][:, LANES:2 * LANES], sb[h]) for h in hs]
        o = [egam[h] * qs[h] + _dot(qkk[h][0:T] * dec[h], delta[h]) for h in hs]
        g_end = [gamc[h][T - 1:T, :] for h in hs]
        upd = [_dot_tn(jnp.exp(g_end[h] - gamc[h]) * delta[h], kb[h]) for h in hs]
        for h in hs:
            sl = slice(h * LANES, (h + 1) * LANES)
            s_scr[h] = jnp.exp(g_end[h]) * s_old[h] + upd[h]
            mix_ref[rows, sl] = _rms(o[h]) * ng_ref[:, sl] * _silu(z_ref[rows, sl])
        return carry

    lax.fori_loop(0, nc, chunk, 0, unroll=min(nc, 2))

    @pl.when(s % sps == sps - 1)
    def _():
        so_ref[...] = s_scr[...]


def gdn(u, gates, conv_w, a_log, dt_bias, norm_g, s0, conv0, nseq, seqlen, T, rows_per_step, n_valid, row0=0):
    m = nseq * seqlen
    nc = rows_per_step // T
    sps = seqlen // rows_per_step
    r = rows_per_step
    off = row0 // r
    gt = jnp.transpose(gates[row0:row0 + m, :2 * H].reshape(m // T, T, 2 * H), (0, 2, 1))
    pc = jnp.zeros((2, LANES), F32).at[0, H:2 * H].set(a_log).at[1, H:2 * H].set(dt_bias)
    pr = jnp.zeros((2 * H, 2), F32).at[H:, 0].set(a_log).at[H:, 1].set(dt_bias)
    kern = functools.partial(_gdn_kernel, T=T, nc=nc, n_valid=n_valid, sps=sps)
    in_specs = [pl.BlockSpec((r, 3 * D), lambda s: (s + off, 0)),
                pl.BlockSpec((r, D), lambda s: (s + off, 3)),
                pl.BlockSpec((r, LANES), lambda s: (s + off, 0)),
                pl.BlockSpec((nc, 2 * H, T), lambda s: (s, 0, 0)),
                pl.BlockSpec((4, 3 * D), lambda s: (0, 0)),
                pl.BlockSpec((2, LANES), lambda s: (0, 0)),
                pl.BlockSpec((2 * H, 2), lambda s: (0, 0)),
                pl.BlockSpec((1, D), lambda s: (0, 0)),
                pl.BlockSpec((None, H, 128, 128), lambda s: (s // sps, 0, 0, 0)),
                pl.BlockSpec((None, 3, 3 * D), lambda s: (s // sps, 0, 0))]
    args = [u, u, gates, gt, conv_w, pc, pr, norm_g.reshape(1, D), s0, conv0]
    return pl.pallas_call(
        kern,
        grid=(m // r,),
        in_specs=in_specs,
        out_specs=[pl.BlockSpec((r, D), lambda s: (s, 0)),
                   pl.BlockSpec((None, H, 128, 128), lambda s: (s // sps, 0, 0, 0))],
        out_shape=[jax.ShapeDtypeStruct((m, D), F32),
                   jax.ShapeDtypeStruct((nseq, H, 128, 128), F32)],
        scratch_shapes=[pltpu.VMEM((r + 8, 3 * D), F32), pltpu.VMEM((r, 3 * D), F32),
                        pltpu.VMEM((H, 128, 128), F32)],
        compiler_params=_cp(1),
        name="gdn",
    )(*args)


def _bucket(rel):
    n = jnp.maximum(rel, 0)
    nf = jnp.maximum(n, 1).astype(F32)
    large = 16 + (jnp.log(nf / 16) / math.log(128 / 16) * 16).astype(jnp.int32)
    return jnp.where(n < 16, n, jnp.minimum(large, 31))


def _diff_lambda(lw_ref, lam_init):
    lw = lw_ref[...]
    a = jnp.sum(lw[0:1, :] * lw[1:2, :], axis=-1, keepdims=True)
    b = jnp.sum(lw[2:3, :] * lw[3:4, :], axis=-1, keepdims=True)
    return jnp.exp(a) - jnp.exp(b) + lam_init


def _dattn_p_kernel(q_ref, k_ref, v_ref, bn_ref, lw_ref, ng_ref, o_ref, m_scr, acc_scr, *, TB, lam_init):
    i = pl.program_id(2)
    W = 2 * TB
    lane = lax.broadcasted_iota(jnp.int32, (TB, LANES), 1)
    q = q_ref[...] * 0.125
    qq = jnp.concatenate([jnp.where(lane < 64, q, 0.0), jnp.where(lane >= 64, q, 0.0)], axis=0).astype(BF16)
    m_scr[...] = jnp.full_like(m_scr, NEG)
    acc_scr[...] = jnp.zeros_like(acc_scr)

    def step(k0, width, bias):
        rows = pl.ds(pl.multiple_of(k0, TB), width)
        kb = k_ref[rows, :].astype(BF16)
        vext = jnp.concatenate([v_ref[rows, :].astype(BF16), jnp.ones((width, LANES), BF16)], axis=1)
        sc = lax.dot_general(qq, kb, NT, preferred_element_type=F32)
        if bias is not None:
            sc = sc + bias
        ng = width // LANES
        mloc = sc[:, 0:LANES]
        for g in range(1, ng):
            mloc = jnp.maximum(mloc, sc[:, g * LANES:(g + 1) * LANES])
        m_old = m_scr[...]
        m_new = jnp.maximum(m_old, jnp.max(mloc, axis=-1, keepdims=True))
        alpha = jnp.exp(m_old - m_new)
        p = jnp.exp(sc - jnp.concatenate([m_new] * ng, axis=1)).astype(BF16)
        acc_scr[...] = (jnp.concatenate([alpha, alpha], axis=1) * acc_scr[...]
                        + jnp.dot(p, vext, preferred_element_type=F32))
        m_scr[...] = m_new

    n_far = jnp.maximum(i - 1, 0)
    n_big = n_far // 4
    rem = n_far % 4

    def body(c, carry):
        step(c * 2 * W, W, None)
        step(c * 2 * W + W, W, None)
        return carry

    lax.fori_loop(0, n_big, body, 0)

    @pl.when(rem >= 2)
    def _():
        step(n_big * 2 * W, W, None)

    @pl.when(rem % 2 == 1)
    def _():
        step((n_far - 1) * TB, TB, None)

    step(n_far * TB, W, bn_ref[...])
    lam = _diff_lambda(lw_ref, lam_init)
    acc = acc_scr[...]
    o = (acc[0:TB, 0:LANES] / acc[0:TB, LANES:2 * LANES]
         - lam * (acc[TB:2 * TB, 0:LANES] / acc[TB:2 * TB, LANES:2 * LANES]))
    o_ref[...] = _rms(o) * ng_ref[...] * (1.0 - lam_init)


def _bias_table(rel_bias, rel):
    onehot = (_bucket(rel)[..., None] == jnp.arange(32)).astype(F32)
    return jnp.einsum('...k,kh->h...', onehot, rel_bias - rel_bias[31:32], precision=HI)


def diff_attn_prompt(u, lam_w, norm_g, rel_bias, lam_init, TB=256):
    nq = SEQ // TB
    ii = jnp.arange(TB)
    rel_d = ii[:, None] - ii[None, :]
    bd = jnp.where((rel_d >= 0)[None], _bias_table(rel_bias, rel_d), NEG)
    bs = _bias_table(rel_bias, rel_d + TB)
    masked = jnp.full((H, TB, TB), NEG, F32)
    bn = jnp.stack([jnp.concatenate([bd, masked], axis=2), jnp.concatenate([bs, bd], axis=2)], axis=1)
    bn = jnp.concatenate([bn, bn], axis=2)
    kern = functools.partial(_dattn_p_kernel, TB=TB, lam_init=lam_init)
    return pl.pallas_call(
        kern,
        grid=(BATCH, H, nq),
        in_specs=[pl.BlockSpec((TB, LANES), lambda b, h, i: (b * nq + i, h)),
                  pl.BlockSpec((SEQ, LANES), lambda b, h, i: (b, H + h)),
                  pl.BlockSpec((SEQ, LANES), lambda b, h, i: (b, 2 * H + h)),
                  pl.BlockSpec((None, None, 2 * TB, 2 * TB), lambda b, h, i: (h, jnp.minimum(i, 1), 0, 0)),
                  pl.BlockSpec((4, 64), lambda b, h, i: (0, 0)),
                  pl.BlockSpec((1, LANES), lambda b, h, i: (0, h))],
        out_specs=pl.BlockSpec((TB, LANES), lambda b, h, i: (b * nq + i, h)),
        out_shape=jax.ShapeDtypeStruct((BATCH * SEQ, D), F32),
        scratch_shapes=[pltpu.VMEM((2 * TB, LANES), F32), pltpu.VMEM((2 * TB, 2 * LANES), F32)],
        compiler_params=_cp(3),
        name="diff_attn_prompt",
    )(u, u, u, bn, lam_w, norm_g.reshape(1, D))


def _dattn_s_kernel(pt_ref, qm_ref, qb_ref, *refs, lam_init):
    del pt_ref
    kp_refs = refs[0:PAGES_PER_STEP]
    vp_refs = refs[PAGES_PER_STEP:2 * PAGES_PER_STEP]
    kn_ref, vn_ref, bl_ref, bn_ref, lw_ref, ng_ref, o_ref, m_scr, l_scr, acc_scr = refs[2 * PAGES_PER_STEP:]
    p = pl.program_id(1)
    last = pl.num_programs(1) - 1
    rows_h = 2 * DEC_PAD

    @pl.when(p == 0)
    def _():
        m_scr[...] = jnp.full_like(m_scr, NEG)
        l_scr[...] = jnp.zeros_like(l_scr)
        acc_scr[...] = jnp.zeros_like(acc_scr)

    def head_rows(page_refs, h):
        return jnp.concatenate([r[pl.ds(h, PAGE, stride=H), :].astype(BF16) for r in page_refs], axis=0)

    qm = qm_ref[...].astype(BF16)
    sc = jnp.concatenate(
        [lax.dot_general(qm[h * rows_h:(h + 1) * rows_h], head_rows(kp_refs, h), NT, preferred_element_type=F32)
         for h in range(H)], axis=0)
    sc = sc + jnp.where(p == last, bl_ref[...], 0.0)
    m_old = m_scr[...]
    m_new = jnp.maximum(m_old, jnp.max(sc, axis=-1, keepdims=True))
    alpha = jnp.exp(m_old - m_new)
    pr = jnp.exp(sc - jnp.concatenate([m_new] * PAGES_PER_STEP, axis=1))
    l_scr[...] = alpha * l_scr[...] + jnp.sum(pr, axis=-1, keepdims=True)
    prb = pr.astype(BF16)
    pv = jnp.concatenate(
        [jnp.dot(prb[h * rows_h:(h + 1) * rows_h], head_rows(vp_refs, h), preferred_element_type=F32)
         for h in range(H)], axis=0)
    acc_scr[...] = alpha * acc_scr[...] + pv
    m_scr[...] = m_new

    @pl.when(p == last)
    def _():
        scn = lax.dot_general(qb_ref[...], kn_ref[...], NT, preferred_element_type=F32) + bn_ref[...]
        m_o = m_scr[...]
        m_n = jnp.maximum(m_o, jnp.max(scn, axis=-1, keepdims=True))
        al = jnp.exp(m_o - m_n)
        pn = jnp.exp(scn - m_n[:, 0:DEC_PAD])
        l_n = al * l_scr[...] + jnp.sum(pn, axis=-1, keepdims=True)
        pvn = jnp.dot(pn, vn_ref[...], preferred_element_type=F32)
        lam = _diff_lambda(lw_ref, lam_init)
        for h in range(H):
            sl = slice(h * LANES, (h + 1) * LANES)
            r0 = slice(h * rows_h, h * rows_h + DEC_PAD)
            r1 = slice(h * rows_h + DEC_PAD, (h + 1) * rows_h)
            a0 = (al[r0] * acc_scr[r0, :] + pvn[r0, sl]) / l_n[r0]
            a1 = (al[r1] * acc_scr[r1, :] + pvn[r1, sl]) / l_n[r1]
            o_ref[:, sl] = _rms(a0 - lam * a1) * ng_ref[:, sl] * (1.0 - lam_init)


def diff_attn_sample(u, cache_k, cache_v, page_table, lam_w, norm_g, rel_bias, lam_init, row0):
    nb = DEC_BATCH
    nr = 2 * H * DEC_PAD
    off = row0 // DEC_PAD
    q = u[row0:row0 + nb * DEC_PAD, :D].reshape(nb, DEC_PAD, D) * 0.125
    grp_row = jnp.arange(2 * H)[:, None, None]
    grp_col = (jnp.arange(D) // 64)[None, None, :]
    qbig = jnp.where(grp_row == grp_col, q[:, None, :, :], 0.0).reshape(nb, nr, D)
    q4 = jnp.transpose(q.reshape(nb, DEC_PAD, H, 1, LANES), (0, 2, 3, 1, 4))
    comp = (jnp.arange(LANES) // 64)[None, None, None, None, :]
    qm = jnp.where(comp == jnp.arange(2)[None, None, :, None, None], q4, 0.0).reshape(nb, nr, LANES)
    qi = jnp.arange(DEC_PAD)
    rel_last = PAGE + qi[:, None] - jnp.arange(PAGE)[None, :]
    b_last = jnp.broadcast_to(_bias_table(rel_bias, rel_last)[:, None], (H, 2, DEC_PAD, PAGE)).reshape(nr, PAGE)
    b_last = jnp.pad(b_last, ((0, 0), ((PAGES_PER_STEP - 1) * PAGE, 0)))
    rel_new = qi[:, None] - qi[None, :]
    vis = (rel_new >= 0) & (qi[None, :] < DEC_SEQ)
    b_new = jnp.where(vis[None], _bias_table(rel_bias, rel_new), NEG)
    b_new = jnp.broadcast_to(b_new[:, None], (H, 2, DEC_PAD, DEC_PAD)).reshape(nr, DEC_PAD)
    kern = functools.partial(_dattn_s_kernel, lam_init=lam_init)
    def page_spec(slot):
        return pl.BlockSpec((None, PAGE * H, LANES), lambda b, p, pt: (pt[b, p * PAGES_PER_STEP + slot], 0, 0))

    pages = [page_spec(slot) for slot in range(PAGES_PER_STEP)]
    in_specs = [pl.BlockSpec((None, nr, LANES), lambda b, p, pt: (b, 0, 0)),
                pl.BlockSpec((None, nr, D), lambda b, p, pt: (b, 0, 0))] + pages + pages + [
                pl.BlockSpec((DEC_PAD, D), lambda b, p, pt: (b + off, 1)),
                pl.BlockSpec((DEC_PAD, D), lambda b, p, pt: (b + off, 2)),
                pl.BlockSpec((nr, PAGES_PER_STEP * PAGE), lambda b, p, pt: (0, 0)),
                pl.BlockSpec((nr, DEC_PAD), lambda b, p, pt: (0, 0)),
                pl.BlockSpec((4, 64), lambda b, p, pt: (0, 0)),
                pl.BlockSpec((1, D), lambda b, p, pt: (0, 0))]
    args = [qm, qbig, *([cache_k] * PAGES_PER_STEP), *([cache_v] * PAGES_PER_STEP), u, u, b_last, b_new,
            lam_w, norm_g.reshape(1, D)]
    grid_spec = pltpu.PrefetchScalarGridSpec(
        num_scalar_prefetch=1,
        grid=(nb, N_PAGES // PAGES_PER_STEP),
        in_specs=in_specs,
        out_specs=pl.BlockSpec((DEC_PAD, D), lambda b, p, pt: (b, 0)),
        scratch_shapes=[pltpu.VMEM((nr, LANES), F32), pltpu.VMEM((nr, LANES), F32), pltpu.VMEM((nr, LANES), F32)],
    )
    return pl.pallas_call(
        kern,
        grid_spec=grid_spec,
        out_shape=jax.ShapeDtypeStruct((nb * DEC_PAD, D), F32),
        compiler_params=_cp(2),
        name="diff_attn_sample",
    )(page_table, *args)


def _router_kernel(x_ref, w_ref, o_ref):
    logits = _dot(x_ref[...], w_ref[...])
    lane = lax.broadcasted_iota(jnp.int32, logits.shape, 1)
    logits = jnp.where(lane < N_EXPERTS, logits, -jnp.inf)
    m1 = jnp.max(logits, axis=-1, keepdims=True)
    i1 = jnp.min(jnp.where(logits == m1, lane, LANES), axis=-1, keepdims=True)
    rest = jnp.where(lane == i1, -jnp.inf, logits)
    m2 = jnp.max(rest, axis=-1, keepdims=True)
    i2 = jnp.min(jnp.where(rest == m2, lane, LANES), axis=-1, keepdims=True)
    e2 = jnp.exp(m2 - m1)
    g1 = 1.0 / (1.0 + e2)
    g2 = e2 / (1.0 + e2)
    o_ref[...] = jnp.where(lane == 0, i1.astype(F32),
                           jnp.where(lane == 1, i2.astype(F32),
                                     jnp.where(lane == 2, g1, jnp.where(lane == 3, g2, 0.0))))


def router(x, w_router, tm):
    m = x.shape[0]
    w = jnp.pad(w_router, ((0, 0), (0, LANES - N_EXPERTS)))
    return pl.pallas_call(
        _router_kernel,
        grid=(m // tm,),
        in_specs=[pl.BlockSpec((tm, D), lambda i: (i, 0)),
                  pl.BlockSpec((D, LANES), lambda i: (0, 0))],
        out_specs=pl.BlockSpec((tm, LANES), lambda i: (i, 0)),
        out_shape=jax.ShapeDtypeStruct((m, LANES), F32),
        compiler_params=_cp(1),
        name="router",
    )(x, w)


def _expert_kernel(be_ref, src_ref, x_hbm, wgu_ref, wd_ref, o_ref, xbuf, sem, *, tf, tm):
    del be_ref
    i = pl.program_id(0)
    nb = pl.num_programs(0)
    slot = i % 2
    nslot = 1 - slot
    nxt = jnp.minimum(i + 1, nb - 1)

    def row_copy(block, r, sl):
        return pltpu.make_async_copy(x_hbm.at[pl.ds(src_ref[block * tm + r], 1), :],
                                     xbuf.at[sl, pl.ds(r, 1), :], sem.at[sl])

    def wait_rows(sl):
        def drain(r, carry):
            row_copy(0, r, sl).wait()
            return carry
        lax.fori_loop(0, tm, drain, 0, unroll=8)

    @pl.when(i == 0)
    def _():
        def issue(r, carry):
            row_copy(0, r, 0).start()
            return carry
        lax.fori_loop(0, tm, issue, 0, unroll=8)

    wait_rows(slot)
    xb = xbuf[slot].astype(BF16)
    chunks = list(range(0, D_FF_EXPERT, tf))
    per = -(-tm // len(chunks))
    for ci, f0 in enumerate(chunks):
        gate = jnp.dot(xb, wgu_ref[:, f0:f0 + tf], preferred_element_type=F32)
        up = jnp.dot(xb, wgu_ref[:, D_FF_EXPERT + f0:D_FF_EXPERT + f0 + tf], preferred_element_type=F32)
        part = jnp.dot((_silu(gate) * up).astype(BF16), wd_ref[f0:f0 + tf, :], preferred_element_type=F32)
        if ci == 0:
            o_ref[...] = part
        else:
            o_ref[...] += part
        for r in range(ci * per, min((ci + 1) * per, tm)):
            row_copy(nxt, r, nslot).start()

    @pl.when(i == nb - 1)
    def _():
        wait_rows(nslot)


def expert_ffn(x_z, src, w_gu, w_down, layer, blk_e, tm, tf=512):
    n = src.shape[0]
    grid_spec = pltpu.PrefetchScalarGridSpec(
        num_scalar_prefetch=2,
        grid=(n // tm,),
        in_specs=[pl.BlockSpec(memory_space=pl.ANY),
                  pl.BlockSpec((None, None, D, 2 * D_FF_EXPERT), lambda i, be, sr: (layer, be[i], 0, 0),
                               pipeline_mode=pl.Buffered(1)),
                  pl.BlockSpec((None, None, D_FF_EXPERT, D), lambda i, be, sr: (layer, be[i], 0, 0),
                               pipeline_mode=pl.Buffered(1))],
        out_specs=pl.BlockSpec((tm, D), lambda i, be, sr: (i, 0)),
        scratch_shapes=[pltpu.VMEM((2, tm, D), F32), pltpu.SemaphoreType.DMA((2,))],
    )
    return pl.pallas_call(
        functools.partial(_expert_kernel, tf=tf, tm=tm),
        grid_spec=grid_spec,
        out_shape=jax.ShapeDtypeStruct((n, D), F32),
        compiler_params=_cp(1),
        name="expert_ffn",
    )(blk_e, src, x_z, w_gu, w_down)


def _combine_kernel(d0_ref, d1_ref, yb_hbm, x_ref, rt_ref, g_ref, b_ref, o_ref, buf0, buf1, sem, *, G):
    base = pl.program_id(0) * G

    def issue(r, carry):
        pltpu.make_async_copy(yb_hbm.at[pl.ds(d0_ref[base + r], 1), :], buf0.at[pl.ds(r, 1), :], sem).start()
        pltpu.make_async_copy(yb_hbm.at[pl.ds(d1_ref[base + r], 1), :], buf1.at[pl.ds(r, 1), :], sem).start()
        return carry

    lax.fori_loop(0, G, issue, 0, unroll=4)

    def drain(r, carry):
        pltpu.make_async_copy(yb_hbm.at[pl.ds(0, 1), :], buf0.at[pl.ds(r, 1), :], sem).wait()
        pltpu.make_async_copy(yb_hbm.at[pl.ds(0, 1), :], buf1.at[pl.ds(r, 1), :], sem).wait()
        return carry

    lax.fori_loop(0, G, drain, 0, unroll=4)
    rt = rt_ref[...]
    y = rt[:, 2:3] * buf0[...] + rt[:, 3:4] * buf1[...]
    o_ref[...] = _layer_norm(ALPHA * x_ref[...] + y, g_ref[...], b_ref[...])


def combine_ln(yb, x, route, d0, d1, g, b, G):
    m = x.shape[0]
    grid_spec = pltpu.PrefetchScalarGridSpec(
        num_scalar_prefetch=2,
        grid=(m // G,),
        in_specs=[pl.BlockSpec(memory_space=pl.ANY),
                  pl.BlockSpec((G, D), lambda i, a, c: (i, 0)),
                  pl.BlockSpec((G, LANES), lambda i, a, c: (i, 0)),
                  pl.BlockSpec((1, D), lambda i, a, c: (0, 0)),
                  pl.BlockSpec((1, D), lambda i, a, c: (0, 0))],
        out_specs=pl.BlockSpec((G, D), lambda i, a, c: (i, 0)),
        scratch_shapes=[pltpu.VMEM((G, D), F32), pltpu.VMEM((G, D), F32), pltpu.SemaphoreType.DMA(())],
    )
    return pl.pallas_call(
        functools.partial(_combine_kernel, G=G),
        grid_spec=grid_spec,
        out_shape=jax.ShapeDtypeStruct((m, D), F32),
        compiler_params=_cp(1),
        name="combine_ln",
    )(d0, d1, yb, x, route, g.reshape(1, D), b.reshape(1, D))


def moe_ln(x, w_router, w_gu, w_down, layer, g, b, tm=256):
    m = x.shape[0]
    route = router(x, w_router, 640)
    flat_e = route[:, :2].astype(jnp.int32).reshape(-1)
    onehot = (flat_e[:, None] == jnp.arange(N_EXPERTS)[None, :]).astype(jnp.int32)
    csum = jnp.cumsum(onehot, axis=0)
    counts = csum[-1]
    padded = (counts + tm - 1) // tm * tm
    p_end = jnp.cumsum(padded)
    dest = jnp.sum(onehot * ((p_end - padded)[None, :] + csum - 1), axis=1)
    n_blocks = -(-(2 * m + N_EXPERTS * (tm - 1)) // tm)
    n_rows = n_blocks * tm
    n_used = (p_end[-1] // tm).astype(jnp.int32).reshape(1)
    blk = jnp.minimum(jnp.arange(n_blocks), n_used[0] - 1) * tm
    blk_e = jnp.minimum(jnp.sum((blk[:, None] >= p_end[None, :]).astype(jnp.int32), axis=1), N_EXPERTS - 1)
    x_z = jnp.concatenate([x, jnp.zeros((8, D), x.dtype)], axis=0)
    src = jnp.full((n_rows,), m, jnp.int32).at[dest].set(jnp.arange(2 * m, dtype=jnp.int32) // 2)
    yb = expert_ffn(x_z, src, w_gu, w_down, layer, blk_e, tm)
    d = dest.reshape(m, 2).astype(jnp.int32)
    return combine_ln(yb, x, route, d[:, 0], d[:, 1], g, b, 640)


def kernel(x_prompt, x_sample, mem_prompt, page_table, cache_diff_k, cache_diff_v, cache_mem_k, cache_mem_v,
           state_mlstm_C, state_mlstm_n, state_mlstm_m, state_gdn_S, state_gdn_conv, rel_bias,
           w_in_a, b_gate_a, norm_a, w_in_b, lambda_b, norm_b, w_in_c, conv_c, a_log_c, dt_bias_c, norm_c,
           w_mem_kv, w_out, ln_g, ln_b, w_ffn_gu, w_ffn_down, w_router, w_exp_gu, w_exp_down):
    mp = BATCH * SEQ
    ms = DEC_BATCH * DEC_PAD
    tm = ROW_TILE
    x = jnp.concatenate([x_prompt.reshape(mp, D),
                         jnp.pad(x_sample, ((0, 0), (0, DEC_PAD - DEC_SEQ), (0, 0))).reshape(ms, D)], axis=0)
    mem2 = mem_prompt.reshape(BATCH * MEM_TOKENS, D)
    n_phys = cache_diff_k.shape[1]
    w_exp_gu_b = w_exp_gu.astype(BF16)
    w_exp_down_b = w_exp_down.astype(BF16)
    cmk = cache_mem_k.reshape(DEPTH * DEC_BATCH, MEM_TOKENS * MEM_HEADS, LANES)
    cmv = cache_mem_v.reshape(DEPTH * DEC_BATCH, MEM_TOKENS * MEM_HEADS, LANES)

    def sample_rows(t):
        return t[mp:].reshape(DEC_BATCH, DEC_PAD, -1)[:, :DEC_SEQ]

    def pad_gates(w):
        return jnp.pad(w, ((0, 0), (0, LANES - w.shape[1])))

    mlstm_p, mlstm_s, diff_p, diff_s, gdn_p, gdn_s, mem_k_new, mem_v_new = [], [], [], [], [], [], [], []
    for i in range(DEPTH):
        kind, j = i % 3, i // 3
        mem_kv = matmul(mem2, w_mem_kv[i], MEM_TOKENS, 512).reshape(BATCH, MEM_TOKENS, 2 * MEM_Q)
        mem_k_new.append(mem_kv[:, :, :MEM_Q].reshape(BATCH, MEM_TOKENS, MEM_HEADS, 128))
        mem_v_new.append(mem_kv[:, :, MEM_Q:].reshape(BATCH, MEM_TOKENS, MEM_HEADS, 128))
        if kind == 0:
            w = w_in_a[j]
            w_main = jnp.concatenate([w[:, :3 * D], w[:, 3 * D + 2 * H:]], axis=1)
            w_g = pad_gates(w[:, 3 * D:3 * D + 2 * H])
            u = matmul(x, w_main, tm, 512)
            gts = matmul(x, w_g, tm, LANES)
            zc = jnp.zeros((BATCH, H, 128, 64), F32)
            zn = jnp.zeros((BATCH, H, 64), F32)
            zm = jnp.zeros((BATCH, H), F32)
            mix_p, c_p, n_p, m_p = mlstm(u, gts, b_gate_a[j], norm_a[j], zc, zn, zm, BATCH, SEQ, CHUNK, 256, CHUNK)
            mix_s, c_s, n_s, m_s = mlstm(u, gts, b_gate_a[j], norm_a[j], state_mlstm_C[j], state_mlstm_n[j],
                                         state_mlstm_m[j], DEC_BATCH, DEC_PAD, DEC_PAD, DEC_PAD, DEC_SEQ, row0=mp)
            mlstm_p.append((c_p, n_p, m_p.reshape(BATCH, H)))
            mlstm_s.append((c_s, n_s, m_s.reshape(DEC_BATCH, H)))
            mq_col = 6
        elif kind == 1:
            lam_init = 0.8 - 0.6 * math.exp(-0.3 * i)
            u = matmul(x, w_in_b[j], tm, 512)
            mix_p = diff_attn_prompt(u, lambda_b[j], norm_b[j], rel_bias, lam_init)
            mix_s = diff_attn_sample(u, cache_diff_k[j].reshape(n_phys, PAGE * H, LANES),
                                     cache_diff_v[j].reshape(n_phys, PAGE * H, LANES), page_table,
                                     lambda_b[j], norm_b[j], rel_bias, lam_init, row0=mp)
            k_rows, v_rows = kv_head_rows(u, mp, 256)
            diff_p.append((k_rows.reshape(BATCH, SEQ, H, 128), v_rows.reshape(BATCH, SEQ, H, 128)))
            u_s = sample_rows(u)
            diff_s.append((u_s[:, :, D:2 * D].reshape(DEC_BATCH, DEC_SEQ, H, 128),
                           u_s[:, :, 2 * D:3 * D].reshape(DEC_BATCH, DEC_SEQ, H, 128)))
            mq_col = 6
        else:
            w = w_in_c[j]
            w_main = jnp.concatenate([w[:, :4 * D], w[:, 4 * D + 2 * H:]], axis=1)
            w_g = pad_gates(w[:, 4 * D:4 * D + 2 * H])
            u = matmul(x, w_main, tm, 512)
            gts = matmul(x, w_g, tm, LANES)
            zs = jnp.zeros((BATCH, H, 128, 128), F32)
            zv = jnp.zeros((BATCH, 3, 3 * D), F32)
            mix_p, s_p = gdn(u, gts, conv_c[j], a_log_c[j], dt_bias_c[j], norm_c[j], zs, zv,
                             BATCH, SEQ, CHUNK, 256, CHUNK)
            mix_s, s_s = gdn(u, gts, conv_c[j], a_log_c[j], dt_bias_c[j], norm_c[j], state_gdn_S[j],
                             state_gdn_conv[j], DEC_BATCH, DEC_PAD, DEC_PAD, DEC_PAD, DEC_SEQ, row0=mp)
            conv_p = jnp.stack([u[(b + 1) * SEQ - 3:(b + 1) * SEQ, :3 * D] for b in range(BATCH)])
            conv_s = sample_rows(u)[:, DEC_SEQ - 3:, :3 * D]
            gdn_p.append((s_p, conv_p))
            gdn_s.append((s_s, conv_s))
            mq_col = 8
        mo_s = mem_attention(u, mq_col, cmk, 0, cmv, 0, DEC_PAD, DEC_BATCH, DEC_PAD, row0=mp, seq0=i * DEC_BATCH,
                             head_rows=True)
        x = outproj_ln(mix_p, u, mq_col, mem_kv, mix_s, mo_s, x, w_out[i], ln_g[i, 0], ln_b[i, 0])
        k_f = i // 2
        if i % 2 == 0:
            x = ffn_ln(x, w_ffn_gu[k_f], w_ffn_down[k_f], ln_g[i, 1], ln_b[i, 1], tm)
        else:
            x = moe_ln(x, w_router[k_f], w_exp_gu_b, w_exp_down_b, k_f, ln_g[i, 1], ln_b[i, 1])

    def stack(lst, k):
        return jnp.stack([t[k] for t in lst])

    return (x[:mp].reshape(BATCH, SEQ, D), sample_rows(x),
            stack(mlstm_p, 0), stack(mlstm_p, 1), stack(mlstm_p, 2),
            stack(mlstm_s, 0), stack(mlstm_s, 1), stack(mlstm_s, 2),
            stack(diff_p, 0), stack(diff_p, 1), stack(diff_s, 0), stack(diff_s, 1),
            stack(gdn_p, 0), stack(gdn_p, 1), stack(gdn_s, 0), stack(gdn_s, 1),
            jnp.stack(mem_k_new), jnp.stack(mem_v_new))
```

```python
import functools
import math

import jax
import jax.numpy as jnp
from jax import lax
from jax.experimental import pallas as pl
from jax.experimental.pallas import tpu as pltpu

F32 = jnp.float32
BF16 = jnp.bfloat16
HI = lax.Precision.HIGHEST
NT = (((1,), (1,)), ((), ()))
TN = (((0,), (0,)), ((), ()))

D = 1024
H = 8
DEPTH = 4
SEQ = 4096
BATCH = 4
DEC_BATCH = 32
DEC_SEQ = 4
DEC_PAD = 8
PAST_LEN = 8192
PAGE = 128
N_PAGES = PAST_LEN // PAGE
PAGES_PER_STEP = 8
ROW_TILE = 640
MEM_TOKENS = 256
MEM_Q = 512
MEM_HEADS = 4
CHUNK = 64
N_EXPERTS = 8
D_FF = 2752
D_FF_PAD = 2816
D_FF_EXPERT = 3584
EPS = 1e-6
ALPHA = (2 * DEPTH) ** 0.25
NEG = -1e30
LANES = 128
VMEM_LIMIT = 56 * 1024 * 1024


def _cp(n_axes, vmem=VMEM_LIMIT):
    return pltpu.CompilerParams(dimension_semantics=("arbitrary",) * n_axes, vmem_limit_bytes=vmem)


def _dot(a, b):
    return jnp.dot(a.astype(BF16), b.astype(BF16), preferred_element_type=F32)


def _dot_nt(a, b):
    return lax.dot_general(a.astype(BF16), b.astype(BF16), NT, preferred_element_type=F32)


def _dot_tn(a, b):
    return lax.dot_general(a.astype(BF16), b.astype(BF16), TN, preferred_element_type=F32)


def _dot_hi(a, b):
    return jnp.dot(a, b, preferred_element_type=F32, precision=HI)


def _split_bf16(a):
    hi = a.astype(BF16)
    return hi, (a - hi.astype(F32)).astype(BF16)


def _dot_split(a_hi, a_lo, b_hi, b_lo):
    return (jnp.dot(a_hi, b_hi, preferred_element_type=F32)
            + (jnp.dot(a_lo, b_hi, preferred_element_type=F32) + jnp.dot(a_hi, b_lo, preferred_element_type=F32)))


def _sigmoid(x):
    return 1.0 / (1.0 + jnp.exp(-x))


def _silu(x):
    return x * _sigmoid(x)


def _softplus(x):
    return jnp.maximum(x, 0.0) + jnp.log1p(jnp.exp(-jnp.abs(x)))


def _log_sigmoid(x):
    return -_softplus(-x)


def _layer_norm(v, g, b):
    mu = jnp.mean(v, axis=-1, keepdims=True)
    d = v - mu
    var = jnp.mean(d * d, axis=-1, keepdims=True)
    return d * lax.rsqrt(var + EPS) * g + b


def _rms(h):
    return h * lax.rsqrt(jnp.mean(h * h, axis=-1, keepdims=True) + EPS)


def _mm_kernel(x_ref, w_ref, o_ref, *, tn):
    xb = x_ref[...].astype(BF16)
    n = o_ref.shape[1]
    for n0 in range(0, n, tn):
        n1 = min(n0 + tn, n)
        o_ref[:, n0:n1] = jnp.dot(xb, w_ref[:, n0:n1], preferred_element_type=F32)


def matmul(x, w, tm, tn):
    m, k = x.shape
    n = w.shape[1]
    return pl.pallas_call(
        functools.partial(_mm_kernel, tn=min(tn, n)),
        grid=(m // tm,),
        in_specs=[pl.BlockSpec((tm, k), lambda i: (i, 0)),
                  pl.BlockSpec((k, n), lambda i: (0, 0), pipeline_mode=pl.Buffered(1))],
        out_specs=pl.BlockSpec((tm, n), lambda i: (i, 0)),
        out_shape=jax.ShapeDtypeStruct((m, n), F32),
        compiler_params=_cp(1),
        name="matmul",
    )(x, w.astype(BF16))


def _headrows_kernel(k_ref, v_ref, ko_ref, vo_ref):
    tm = k_ref.shape[0]
    for h in range(H):
        sl = slice(h * LANES, (h + 1) * LANES)
        ko_ref[pl.ds(h, tm, stride=H), :] = k_ref[:, sl]
        vo_ref[pl.ds(h, tm, stride=H), :] = v_ref[:, sl]


def kv_head_rows(u, m, tm):
    return pl.pallas_call(
        _headrows_kernel,
        grid=(m // tm,),
        in_specs=[pl.BlockSpec((tm, D), lambda i: (i, 1)),
                  pl.BlockSpec((tm, D), lambda i: (i, 2))],
        out_specs=[pl.BlockSpec((tm * H, LANES), lambda i: (i, 0)),
                   pl.BlockSpec((tm * H, LANES), lambda i: (i, 0))],
        out_shape=[jax.ShapeDtypeStruct((m * H, LANES), F32), jax.ShapeDtypeStruct((m * H, LANES), F32)],
        compiler_params=_cp(1),
        name="kv_head_rows",
    )(u, u)


def _outproj_kernel(mixp_ref, q_ref, k_ref, v_ref, mixs_ref, mos_ref, x_ref, w1_ref, w2_ref, g_ref, b_ref, o_ref,
                    mo_scr, *, n_prompt):
    is_p = pl.program_id(0) < n_prompt

    @pl.when(is_p)
    def _():
        q = q_ref[...]
        for h in range(MEM_HEADS):
            sl = slice(h * LANES, (h + 1) * LANES)
            s = _dot_nt(q[:, sl], k_ref[:, sl]) * LANES ** -0.5
            s = s - jnp.max(s, axis=-1, keepdims=True)
            p = jnp.exp(s)
            p = p / jnp.sum(p, axis=-1, keepdims=True)
            mo_scr[:, sl] = _dot(p, v_ref[:, sl])

    @pl.when(jnp.logical_not(is_p))
    def _():
        mo_scr[...] = mos_ref[...]

    mix = jnp.where(is_p, mixp_ref[...], mixs_ref[...])
    y = _dot(mix, w1_ref[...]) + _dot(mo_scr[...], w2_ref[...])
    o_ref[...] = _layer_norm(ALPHA * x_ref[...] + y, g_ref[...], b_ref[...])


def outproj_ln(mix_p, u, q_col, mem_kv, mix_s, mo_s, x, w_out, g, b):
    m = x.shape[0]
    tm = mix_s.shape[0]
    n_prompt = mix_p.shape[0] // tm
    per_seq = SEQ // tm
    w1 = w_out[:D].astype(BF16)
    w2 = w_out[D:].astype(BF16)

    def prow(i):
        return jnp.minimum(i, n_prompt - 1)

    return pl.pallas_call(
        functools.partial(_outproj_kernel, n_prompt=n_prompt),
        grid=(m // tm,),
        in_specs=[pl.BlockSpec((tm, D), lambda i: (prow(i), 0)),
                  pl.BlockSpec((tm, MEM_Q), lambda i: (prow(i), q_col)),
                  pl.BlockSpec((None, MEM_TOKENS, MEM_Q), lambda i: (prow(i) // per_seq, 0, 0)),
                  pl.BlockSpec((None, MEM_TOKENS, MEM_Q), lambda i: (prow(i) // per_seq, 0, 1)),
                  pl.BlockSpec((tm, D), lambda i: (0, 0)),
                  pl.BlockSpec((tm, MEM_Q), lambda i: (0, 0)),
                  pl.BlockSpec((tm, D), lambda i: (i, 0)),
                  pl.BlockSpec((D, D), lambda i: (0, 0)),
                  pl.BlockSpec((MEM_Q, D), lambda i: (0, 0)),
                  pl.BlockSpec((1, D), lambda i: (0, 0)),
                  pl.BlockSpec((1, D), lambda i: (0, 0))],
        out_specs=pl.BlockSpec((tm, D), lambda i: (i, 0)),
        out_shape=jax.ShapeDtypeStruct((m, D), F32),
        scratch_shapes=[pltpu.VMEM((tm, MEM_Q), F32)],
        compiler_params=_cp(1),
        name="outproj_ln",
    )(mix_p, u, mem_kv, mem_kv, mix_s, mo_s, x, w1, w2, g.reshape(1, D), b.reshape(1, D))


def _ffn_kernel(x_ref, wgu_ref, wd_ref, g_ref, b_ref, o_ref, *, tf):
    x = x_ref[...]
    xb = x.astype(BF16)
    for f0 in range(0, D_FF_PAD, tf):
        gate = jnp.dot(xb, wgu_ref[:, f0:f0 + tf], preferred_element_type=F32)
        up = jnp.dot(xb, wgu_ref[:, D_FF_PAD + f0:D_FF_PAD + f0 + tf], preferred_element_type=F32)
        part = jnp.dot((_silu(gate) * up).astype(BF16), wd_ref[f0:f0 + tf, :], preferred_element_type=F32)
        if f0 == 0:
            o_ref[...] = part
        else:
            o_ref[...] += part
    o_ref[...] = _layer_norm(ALPHA * x + o_ref[...], g_ref[...], b_ref[...])


def ffn_ln(x, w_gu, w_down, g, b, tm, tf=256):
    m = x.shape[0]
    pad = D_FF_PAD - D_FF
    wgu = jnp.concatenate([jnp.pad(w_gu[:, :D_FF], ((0, 0), (0, pad))),
                           jnp.pad(w_gu[:, D_FF:], ((0, 0), (0, pad)))], axis=1).astype(BF16)
    wd = jnp.pad(w_down, ((0, pad), (0, 0))).astype(BF16)
    return pl.pallas_call(
        functools.partial(_ffn_kernel, tf=tf),
        grid=(m // tm,),
        in_specs=[pl.BlockSpec((tm, D), lambda i: (i, 0)),
                  pl.BlockSpec((D, 2 * D_FF_PAD), lambda i: (0, 0), pipeline_mode=pl.Buffered(1)),
                  pl.BlockSpec((D_FF_PAD, D), lambda i: (0, 0), pipeline_mode=pl.Buffered(1)),
                  pl.BlockSpec((1, D), lambda i: (0, 0)),
                  pl.BlockSpec((1, D), lambda i: (0, 0))],
        out_specs=pl.BlockSpec((tm, D), lambda i: (i, 0)),
        out_shape=jax.ShapeDtypeStruct((m, D), F32),
        compiler_params=_cp(1),
        name="ffn_ln",
    )(x, wgu, wd, g.reshape(1, D), b.reshape(1, D))


def _memattn_kernel(q_ref, k_ref, v_ref, o_ref, *, head_rows):
    q = q_ref[...]
    for h in range(MEM_HEADS):
        sl = slice(h * LANES, (h + 1) * LANES)
        if head_rows:
            k = k_ref[pl.ds(h, MEM_TOKENS, stride=MEM_HEADS), :]
            v = v_ref[pl.ds(h, MEM_TOKENS, stride=MEM_HEADS), :]
        else:
            k = k_ref[:, sl]
            v = v_ref[:, sl]
        s = _dot_nt(q[:, sl], k) * LANES ** -0.5
        s = s - jnp.max(s, axis=-1, keepdims=True)
        p = jnp.exp(s)
        p = p / jnp.sum(p, axis=-1, keepdims=True)
        o_ref[:, sl] = _dot(p, v)


def mem_attention(u, q_col, mem_k, k_col, mem_v, v_col, tq, nseq, seqlen, row0=0, seq0=0, head_rows=False):
    nq = seqlen // tq
    off = row0 // tq
    if head_rows:
        kv_block = (None, MEM_TOKENS * MEM_HEADS, LANES)
    else:
        kv_block = (None, MEM_TOKENS, MEM_Q)
    kern = functools.partial(_memattn_kernel, head_rows=head_rows)
    in_specs = [pl.BlockSpec((tq, MEM_Q), lambda bi, i: (off + bi * nq + i, q_col)),
                pl.BlockSpec(kv_block, lambda bi, i: (bi + seq0, 0, k_col)),
                pl.BlockSpec(kv_block, lambda bi, i: (bi + seq0, 0, v_col))]
    return pl.pallas_call(
        kern,
        grid=(nseq, nq),
        in_specs=in_specs,
        out_specs=pl.BlockSpec((tq, MEM_Q), lambda bi, i: (bi * nq + i, 0)),
        out_shape=jax.ShapeDtypeStruct((nseq * seqlen, MEM_Q), F32),
        compiler_params=_cp(2),
        name="mem_attention",
    )(u, mem_k, mem_v)


def _mlstm_kernel(qk_ref, v_ref, og_ref, gc_ref, gt_ref, bgc_ref, bgr_ref, ng_ref, c0_ref, n0_ref, m0_ref,
                  mix_ref, co_ref, no_ref, mo_ref, c_scr, n_scr, m_scr, *, T, nc, n_valid, sps):
    s = pl.program_id(0)

    @pl.when(s % sps == 0)
    def _():
        c_scr[...] = c0_ref[...]
        n_scr[...] = n0_ref[...]
        m_scr[...] = m0_ref[...]

    row = lax.broadcasted_iota(jnp.int32, (T, T), 0)
    col = lax.broadcasted_iota(jnp.int32, (T, T), 1)
    causal = col <= row
    tril = causal.astype(F32)
    triu = (row <= col).astype(F32)

    cs = range(nc)
    hs = range(H)
    ch = [(c, h) for c in cs for h in hs]
    rows = [slice(c * T, (c + 1) * T) for c in cs]
    g, bcol, brow, igr_all = [], [], [], []
    for c in cs:
        g_c = gc_ref[rows[c], :] + bgc_ref[...]
        lfc = _log_sigmoid(g_c)
        gt = gt_ref[c] + bgr_ref[...]
        igr = gt[0:H, :]
        lfr = _log_sigmoid(gt[H:2 * H, :])
        if n_valid < T:
            rid = lax.broadcasted_iota(jnp.int32, (T, LANES), 0)
            g_c = jnp.where(rid < n_valid, g_c, NEG)
            lfc = jnp.where(rid < n_valid, lfc, 0.0)
            cid = lax.broadcasted_iota(jnp.int32, (H, T), 1)
            igr = jnp.where(cid < n_valid, igr, NEG)
            lfr = jnp.where(cid < n_valid, lfr, 0.0)
        g.append(g_c)
        igr_all.append(igr)
        bcol.append(_dot_hi(tril, lfc))
        brow.append(_dot_hi(lfr, triu))
    bc = {(c, h): bcol[c][:, H + h:H + h + 1] for c, h in ch}
    log_d = {(c, h): jnp.where(causal, bc[c, h] - brow[c][h:h + 1, :] + igr_all[c][h:h + 1, :], -jnp.inf)
             for c, h in ch}
    rmax = {(c, h): jnp.max(log_d[c, h], axis=-1, keepdims=True) for c, h in ch}
    m_all = m_scr[...]
    m_prev = [m_all[:, h:h + 1] for h in hs]
    inter, m_t = {}, {}
    for c, h in ch:
        inter[c, h] = bc[c, h] + m_prev[h]
        m_t[c, h] = jnp.maximum(inter[c, h], rmax[c, h])
        m_prev[h] = m_t[c, h][T - 1:T, :]
    qk = [qk_ref[rows[c], :] for c in cs]
    vv = [v_ref[rows[c], :] for c in cs]
    q = {(c, h): (qk[c][:, h * 64:(h + 1) * 64] * 0.125).astype(BF16) for c, h in ch}
    k = {(c, h): qk[c][:, 512 + h * 64:512 + (h + 1) * 64] for c, h in ch}
    kb = {(c, h): k[c, h].astype(BF16) for c, h in ch}
    v = {(c, h): vv[c][:, h * LANES:(h + 1) * LANES] for c, h in ch}
    qkt = {(c, h): lax.dot_general(q[c, h], kb[c, h], NT, preferred_element_type=F32) for c, h in ch}
    w_inter = {(c, h): jnp.exp(inter[c, h] - m_t[c, h]) for c, h in ch}
    sc = {(c, h): qkt[c, h] * jnp.exp(log_d[c, h] - m_t[c, h]) for c, h in ch}
    sv = {(c, h): jnp.dot(sc[c, h].astype(BF16), v[c, h].astype(BF16), preferred_element_type=F32) for c, h in ch}
    den0 = {(c, h): jnp.sum(sc[c, h], axis=-1, keepdims=True) for c, h in ch}
    w_end = {(c, h): jnp.exp(bc[c, h][T - 1:T, :] - bc[c, h] + g[c][:, h:h + 1] - m_t[c, h][T - 1:T, :])
             for c, h in ch}
    vk = {(c, h): lax.dot_general((w_end[c, h] * v[c, h]).astype(BF16), kb[c, h], TN, preferred_element_type=F32)
          for c, h in ch}
    kw = {(c, h): jnp.sum(w_end[c, h] * k[c, h], axis=0, keepdims=True) for c, h in ch}
    c_in = {(0, h): c_scr[h] for h in hs}
    n_in = {(0, h): n_scr[h:h + 1, :] for h in hs}
    for c, h in ch:
        decay = w_inter[c, h][T - 1:T, :]
        c_in[c + 1, h] = decay * c_in[c, h] + vk[c, h]
        n_in[c + 1, h] = decay * n_in[c, h] + kw[c, h]
    qc = {(c, h): lax.dot_general(q[c, h], c_in[c, h].astype(BF16), NT, preferred_element_type=F32) for c, h in ch}
    for c, h in ch:
        num = sv[c, h] + w_inter[c, h] * qc[c, h]
        n_b = n_in[c, h].astype(BF16).astype(F32)
        den = den0[c, h] + w_inter[c, h] * jnp.sum(q[c, h].astype(F32) * n_b, axis=-1, keepdims=True)
        hh = num / jnp.maximum(jnp.abs(den), jnp.exp(-m_t[c, h]))
        sl = slice(h * LANES, (h + 1) * LANES)
        mix_ref[rows[c], sl] = _sigmoid(og_ref[rows[c], sl]) * (_rms(hh) * ng_ref[:, sl])
    for h in hs:
        c_scr[h] = c_in[nc, h]
        n_scr[h:h + 1, :] = n_in[nc, h]
        m_scr[:, h:h + 1] = m_prev[h]

    @pl.when(s % sps == sps - 1)
    def _():
        co_ref[...] = c_scr[...]
        no_ref[...] = n_scr[...]
        mo_ref[...] = m_scr[...]


def mlstm(u, g_col, b_gate, norm_g, c0, n0, m0, nseq, seqlen, T, rows_per_step, n_valid, row0=0):
    m = nseq * seqlen
    nc = rows_per_step // T
    sps = seqlen // rows_per_step
    gates = u
    g0 = g_col * LANES
    gt = jnp.transpose(u[row0:row0 + m, g0:g0 + 2 * H].reshape(m // T, T, 2 * H), (0, 2, 1))
    bgc = jnp.zeros((1, LANES), F32).at[0, :2 * H].set(b_gate.reshape(-1))
    bgr = b_gate.reshape(2 * H, 1)
    r = rows_per_step
    off = row0 // r
    kern = functools.partial(_mlstm_kernel, T=T, nc=nc, n_valid=n_valid, sps=sps)
    in_specs = [pl.BlockSpec((r, D), lambda s: (s + off, 0)),
                pl.BlockSpec((r, D), lambda s: (s + off, 1)),
                pl.BlockSpec((r, D), lambda s: (s + off, 2)),
                pl.BlockSpec((r, LANES), lambda s: (s + off, g_col)),
                pl.BlockSpec((nc, 2 * H, T), lambda s: (s, 0, 0)),
                pl.BlockSpec((1, LANES), lambda s: (0, 0)),
                pl.BlockSpec((2 * H, 1), lambda s: (0, 0)),
                pl.BlockSpec((1, D), lambda s: (0, 0)),
                pl.BlockSpec((None, H, 128, 64), lambda s: (s // sps, 0, 0, 0)),
                pl.BlockSpec((None, H, 64), lambda s: (s // sps, 0, 0)),
                pl.BlockSpec((None, 1, H), lambda s: (s // sps, 0, 0))]
    args = [u, u, u, gates, gt, bgc, bgr, norm_g.reshape(1, D), c0, n0, m0.reshape(nseq, 1, H)]
    return pl.pallas_call(
        kern,
        grid=(m // r,),
        in_specs=in_specs,
        out_specs=[pl.BlockSpec((r, D), lambda s: (s, 0)),
                   pl.BlockSpec((None, H, 128, 64), lambda s: (s // sps, 0, 0, 0)),
                   pl.BlockSpec((None, H, 64), lambda s: (s // sps, 0, 0)),
                   pl.BlockSpec((None, 1, H), lambda s: (s // sps, 0, 0))],
        out_shape=[jax.ShapeDtypeStruct((m, D), F32),
                   jax.ShapeDtypeStruct((nseq, H, 128, 64), F32),
                   jax.ShapeDtypeStruct((nseq, H, 64), F32),
                   jax.ShapeDtypeStruct((nseq, 1, H), F32)],
        scratch_shapes=[pltpu.VMEM((H, 128, 64), F32), pltpu.VMEM((H, 64), F32), pltpu.VMEM((1, H), F32)],
        compiler_params=_cp(1),
        name="mlstm",
    )(*args)


def _gdn_kernel(x_ref, z_ref, gc_ref, gt_ref, cw_ref, pc_ref, pr_ref, ng_ref, s0_ref, cv0_ref,
                mix_ref, so_ref, xbuf, cv_scr, s_scr, *, T, nc, n_valid, sps):
    s = pl.program_id(0)
    R = T * nc

    @pl.when(s % sps == 0)
    def _():
        s_scr[...] = s0_ref[...]
        xbuf[5:8, :] = cv0_ref[...]

    xbuf[8:8 + R, :] = x_ref[...]
    conv = (cw_ref[0:1, :] * xbuf[5:5 + R, :] + cw_ref[1:2, :] * xbuf[6:6 + R, :]
            + cw_ref[2:3, :] * xbuf[7:7 + R, :] + cw_ref[3:4, :] * xbuf[8:8 + R, :])
    cv_scr[...] = _silu(conv)
    xbuf[0:8, :] = xbuf[R:R + 8, :]

    row = lax.broadcasted_iota(jnp.int32, (T, T), 0)
    col = lax.broadcasted_iota(jnp.int32, (T, T), 1)
    causal = col <= row
    strict = col < row
    tril = causal.astype(F32)
    triu = (row <= col).astype(F32)

    def chunk(c, carry):
        r0 = pl.multiple_of(c * T, T)
        rows = pl.ds(r0, T)
        gpre = gc_ref[rows, :]
        beta_all = _sigmoid(gpre)
        g_all = -jnp.exp(pc_ref[0:1, :]) * _softplus(gpre + pc_ref[1:2, :])
        gt = gt_ref[c]
        gr_all = -jnp.exp(pr_ref[:, 0:1]) * _softplus(gt + pr_ref[:, 1:2])
        gr = gr_all[H:2 * H, :]
        if n_valid < T:
            rid = lax.broadcasted_iota(jnp.int32, (T, LANES), 0)
            beta_all = jnp.where(rid < n_valid, beta_all, 0.0)
            g_all = jnp.where(rid < n_valid, g_all, 0.0)
            cid = lax.broadcasted_iota(jnp.int32, (H, T), 1)
            gr = jnp.where(cid < n_valid, gr, 0.0)
        gamc_all = _dot_hi(tril, g_all)
        gamr_all = _dot_hi(gr, triu)
        hs = range(H)
        qf = [cv_scr[rows, h * LANES:(h + 1) * LANES] for h in hs]
        kf = [cv_scr[rows, D + h * LANES:D + (h + 1) * LANES] for h in hs]
        vf = [cv_scr[rows, 2 * D + h * LANES:2 * D + (h + 1) * LANES] for h in hs]
        qb = [(qf[h] * lax.rsqrt(jnp.sum(qf[h] * qf[h], axis=-1, keepdims=True) + EPS) * LANES ** -0.5).astype(BF16)
              for h in hs]
        kn = [kf[h] * lax.rsqrt(jnp.sum(kf[h] * kf[h], axis=-1, keepdims=True) + EPS) for h in hs]
        kb = [kn[h].astype(BF16) for h in hs]
        beta = [beta_all[:, h:h + 1] for h in hs]
        gamc = [gamc_all[:, H + h:H + h + 1] for h in hs]
        egam = [jnp.exp(gamc[h]) for h in hs]
        dec = [jnp.exp(jnp.where(causal, gamc[h] - gamr_all[h:h + 1, :], NEG)) for h in hs]
        s_old = [s_scr[h] for h in hs]
        sb = [s_old[h].astype(BF16) for h in hs]
        qkk = [lax.dot_general(jnp.concatenate([qb[h], kb[h]], axis=0), kb[h], NT, preferred_element_type=F32)
               for h in hs]
        qs = [lax.dot_general(qb[h], sb[h], NT, preferred_element_type=F32) for h in hs]
        nmat = [jnp.where(strict, beta[h] * qkk[h][T:2 * T] * dec[h], 0.0) for h in hs]
        y = [-nmat[h] for h in hs]
        npow = nmat
        nsp = [_split_bf16(nmat[h]) for h in hs]
        span = 2
        while span < T:
            if T < 16:
                npow = [_dot_hi(npow[h], npow[h]) for h in hs]
                y = [y[h] + npow[h] + _dot_hi(y[h], npow[h]) for h in hs]
            else:
                npow = [_dot_split(nsp[h][0], nsp[h][1], nsp[h][0], nsp[h][1]) for h in hs]
                nsp = [_split_bf16(npow[h]) for h in hs]
                ysp = [_split_bf16(y[h]) for h in hs]
                y = [y[h] + npow[h] + _dot_split(ysp[h][0], ysp[h][1], nsp[h][0], nsp[h][1]) for h in hs]
            span *= 2
        rhs = [jnp.concatenate([beta[h] * vf[h], (beta[h] * egam[h]) * kn[h]], axis=1) for h in hs]
        if T < 16:
            sol = [rhs[h] + _dot_hi(y[h], rhs[h]) for h in hs]
        else:
            sol = [rhs[h] + _dot(y[h], rhs[h]) for h in hs]
        delta = [sol[h][:, 0:LANES] - _dot_nt(sol[h][:, LANES:2 * LANES], sb[h]) for h in hs]
        o = [egam[h] * qs[h] + _dot(qkk[h][0:T] * dec[h], delta[h]) for h in hs]
        g_end = [gamc[h][T - 1:T, :] for h in hs]
        upd = [_dot_tn(jnp.exp(g_end[h] - gamc[h]) * delta[h], kb[h]) for h in hs]
        for h in hs:
            sl = slice(h * LANES, (h + 1) * LANES)
            s_scr[h] = jnp.exp(g_end[h]) * s_old[h] + upd[h]
            mix_ref[rows, sl] = _rms(o[h]) * ng_ref[:, sl] * _silu(z_ref[rows, sl])
        return carry

    lax.fori_loop(0, nc, chunk, 0, unroll=min(nc, 2))

    @pl.when(s % sps == sps - 1)
    def _():
        so_ref[...] = s_scr[...]


def gdn(u, g_col, conv_w, a_log, dt_bias, norm_g, s0, conv0, nseq, seqlen, T, rows_per_step, n_valid, row0=0):
    m = nseq * seqlen
    nc = rows_per_step // T
    sps = seqlen // rows_per_step
    r = rows_per_step
    off = row0 // r
    gates = u
    g0 = g_col * LANES
    gt = jnp.transpose(u[row0:row0 + m, g0:g0 + 2 * H].reshape(m // T, T, 2 * H), (0, 2, 1))
    pc = jnp.zeros((2, LANES), F32).at[0, H:2 * H].set(a_log).at[1, H:2 * H].set(dt_bias)
    pr = jnp.zeros((2 * H, 2), F32).at[H:, 0].set(a_log).at[H:, 1].set(dt_bias)
    kern = functools.partial(_gdn_kernel, T=T, nc=nc, n_valid=n_valid, sps=sps)
    in_specs = [pl.BlockSpec((r, 3 * D), lambda s: (s + off, 0)),
                pl.BlockSpec((r, D), lambda s: (s + off, 3)),
                pl.BlockSpec((r, LANES), lambda s: (s + off, g_col)),
                pl.BlockSpec((nc, 2 * H, T), lambda s: (s, 0, 0)),
                pl.BlockSpec((4, 3 * D), lambda s: (0, 0)),
                pl.BlockSpec((2, LANES), lambda s: (0, 0)),
                pl.BlockSpec((2 * H, 2), lambda s: (0, 0)),
                pl.BlockSpec((1, D), lambda s: (0, 0)),
                pl.BlockSpec((None, H, 128, 128), lambda s: (s // sps, 0, 0, 0)),
                pl.BlockSpec((None, 3, 3 * D), lambda s: (s // sps, 0, 0))]
    args = [u, u, gates, gt, conv_w, pc, pr, norm_g.reshape(1, D), s0, conv0]
    return pl.pallas_call(
        kern,
        grid=(m // r,),
        in_specs=in_specs,
        out_specs=[pl.BlockSpec((r, D), lambda s: (s, 0)),
                   pl.BlockSpec((None, H, 128, 128), lambda s: (s // sps, 0, 0, 0))],
        out_shape=[jax.ShapeDtypeStruct((m, D), F32),
                   jax.ShapeDtypeStruct((nseq, H, 128, 128), F32)],
        scratch_shapes=[pltpu.VMEM((r + 8, 3 * D), F32), pltpu.VMEM((r, 3 * D), F32),
                        pltpu.VMEM((H, 128, 128), F32)],
        compiler_params=_cp(1),
        name="gdn",
    )(*args)


def _bucket(rel):
    n = jnp.maximum(rel, 0)
    nf = jnp.maximum(n, 1).astype(F32)
    large = 16 + (jnp.log(nf / 16) / math.log(128 / 16) * 16).astype(jnp.int32)
    return jnp.where(n < 16, n, jnp.minimum(large, 31))


def _diff_lambda(lw_ref, lam_init):
    lw = lw_ref[...]
    a = jnp.sum(lw[0:1, :] * lw[1:2, :], axis=-1, keepdims=True)
    b = jnp.sum(lw[2:3, :] * lw[3:4, :], axis=-1, keepdims=True)
    return jnp.exp(a) - jnp.exp(b) + lam_init


def _dattn_p_kernel(q_ref, k_ref, v_ref, bn_ref, lw_ref, ng_ref, o_ref, m_scr, acc_scr, *, TB, lam_init):
    i = pl.program_id(2)
    W = 2 * TB
    lane = lax.broadcasted_iota(jnp.int32, (TB, LANES), 1)
    q = q_ref[...] * 0.125
    qq = jnp.concatenate([jnp.where(lane < 64, q, 0.0), jnp.where(lane >= 64, q, 0.0)], axis=0).astype(BF16)
    m_scr[...] = jnp.full_like(m_scr, NEG)
    acc_scr[...] = jnp.zeros_like(acc_scr)

    def step(k0, width, bias):
        rows = pl.ds(pl.multiple_of(k0, TB), width)
        kb = k_ref[rows, :].astype(BF16)
        vext = jnp.concatenate([v_ref[rows, :].astype(BF16), jnp.ones((width, LANES), BF16)], axis=1)
        sc = lax.dot_general(qq, kb, NT, preferred_element_type=F32)
        if bias is not None:
            sc = sc + bias
        ng = width // LANES
        mloc = sc[:, 0:LANES]
        for g in range(1, ng):
            mloc = jnp.maximum(mloc, sc[:, g * LANES:(g + 1) * LANES])
        m_old = m_scr[...]
        m_new = jnp.maximum(m_old, jnp.max(mloc, axis=-1, keepdims=True))
        alpha = jnp.exp(m_old - m_new)
        p = jnp.exp(sc - jnp.concatenate([m_new] * ng, axis=1)).astype(BF16)
        acc_scr[...] = (jnp.concatenate([alpha, alpha], axis=1) * acc_scr[...]
                        + jnp.dot(p, vext, preferred_element_type=F32))
        m_scr[...] = m_new

    n_far = jnp.maximum(i - 1, 0)
    n_big = n_far // 4
    rem = n_far % 4

    def body(c, carry):
        step(c * 2 * W, W, None)
        step(c * 2 * W + W, W, None)
        return carry

    lax.fori_loop(0, n_big, body, 0)

    @pl.when(rem >= 2)
    def _():
        step(n_big * 2 * W, W, None)

    @pl.when(rem % 2 == 1)
    def _():
        step((n_far - 1) * TB, TB, None)

    step(n_far * TB, W, bn_ref[...])
    lam = _diff_lambda(lw_ref, lam_init)
    acc = acc_scr[...]
    o = (acc[0:TB, 0:LANES] / acc[0:TB, LANES:2 * LANES]
         - lam * (acc[TB:2 * TB, 0:LANES] / acc[TB:2 * TB, LANES:2 * LANES]))
    o_ref[...] = _rms(o) * ng_ref[...] * (1.0 - lam_init)


def _bias_table(rel_bias, rel):
    onehot = (_bucket(rel)[..., None] == jnp.arange(32)).astype(F32)
    return jnp.einsum('...k,kh->h...', onehot, rel_bias - rel_bias[31:32], precision=HI)


def diff_attn_prompt(u, lam_w, norm_g, rel_bias, lam_init, TB=256):
    nq = SEQ // TB
    ii = jnp.arange(TB)
    rel_d = ii[:, None] - ii[None, :]
    bd = jnp.where((rel_d >= 0)[None], _bias_table(rel_bias, rel_d), NEG)
    bs = _bias_table(rel_bias, rel_d + TB)
    masked = jnp.full((H, TB, TB), NEG, F32)
    bn = jnp.stack([jnp.concatenate([bd, masked], axis=2), jnp.concatenate([bs, bd], axis=2)], axis=1)
    bn = jnp.concatenate([bn, bn], axis=2)
    kern = functools.partial(_dattn_p_kernel, TB=TB, lam_init=lam_init)
    return pl.pallas_call(
        kern,
        grid=(BATCH, H, nq),
        in_specs=[pl.BlockSpec((TB, LANES), lambda b, h, i: (b * nq + i, h)),
                  pl.BlockSpec((SEQ, LANES), lambda b, h, i: (b, H + h)),
                  pl.BlockSpec((SEQ, LANES), lambda b, h, i: (b, 2 * H + h)),
                  pl.BlockSpec((None, None, 2 * TB, 2 * TB), lambda b, h, i: (h, jnp.minimum(i, 1), 0, 0)),
                  pl.BlockSpec((4, 64), lambda b, h, i: (0, 0)),
                  pl.BlockSpec((1, LANES), lambda b, h, i: (0, h))],
        out_specs=pl.BlockSpec((TB, LANES), lambda b, h, i: (b * nq + i, h)),
        out_shape=jax.ShapeDtypeStruct((BATCH * SEQ, D), F32),
        scratch_shapes=[pltpu.VMEM((2 * TB, LANES), F32), pltpu.VMEM((2 * TB, 2 * LANES), F32)],
        compiler_params=_cp(3),
        name="diff_attn_prompt",
    )(u, u, u, bn, lam_w, norm_g.reshape(1, D))


def _dattn_s_kernel(pt_ref, qm_ref, qb_ref, *refs, lam_init):
    del pt_ref
    kp_refs = refs[0:PAGES_PER_STEP]
    vp_refs = refs[PAGES_PER_STEP:2 * PAGES_PER_STEP]
    kn_ref, vn_ref, bl_ref, bn_ref, lw_ref, ng_ref, o_ref, m_scr, l_scr, acc_scr = refs[2 * PAGES_PER_STEP:]
    p = pl.program_id(1)
    last = pl.num_programs(1) - 1
    rows_h = 2 * DEC_PAD

    @pl.when(p == 0)
    def _():
        m_scr[...] = jnp.full_like(m_scr, NEG)
        l_scr[...] = jnp.zeros_like(l_scr)
        acc_scr[...] = jnp.zeros_like(acc_scr)

    def head_rows(page_refs, h):
        return jnp.concatenate([r[pl.ds(h, PAGE, stride=H), :].astype(BF16) for r in page_refs], axis=0)

    qm = qm_ref[...].astype(BF16)
    sc = jnp.concatenate(
        [lax.dot_general(qm[h * rows_h:(h + 1) * rows_h], head_rows(kp_refs, h), NT, preferred_element_type=F32)
         for h in range(H)], axis=0)
    sc = sc + jnp.where(p == last, bl_ref[...], 0.0)
    m_old = m_scr[...]
    m_new = jnp.maximum(m_old, jnp.max(sc, axis=-1, keepdims=True))
    alpha = jnp.exp(m_old - m_new)
    pr = jnp.exp(sc - jnp.concatenate([m_new] * PAGES_PER_STEP, axis=1))
    l_scr[...] = alpha * l_scr[...] + jnp.sum(pr, axis=-1, keepdims=True)
    prb = pr.astype(BF16)
    pv = jnp.concatenate(
        [jnp.dot(prb[h * rows_h:(h + 1) * rows_h], head_rows(vp_refs, h), preferred_element_type=F32)
         for h in range(H)], axis=0)
    acc_scr[...] = alpha * acc_scr[...] + pv
    m_scr[...] = m_new

    @pl.when(p == last)
    def _():
        scn = lax.dot_general(qb_ref[...], kn_ref[...], NT, preferred_element_type=F32) + bn_ref[...]
        m_o = m_scr[...]
        m_n = jnp.maximum(m_o, jnp.max(scn, axis=-1, keepdims=True))
        al = jnp.exp(m_o - m_n)
        pn = jnp.exp(scn - m_n[:, 0:DEC_PAD])
        l_n = al * l_scr[...] + jnp.sum(pn, axis=-1, keepdims=True)
        pvn = jnp.dot(pn, vn_ref[...], preferred_element_type=F32)
        lam = _diff_lambda(lw_ref, lam_init)
        for h in range(H):
            sl = slice(h * LANES, (h + 1) * LANES)
            r0 = slice(h * rows_h, h * rows_h + DEC_PAD)
            r1 = slice(h * rows_h + DEC_PAD, (h + 1) * rows_h)
            a0 = (al[r0] * acc_scr[r0, :] + pvn[r0, sl]) / l_n[r0]
            a1 = (al[r1] * acc_scr[r1, :] + pvn[r1, sl]) / l_n[r1]
            o_ref[:, sl] = _rms(a0 - lam * a1) * ng_ref[:, sl] * (1.0 - lam_init)


def diff_attn_sample(u, cache_k, cache_v, page_table, lam_w, norm_g, rel_bias, lam_init, row0):
    nb = DEC_BATCH
    nr = 2 * H * DEC_PAD
    off = row0 // DEC_PAD
    q = u[row0:row0 + nb * DEC_PAD, :D].reshape(nb, DEC_PAD, D) * 0.125
    grp_row = jnp.arange(2 * H)[:, None, None]
    grp_col = (jnp.arange(D) // 64)[None, None, :]
    qbig = jnp.where(grp_row == grp_col, q[:, None, :, :], 0.0).reshape(nb, nr, D)
    q4 = jnp.transpose(q.reshape(nb, DEC_PAD, H, 1, LANES), (0, 2, 3, 1, 4))
    comp = (jnp.arange(LANES) // 64)[None, None, None, None, :]
    qm = jnp.where(comp == jnp.arange(2)[None, None, :, None, None], q4, 0.0).reshape(nb, nr, LANES)
    qi = jnp.arange(DEC_PAD)
    rel_last = PAGE + qi[:, None] - jnp.arange(PAGE)[None, :]
    b_last = jnp.broadcast_to(_bias_table(rel_bias, rel_last)[:, None], (H, 2, DEC_PAD, PAGE)).reshape(nr, PAGE)
    b_last = jnp.pad(b_last, ((0, 0), ((PAGES_PER_STEP - 1) * PAGE, 0)))
    rel_new = qi[:, None] - qi[None, :]
    vis = (rel_new >= 0) & (qi[None, :] < DEC_SEQ)
    b_new = jnp.where(vis[None], _bias_table(rel_bias, rel_new), NEG)
    b_new = jnp.broadcast_to(b_new[:, None], (H, 2, DEC_PAD, DEC_PAD)).reshape(nr, DEC_PAD)
    kern = functools.partial(_dattn_s_kernel, lam_init=lam_init)
    def page_spec(slot):
        return pl.BlockSpec((None, PAGE * H, LANES), lambda b, p, pt: (pt[b, p * PAGES_PER_STEP + slot], 0, 0))

    pages = [page_spec(slot) for slot in range(PAGES_PER_STEP)]
    in_specs = [pl.BlockSpec((None, nr, LANES), lambda b, p, pt: (b, 0, 0)),
                pl.BlockSpec((None, nr, D), lambda b, p, pt: (b, 0, 0))] + pages + pages + [
                pl.BlockSpec((DEC_PAD, D), lambda b, p, pt: (b + off, 1)),
                pl.BlockSpec((DEC_PAD, D), lambda b, p, pt: (b + off, 2)),
                pl.BlockSpec((nr, PAGES_PER_STEP * PAGE), lambda b, p, pt: (0, 0)),
                pl.BlockSpec((nr, DEC_PAD), lambda b, p, pt: (0, 0)),
                pl.BlockSpec((4, 64), lambda b, p, pt: (0, 0)),
                pl.BlockSpec((1, D), lambda b, p, pt: (0, 0))]
    args = [qm, qbig, *([cache_k] * PAGES_PER_STEP), *([cache_v] * PAGES_PER_STEP), u, u, b_last, b_new,
            lam_w, norm_g.reshape(1, D)]
    grid_spec = pltpu.PrefetchScalarGridSpec(
        num_scalar_prefetch=1,
        grid=(nb, N_PAGES // PAGES_PER_STEP),
        in_specs=in_specs,
        out_specs=pl.BlockSpec((DEC_PAD, D), lambda b, p, pt: (b, 0)),
        scratch_shapes=[pltpu.VMEM((nr, LANES), F32), pltpu.VMEM((nr, LANES), F32), pltpu.VMEM((nr, LANES), F32)],
    )
    return pl.pallas_call(
        kern,
        grid_spec=grid_spec,
        out_shape=jax.ShapeDtypeStruct((nb * DEC_PAD, D), F32),
        compiler_params=_cp(2),
        name="diff_attn_sample",
    )(page_table, *args)


def _router_kernel(x_ref, w_ref, o_ref):
    logits = _dot(x_ref[...], w_ref[...])
    lane = lax.broadcasted_iota(jnp.int32, logits.shape, 1)
    logits = jnp.where(lane < N_EXPERTS, logits, -jnp.inf)
    m1 = jnp.max(logits, axis=-1, keepdims=True)
    i1 = jnp.min(jnp.where(logits == m1, lane, LANES), axis=-1, keepdims=True)
    rest = jnp.where(lane == i1, -jnp.inf, logits)
    m2 = jnp.max(rest, axis=-1, keepdims=True)
    i2 = jnp.min(jnp.where(rest == m2, lane, LANES), axis=-1, keepdims=True)
    e2 = jnp.exp(m2 - m1)
    g1 = 1.0 / (1.0 + e2)
    g2 = e2 / (1.0 + e2)
    o_ref[...] = jnp.where(lane == 0, i1.astype(F32),
                           jnp.where(lane == 1, i2.astype(F32),
                                     jnp.where(lane == 2, g1, jnp.where(lane == 3, g2, 0.0))))


def router(x, w_router, tm):
    m = x.shape[0]
    w = jnp.pad(w_router, ((0, 0), (0, LANES - N_EXPERTS)))
    return pl.pallas_call(
        _router_kernel,
        grid=(m // tm,),
        in_specs=[pl.BlockSpec((tm, D), lambda i: (i, 0)),
                  pl.BlockSpec((D, LANES), lambda i: (0, 0))],
        out_specs=pl.BlockSpec((tm, LANES), lambda i: (i, 0)),
        out_shape=jax.ShapeDtypeStruct((m, LANES), F32),
        compiler_params=_cp(1),
        name="router",
    )(x, w)


def _expert_kernel(be_ref, src_ref, x_hbm, wgu_ref, wd_ref, o_ref, xbuf, sem, *, tf, tm):
    del be_ref
    i = pl.program_id(0)
    nb = pl.num_programs(0)
    slot = i % 2
    nslot = 1 - slot
    nxt = jnp.minimum(i + 1, nb - 1)

    def row_copy(block, r, sl):
        return pltpu.make_async_copy(x_hbm.at[pl.ds(src_ref[block * tm + r], 1), :],
                                     xbuf.at[sl, pl.ds(r, 1), :], sem.at[sl])

    def wait_rows(sl):
        def drain(r, carry):
            row_copy(0, r, sl).wait()
            return carry
        lax.fori_loop(0, tm, drain, 0, unroll=8)

    @pl.when(i == 0)
    def _():
        def issue(r, carry):
            row_copy(0, r, 0).start()
            return carry
        lax.fori_loop(0, tm, issue, 0, unroll=8)

    wait_rows(slot)
    xb = xbuf[slot].astype(BF16)
    chunks = list(range(0, D_FF_EXPERT, tf))
    per = -(-tm // len(chunks))
    for ci, f0 in enumerate(chunks):
        gate = jnp.dot(xb, wgu_ref[:, f0:f0 + tf], preferred_element_type=F32)
        up = jnp.dot(xb, wgu_ref[:, D_FF_EXPERT + f0:D_FF_EXPERT + f0 + tf], preferred_element_type=F32)
        part = jnp.dot((_silu(gate) * up).astype(BF16), wd_ref[f0:f0 + tf, :], preferred_element_type=F32)
        if ci == 0:
            o_ref[...] = part
        else:
            o_ref[...] += part
        for r in range(ci * per, min((ci + 1) * per, tm)):
            row_copy(nxt, r, nslot).start()

    @pl.when(i == nb - 1)
    def _():
        wait_rows(nslot)


def expert_ffn(x, src, w_gu, w_down, layer, blk_e, tm, tf=512):
    n = src.shape[0]
    grid_spec = pltpu.PrefetchScalarGridSpec(
        num_scalar_prefetch=2,
        grid=(n // tm,),
        in_specs=[pl.BlockSpec(memory_space=pl.ANY),
                  pl.BlockSpec((None, None, D, 2 * D_FF_EXPERT), lambda i, be, sr: (layer, be[i], 0, 0),
                               pipeline_mode=pl.Buffered(1)),
                  pl.BlockSpec((None, None, D_FF_EXPERT, D), lambda i, be, sr: (layer, be[i], 0, 0),
                               pipeline_mode=pl.Buffered(1))],
        out_specs=pl.BlockSpec((tm, D), lambda i, be, sr: (i, 0)),
        scratch_shapes=[pltpu.VMEM((2, tm, D), F32), pltpu.SemaphoreType.DMA((2,))],
    )
    return pl.pallas_call(
        functools.partial(_expert_kernel, tf=tf, tm=tm),
        grid_spec=grid_spec,
        out_shape=jax.ShapeDtypeStruct((n, D), F32),
        compiler_params=_cp(1),
        name="expert_ffn",
    )(blk_e, src, x, w_gu, w_down)


def _combine_kernel(d0_ref, d1_ref, yb_hbm, x_ref, rt_ref, g_ref, b_ref, o_ref, buf0, buf1, sem, *, G):
    base = pl.program_id(0) * G

    def issue(r, carry):
        pltpu.make_async_copy(yb_hbm.at[pl.ds(d0_ref[base + r], 1), :], buf0.at[pl.ds(r, 1), :], sem).start()
        pltpu.make_async_copy(yb_hbm.at[pl.ds(d1_ref[base + r], 1), :], buf1.at[pl.ds(r, 1), :], sem).start()
        return carry

    lax.fori_loop(0, G, issue, 0, unroll=4)

    def drain(r, carry):
        pltpu.make_async_copy(yb_hbm.at[pl.ds(0, 1), :], buf0.at[pl.ds(r, 1), :], sem).wait()
        pltpu.make_async_copy(yb_hbm.at[pl.ds(0, 1), :], buf1.at[pl.ds(r, 1), :], sem).wait()
        return carry

    lax.fori_loop(0, G, drain, 0, unroll=4)
    rt = rt_ref[...]
    y = rt[:, 2:3] * buf0[...] + rt[:, 3:4] * buf1[...]
    o_ref[...] = _layer_norm(ALPHA * x_ref[...] + y, g_ref[...], b_ref[...])


def combine_ln(yb, x, route, d0, d1, g, b, G):
    m = x.shape[0]
    grid_spec = pltpu.PrefetchScalarGridSpec(
        num_scalar_prefetch=2,
        grid=(m // G,),
        in_specs=[pl.BlockSpec(memory_space=pl.ANY),
                  pl.BlockSpec((G, D), lambda i, a, c: (i, 0)),
                  pl.BlockSpec((G, LANES), lambda i, a, c: (i, 0)),
                  pl.BlockSpec((1, D), lambda i, a, c: (0, 0)),
                  pl.BlockSpec((1, D), lambda i, a, c: (0, 0))],
        out_specs=pl.BlockSpec((G, D), lambda i, a, c: (i, 0)),
        scratch_shapes=[pltpu.VMEM((G, D), F32), pltpu.VMEM((G, D), F32), pltpu.SemaphoreType.DMA(())],
    )
    return pl.pallas_call(
        functools.partial(_combine_kernel, G=G),
        grid_spec=grid_spec,
        out_shape=jax.ShapeDtypeStruct((m, D), F32),
        compiler_params=_cp(1),
        name="combine_ln",
    )(d0, d1, yb, x, route, g.reshape(1, D), b.reshape(1, D))


def moe_ln(x, w_router, w_gu, w_down, layer, g, b, tm=256):
    m = x.shape[0]
    route = router(x, w_router, 640)
    flat_e = route[:, :2].astype(jnp.int32).reshape(-1)
    onehot = (flat_e[:, None] == jnp.arange(N_EXPERTS)[None, :]).astype(jnp.int32)
    csum = jnp.cumsum(onehot, axis=0)
    counts = csum[-1]
    padded = (counts + tm - 1) // tm * tm
    p_end = jnp.cumsum(padded)
    dest = jnp.sum(onehot * ((p_end - padded)[None, :] + csum - 1), axis=1)
    n_blocks = -(-(2 * m + N_EXPERTS * (tm - 1)) // tm)
    n_rows = n_blocks * tm
    n_used = (p_end[-1] // tm).astype(jnp.int32).reshape(1)
    blk = jnp.minimum(jnp.arange(n_blocks), n_used[0] - 1) * tm
    blk_e = jnp.minimum(jnp.sum((blk[:, None] >= p_end[None, :]).astype(jnp.int32), axis=1), N_EXPERTS - 1)
    src = jnp.zeros((n_rows,), jnp.int32).at[dest].set(jnp.arange(2 * m, dtype=jnp.int32) // 2)
    yb = expert_ffn(x, src, w_gu, w_down, layer, blk_e, tm)
    d = dest.reshape(m, 2).astype(jnp.int32)
    return combine_ln(yb, x, route, d[:, 0], d[:, 1], g, b, 640)


def kernel(x_prompt, x_sample, mem_prompt, page_table, cache_diff_k, cache_diff_v, cache_mem_k, cache_mem_v,
           state_mlstm_C, state_mlstm_n, state_mlstm_m, state_gdn_S, state_gdn_conv, rel_bias,
           w_in_a, b_gate_a, norm_a, w_in_b, lambda_b, norm_b, w_in_c, conv_c, a_log_c, dt_bias_c, norm_c,
           w_mem_kv, w_out, ln_g, ln_b, w_ffn_gu, w_ffn_down, w_router, w_exp_gu, w_exp_down):
    mp = BATCH * SEQ
    ms = DEC_BATCH * DEC_PAD
    tm = ROW_TILE
    x = jnp.concatenate([x_prompt.reshape(mp, D),
                         jnp.pad(x_sample, ((0, 0), (0, DEC_PAD - DEC_SEQ), (0, 0))).reshape(ms, D)], axis=0)
    mem2 = mem_prompt.reshape(BATCH * MEM_TOKENS, D)
    n_phys = cache_diff_k.shape[1]
    w_exp_gu_b = w_exp_gu.astype(BF16)
    w_exp_down_b = w_exp_down.astype(BF16)
    cmk = cache_mem_k.reshape(DEPTH * DEC_BATCH, MEM_TOKENS * MEM_HEADS, LANES)
    cmv = cache_mem_v.reshape(DEPTH * DEC_BATCH, MEM_TOKENS * MEM_HEADS, LANES)

    def sample_rows(t):
        return t[mp:].reshape(DEC_BATCH, DEC_PAD, -1)[:, :DEC_SEQ]

    def pad_gates(w):
        return jnp.pad(w, ((0, 0), (0, LANES - w.shape[1])))

    mlstm_p, mlstm_s, diff_p, diff_s, gdn_p, gdn_s, mem_k_new, mem_v_new = [], [], [], [], [], [], [], []
    for i in range(DEPTH):
        kind, j = i % 3, i // 3
        mem_kv = matmul(mem2, w_mem_kv[i], MEM_TOKENS, 512).reshape(BATCH, MEM_TOKENS, 2 * MEM_Q)
        mem_k_new.append(mem_kv[:, :, :MEM_Q].reshape(BATCH, MEM_TOKENS, MEM_HEADS, 128))
        mem_v_new.append(mem_kv[:, :, MEM_Q:].reshape(BATCH, MEM_TOKENS, MEM_HEADS, 128))
        if kind == 0:
            w = w_in_a[j]
            w_main = jnp.concatenate([w[:, :3 * D], w[:, 3 * D + 2 * H:], pad_gates(w[:, 3 * D:3 * D + 2 * H])], axis=1)
            u = matmul(x, w_main, tm, 512)
            gts = (3 * D + MEM_Q) // LANES
            zc = jnp.zeros((BATCH, H, 128, 64), F32)
            zn = jnp.zeros((BATCH, H, 64), F32)
            zm = jnp.zeros((BATCH, H), F32)
            mix_p, c_p, n_p, m_p = mlstm(u, gts, b_gate_a[j], norm_a[j], zc, zn, zm, BATCH, SEQ, CHUNK, 256, CHUNK)
            mix_s, c_s, n_s, m_s = mlstm(u, gts, b_gate_a[j], norm_a[j], state_mlstm_C[j], state_mlstm_n[j],
                                         state_mlstm_m[j], DEC_BATCH, DEC_PAD, DEC_PAD, DEC_PAD, DEC_SEQ, row0=mp)
            mlstm_p.append((c_p, n_p, m_p.reshape(BATCH, H)))
            mlstm_s.append((c_s, n_s, m_s.reshape(DEC_BATCH, H)))
            mq_col = 6
        elif kind == 1:
            lam_init = 0.8 - 0.6 * math.exp(-0.3 * i)
            u = matmul(x, w_in_b[j], tm, 512)
            mix_p = diff_attn_prompt(u, lambda_b[j], norm_b[j], rel_bias, lam_init)
            mix_s = diff_attn_sample(u, cache_diff_k[j].reshape(n_phys, PAGE * H, LANES),
                                     cache_diff_v[j].reshape(n_phys, PAGE * H, LANES), page_table,
                                     lambda_b[j], norm_b[j], rel_bias, lam_init, row0=mp)
            k_rows, v_rows = kv_head_rows(u, mp, 256)
            diff_p.append((k_rows.reshape(BATCH, SEQ, H, 128), v_rows.reshape(BATCH, SEQ, H, 128)))
            u_s = sample_rows(u)
            diff_s.append((u_s[:, :, D:2 * D].reshape(DEC_BATCH, DEC_SEQ, H, 128),
                           u_s[:, :, 2 * D:3 * D].reshape(DEC_BATCH, DEC_SEQ, H, 128)))
            mq_col = 6
        else:
            w = w_in_c[j]
            w_main = jnp.concatenate([w[:, :4 * D], w[:, 4 * D + 2 * H:], pad_gates(w[:, 4 * D:4 * D + 2 * H])], axis=1)
            u = matmul(x, w_main, tm, 512)
            gts = (4 * D + MEM_Q) // LANES
            zs = jnp.zeros((BATCH, H, 128, 128), F32)
            zv = jnp.zeros((BATCH, 3, 3 * D), F32)
            mix_p, s_p = gdn(u, gts, conv_c[j], a_log_c[j], dt_bias_c[j], norm_c[j], zs, zv,
                             BATCH, SEQ, CHUNK, 256, CHUNK)
            mix_s, s_s = gdn(u, gts, conv_c[j], a_log_c[j], dt_bias_c[j], norm_c[j], state_gdn_S[j],
                             state_gdn_conv[j], DEC_BATCH, DEC_PAD, DEC_PAD, DEC_PAD, DEC_SEQ, row0=mp)
            conv_p = jnp.stack([u[(b + 1) * SEQ - 3:(b + 1) * SEQ, :3 * D] for b in range(BATCH)])
            conv_s = sample_rows(u)[:, DEC_SEQ - 3:, :3 * D]
            gdn_p.append((s_p, conv_p))
            gdn_s.append((s_s, conv_s))
            mq_col = 8
        mo_s = mem_attention(u, mq_col, cmk, 0, cmv, 0, DEC_PAD, DEC_BATCH, DEC_PAD, row0=mp, seq0=i * DEC_BATCH,
                             head_rows=True)
        x = outproj_ln(mix_p, u, mq_col, mem_kv, mix_s, mo_s, x, w_out[i], ln_g[i, 0], ln_b[i, 0])
        k_f = i // 2
        if i % 2 == 0:
            x = ffn_ln(x, w_ffn_gu[k_f], w_ffn_down[k_f], ln_g[i, 1], ln_b[i, 1], tm)
        else:
            x = moe_ln(x, w_router[k_f], w_exp_gu_b, w_exp_down_b, k_f, ln_g[i, 1], ln_b[i, 1])

    def stack(lst, k):
        return jnp.stack([t[k] for t in lst])

    return (x[:mp].reshape(BATCH, SEQ, D), sample_rows(x),
            stack(mlstm_p, 0), stack(mlstm_p, 1), stack(mlstm_p, 2),
            stack(mlstm_s, 0), stack(mlstm_s, 1), stack(mlstm_s, 2),
            stack(diff_p, 0), stack(diff_p, 1), stack(diff_s, 0), stack(diff_s, 1),
            stack(gdn_p, 0), stack(gdn_p, 1), stack(gdn_s, 0), stack(gdn_s, 1),
            jnp.stack(mem_k_new), jnp.stack(mem_v_new))
```

```python
import functools
import math

import jax
import jax.numpy as jnp
from jax import lax
from jax.experimental import pallas as pl
from jax.experimental.pallas import tpu as pltpu

F32 = jnp.float32
BF16 = jnp.bfloat16
HI = lax.Precision.HIGHEST
NT = (((1,), (1,)), ((), ()))
TN = (((0,), (0,)), ((), ()))

D = 1024
H = 8
DEPTH = 4
SEQ = 4096
BATCH = 4
DEC_BATCH = 32
DEC_SEQ = 4
DEC_PAD = 8
PAST_LEN = 8192
PAGE = 128
N_PAGES = PAST_LEN // PAGE
PAGES_PER_STEP = 16
ROW_TILE = 640
MEM_TOKENS = 256
MEM_Q = 512
MEM_HEADS = 4
CHUNK = 64
N_EXPERTS = 8
D_FF = 2752
D_FF_PAD = 2816
D_FF_EXPERT = 3584
EPS = 1e-6
ALPHA = (2 * DEPTH) ** 0.25
NEG = -1e30
LANES = 128
VMEM_LIMIT = 56 * 1024 * 1024


def _cp(n_axes, vmem=VMEM_LIMIT):
    return pltpu.CompilerParams(dimension_semantics=("arbitrary",) * n_axes, vmem_limit_bytes=vmem)


def _dot(a, b):
    return jnp.dot(a.astype(BF16), b.astype(BF16), preferred_element_type=F32)


def _dot_nt(a, b):
    return lax.dot_general(a.astype(BF16), b.astype(BF16), NT, preferred_element_type=F32)


def _dot_tn(a, b):
    return lax.dot_general(a.astype(BF16), b.astype(BF16), TN, preferred_element_type=F32)


def _dot_hi(a, b):
    return jnp.dot(a, b, preferred_element_type=F32, precision=HI)


def _split_bf16(a):
    hi = a.astype(BF16)
    return hi, (a - hi.astype(F32)).astype(BF16)


def _dot_split(a_hi, a_lo, b_hi, b_lo):
    return (jnp.dot(a_hi, b_hi, preferred_element_type=F32)
            + (jnp.dot(a_lo, b_hi, preferred_element_type=F32) + jnp.dot(a_hi, b_lo, preferred_element_type=F32)))


def _sigmoid(x):
    return 1.0 / (1.0 + jnp.exp(-x))


def _silu(x):
    return x * _sigmoid(x)


def _softplus(x):
    return jnp.maximum(x, 0.0) + jnp.log1p(jnp.exp(-jnp.abs(x)))


def _log_sigmoid(x):
    return -_softplus(-x)


def _layer_norm(v, g, b):
    mu = jnp.mean(v, axis=-1, keepdims=True)
    d = v - mu
    var = jnp.mean(d * d, axis=-1, keepdims=True)
    return d * lax.rsqrt(var + EPS) * g + b


def _rms(h):
    return h * lax.rsqrt(jnp.mean(h * h, axis=-1, keepdims=True) + EPS)


def _mm_kernel(x_ref, w_ref, o_ref, *, tn):
    xb = x_ref[...].astype(BF16)
    n = o_ref.shape[1]
    for n0 in range(0, n, tn):
        n1 = min(n0 + tn, n)
        o_ref[:, n0:n1] = jnp.dot(xb, w_ref[:, n0:n1], preferred_element_type=F32)


def matmul(x, w, tm, tn):
    m, k = x.shape
    n = w.shape[1]
    return pl.pallas_call(
        functools.partial(_mm_kernel, tn=min(tn, n)),
        grid=(m // tm,),
        in_specs=[pl.BlockSpec((tm, k), lambda i: (i, 0)),
                  pl.BlockSpec((k, n), lambda i: (0, 0), pipeline_mode=pl.Buffered(1))],
        out_specs=pl.BlockSpec((tm, n), lambda i: (i, 0)),
        out_shape=jax.ShapeDtypeStruct((m, n), F32),
        compiler_params=_cp(1),
        name="matmul",
    )(x, w.astype(BF16))


def _headrows_kernel(k_ref, v_ref, ko_ref, vo_ref):
    tm = k_ref.shape[0]
    for h in range(H):
        sl = slice(h * LANES, (h + 1) * LANES)
        ko_ref[pl.ds(h, tm, stride=H), :] = k_ref[:, sl]
        vo_ref[pl.ds(h, tm, stride=H), :] = v_ref[:, sl]


def kv_head_rows(u, m, tm):
    return pl.pallas_call(
        _headrows_kernel,
        grid=(m // tm,),
        in_specs=[pl.BlockSpec((tm, D), lambda i: (i, 1)),
                  pl.BlockSpec((tm, D), lambda i: (i, 2))],
        out_specs=[pl.BlockSpec((tm * H, LANES), lambda i: (i, 0)),
                   pl.BlockSpec((tm * H, LANES), lambda i: (i, 0))],
        out_shape=[jax.ShapeDtypeStruct((m * H, LANES), F32), jax.ShapeDtypeStruct((m * H, LANES), F32)],
        compiler_params=_cp(1),
        name="kv_head_rows",
    )(u, u)


def _outproj_kernel(mixp_ref, q_ref, k_ref, v_ref, mixs_ref, mos_ref, x_ref, w1_ref, w2_ref, g_ref, b_ref, o_ref,
                    mo_scr, *, n_prompt):
    is_p = pl.program_id(0) < n_prompt

    @pl.when(is_p)
    def _():
        q = q_ref[...]
        for h in range(MEM_HEADS):
            sl = slice(h * LANES, (h + 1) * LANES)
            s = _dot_nt(q[:, sl], k_ref[:, sl]) * LANES ** -0.5
            s = s - jnp.max(s, axis=-1, keepdims=True)
            p = jnp.exp(s)
            p = p / jnp.sum(p, axis=-1, keepdims=True)
            mo_scr[:, sl] = _dot(p, v_ref[:, sl])

    @pl.when(jnp.logical_not(is_p))
    def _():
        mo_scr[...] = mos_ref[...]

    mix = jnp.where(is_p, mixp_ref[...], mixs_ref[...])
    y = _dot(mix, w1_ref[...]) + _dot(mo_scr[...], w2_ref[...])
    o_ref[...] = _layer_norm(ALPHA * x_ref[...] + y, g_ref[...], b_ref[...])


def outproj_ln(mix_p, u, q_col, mem_kv, mix_s, mo_s, x, w_out, g, b):
    m = x.shape[0]
    tm = mix_s.shape[0]
    n_prompt = mix_p.shape[0] // tm
    per_seq = SEQ // tm
    w1 = w_out[:D].astype(BF16)
    w2 = w_out[D:].astype(BF16)

    def prow(i):
        return jnp.minimum(i, n_prompt - 1)

    return pl.pallas_call(
        functools.partial(_outproj_kernel, n_prompt=n_prompt),
        grid=(m // tm,),
        in_specs=[pl.BlockSpec((tm, D), lambda i: (prow(i), 0)),
                  pl.BlockSpec((tm, MEM_Q), lambda i: (prow(i), q_col)),
                  pl.BlockSpec((None, MEM_TOKENS, MEM_Q), lambda i: (prow(i) // per_seq, 0, 0)),
                  pl.BlockSpec((None, MEM_TOKENS, MEM_Q), lambda i: (prow(i) // per_seq, 0, 1)),
                  pl.BlockSpec((tm, D), lambda i: (0, 0)),
                  pl.BlockSpec((tm, MEM_Q), lambda i: (0, 0)),
                  pl.BlockSpec((tm, D), lambda i: (i, 0)),
                  pl.BlockSpec((D, D), lambda i: (0, 0)),
                  pl.BlockSpec((MEM_Q, D), lambda i: (0, 0)),
                  pl.BlockSpec((1, D), lambda i: (0, 0)),
                  pl.BlockSpec((1, D), lambda i: (0, 0))],
        out_specs=pl.BlockSpec((tm, D), lambda i: (i, 0)),
        out_shape=jax.ShapeDtypeStruct((m, D), F32),
        scratch_shapes=[pltpu.VMEM((tm, MEM_Q), F32)],
        compiler_params=_cp(1),
        name="outproj_ln",
    )(mix_p, u, mem_kv, mem_kv, mix_s, mo_s, x, w1, w2, g.reshape(1, D), b.reshape(1, D))


def _ffn_kernel(x_ref, wgu_ref, wd_ref, g_ref, b_ref, o_ref, *, tf):
    x = x_ref[...]
    xb = x.astype(BF16)
    for f0 in range(0, D_FF_PAD, tf):
        gate = jnp.dot(xb, wgu_ref[:, f0:f0 + tf], preferred_element_type=F32)
        up = jnp.dot(xb, wgu_ref[:, D_FF_PAD + f0:D_FF_PAD + f0 + tf], preferred_element_type=F32)
        part = jnp.dot((_silu(gate) * up).astype(BF16), wd_ref[f0:f0 + tf, :], preferred_element_type=F32)
        if f0 == 0:
            o_ref[...] = part
        else:
            o_ref[...] += part
    o_ref[...] = _layer_norm(ALPHA * x + o_ref[...], g_ref[...], b_ref[...])


def ffn_ln(x, w_gu, w_down, g, b, tm, tf=256):
    m = x.shape[0]
    pad = D_FF_PAD - D_FF
    wgu = jnp.concatenate([jnp.pad(w_gu[:, :D_FF], ((0, 0), (0, pad))),
                           jnp.pad(w_gu[:, D_FF:], ((0, 0), (0, pad)))], axis=1).astype(BF16)
    wd = jnp.pad(w_down, ((0, pad), (0, 0))).astype(BF16)
    return pl.pallas_call(
        functools.partial(_ffn_kernel, tf=tf),
        grid=(m // tm,),
        in_specs=[pl.BlockSpec((tm, D), lambda i: (i, 0)),
                  pl.BlockSpec((D, 2 * D_FF_PAD), lambda i: (0, 0), pipeline_mode=pl.Buffered(1)),
                  pl.BlockSpec((D_FF_PAD, D), lambda i: (0, 0), pipeline_mode=pl.Buffered(1)),
                  pl.BlockSpec((1, D), lambda i: (0, 0)),
                  pl.BlockSpec((1, D), lambda i: (0, 0))],
        out_specs=pl.BlockSpec((tm, D), lambda i: (i, 0)),
        out_shape=jax.ShapeDtypeStruct((m, D), F32),
        compiler_params=_cp(1),
        name="ffn_ln",
    )(x, wgu, wd, g.reshape(1, D), b.reshape(1, D))


def _memattn_kernel(q_ref, k_ref, v_ref, o_ref, *, head_rows):
    q = q_ref[...]
    for h in range(MEM_HEADS):
        sl = slice(h * LANES, (h + 1) * LANES)
        if head_rows:
            k = k_ref[pl.ds(h, MEM_TOKENS, stride=MEM_HEADS), :]
            v = v_ref[pl.ds(h, MEM_TOKENS, stride=MEM_HEADS), :]
        else:
            k = k_ref[:, sl]
            v = v_ref[:, sl]
        s = _dot_nt(q[:, sl], k) * LANES ** -0.5
        s = s - jnp.max(s, axis=-1, keepdims=True)
        p = jnp.exp(s)
        p = p / jnp.sum(p, axis=-1, keepdims=True)
        o_ref[:, sl] = _dot(p, v)


def mem_attention(u, q_col, mem_k, k_col, mem_v, v_col, tq, nseq, seqlen, row0=0, seq0=0, head_rows=False):
    nq = seqlen // tq
    off = row0 // tq
    if head_rows:
        kv_block = (None, MEM_TOKENS * MEM_HEADS, LANES)
    else:
        kv_block = (None, MEM_TOKENS, MEM_Q)
    kern = functools.partial(_memattn_kernel, head_rows=head_rows)
    in_specs = [pl.BlockSpec((tq, MEM_Q), lambda bi, i: (off + bi * nq + i, q_col)),
                pl.BlockSpec(kv_block, lambda bi, i: (bi + seq0, 0, k_col)),
                pl.BlockSpec(kv_block, lambda bi, i: (bi + seq0, 0, v_col))]
    return pl.pallas_call(
        kern,
        grid=(nseq, nq),
        in_specs=in_specs,
        out_specs=pl.BlockSpec((tq, MEM_Q), lambda bi, i: (bi * nq + i, 0)),
        out_shape=jax.ShapeDtypeStruct((nseq * seqlen, MEM_Q), F32),
        compiler_params=_cp(2),
        name="mem_attention",
    )(u, mem_k, mem_v)


def _mlstm_kernel(qk_ref, v_ref, og_ref, gc_ref, gt_ref, bgc_ref, bgr_ref, ng_ref, c0_ref, n0_ref, m0_ref,
                  mix_ref, co_ref, no_ref, mo_ref, c_scr, n_scr, m_scr, *, T, nc, n_valid, sps):
    s = pl.program_id(0)

    @pl.when(s % sps == 0)
    def _():
        c_scr[...] = c0_ref[...]
        n_scr[...] = n0_ref[...]
        m_scr[...] = m0_ref[...]

    row = lax.broadcasted_iota(jnp.int32, (T, T), 0)
    col = lax.broadcasted_iota(jnp.int32, (T, T), 1)
    causal = col <= row
    tril = causal.astype(F32)
    triu = (row <= col).astype(F32)

    cs = range(nc)
    hs = range(H)
    ch = [(c, h) for c in cs for h in hs]
    rows = [slice(c * T, (c + 1) * T) for c in cs]
    g, bcol, brow, igr_all = [], [], [], []
    for c in cs:
        g_c = gc_ref[rows[c], :] + bgc_ref[...]
        lfc = _log_sigmoid(g_c)
        gt = gt_ref[c] + bgr_ref[...]
        igr = gt[0:H, :]
        lfr = _log_sigmoid(gt[H:2 * H, :])
        if n_valid < T:
            rid = lax.broadcasted_iota(jnp.int32, (T, LANES), 0)
            g_c = jnp.where(rid < n_valid, g_c, NEG)
            lfc = jnp.where(rid < n_valid, lfc, 0.0)
            cid = lax.broadcasted_iota(jnp.int32, (H, T), 1)
            igr = jnp.where(cid < n_valid, igr, NEG)
            lfr = jnp.where(cid < n_valid, lfr, 0.0)
        g.append(g_c)
        igr_all.append(igr)
        bcol.append(_dot_hi(tril, lfc))
        brow.append(_dot_hi(lfr, triu))
    bc = {(c, h): bcol[c][:, H + h:H + h + 1] for c, h in ch}
    log_d = {(c, h): jnp.where(causal, bc[c, h] - brow[c][h:h + 1, :] + igr_all[c][h:h + 1, :], -jnp.inf)
             for c, h in ch}
    rmax = {(c, h): jnp.max(log_d[c, h], axis=-1, keepdims=True) for c, h in ch}
    m_all = m_scr[...]
    m_prev = [m_all[:, h:h + 1] for h in hs]
    inter, m_t = {}, {}
    for c, h in ch:
        inter[c, h] = bc[c, h] + m_prev[h]
        m_t[c, h] = jnp.maximum(inter[c, h], rmax[c, h])
        m_prev[h] = m_t[c, h][T - 1:T, :]
    qk = [qk_ref[rows[c], :] for c in cs]
    vv = [v_ref[rows[c], :] for c in cs]
    q = {(c, h): (qk[c][:, h * 64:(h + 1) * 64] * 0.125).astype(BF16) for c, h in ch}
    k = {(c, h): qk[c][:, 512 + h * 64:512 + (h + 1) * 64] for c, h in ch}
    kb = {(c, h): k[c, h].astype(BF16) for c, h in ch}
    v = {(c, h): vv[c][:, h * LANES:(h + 1) * LANES] for c, h in ch}
    qkt = {(c, h): lax.dot_general(q[c, h], kb[c, h], NT, preferred_element_type=F32) for c, h in ch}
    w_inter = {(c, h): jnp.exp(inter[c, h] - m_t[c, h]) for c, h in ch}
    sc = {(c, h): qkt[c, h] * jnp.exp(log_d[c, h] - m_t[c, h]) for c, h in ch}
    sv = {(c, h): jnp.dot(sc[c, h].astype(BF16), v[c, h].astype(BF16), preferred_element_type=F32) for c, h in ch}
    den0 = {(c, h): jnp.sum(sc[c, h], axis=-1, keepdims=True) for c, h in ch}
    w_end = {(c, h): jnp.exp(bc[c, h][T - 1:T, :] - bc[c, h] + g[c][:, h:h + 1] - m_t[c, h][T - 1:T, :])
             for c, h in ch}
    vk = {(c, h): lax.dot_general((w_end[c, h] * v[c, h]).astype(BF16), kb[c, h], TN, preferred_element_type=F32)
          for c, h in ch}
    kw = {(c, h): jnp.sum(w_end[c, h] * k[c, h], axis=0, keepdims=True) for c, h in ch}
    c_in = {(0, h): c_scr[h] for h in hs}
    n_in = {(0, h): n_scr[h:h + 1, :] for h in hs}
    for c, h in ch:
        decay = w_inter[c, h][T - 1:T, :]
        c_in[c + 1, h] = decay * c_in[c, h] + vk[c, h]
        n_in[c + 1, h] = decay * n_in[c, h] + kw[c, h]
    qc = {(c, h): lax.dot_general(q[c, h], c_in[c, h].astype(BF16), NT, preferred_element_type=F32) for c, h in ch}
    for c, h in ch:
        num = sv[c, h] + w_inter[c, h] * qc[c, h]
        n_b = n_in[c, h].astype(BF16).astype(F32)
        den = den0[c, h] + w_inter[c, h] * jnp.sum(q[c, h].astype(F32) * n_b, axis=-1, keepdims=True)
        hh = num / jnp.maximum(jnp.abs(den), jnp.exp(-m_t[c, h]))
        sl = slice(h * LANES, (h + 1) * LANES)
        mix_ref[rows[c], sl] = _sigmoid(og_ref[rows[c], sl]) * (_rms(hh) * ng_ref[:, sl])
    for h in hs:
        c_scr[h] = c_in[nc, h]
        n_scr[h:h + 1, :] = n_in[nc, h]
        m_scr[:, h:h + 1] = m_prev[h]

    @pl.when(s % sps == sps - 1)
    def _():
        co_ref[...] = c_scr[...]
        no_ref[...] = n_scr[...]
        mo_ref[...] = m_scr[...]


def mlstm(u, g_col, b_gate, norm_g, c0, n0, m0, nseq, seqlen, T, rows_per_step, n_valid, row0=0):
    m = nseq * seqlen
    nc = rows_per_step // T
    sps = seqlen // rows_per_step
    gates = u
    g0 = g_col * LANES
    gt = jnp.transpose(u[row0:row0 + m, g0:g0 + 2 * H].reshape(m // T, T, 2 * H), (0, 2, 1))
    bgc = jnp.zeros((1, LANES), F32).at[0, :2 * H].set(b_gate.reshape(-1))
    bgr = b_gate.reshape(2 * H, 1)
    r = rows_per_step
    off = row0 // r
    kern = functools.partial(_mlstm_kernel, T=T, nc=nc, n_valid=n_valid, sps=sps)
    in_specs = [pl.BlockSpec((r, D), lambda s: (s + off, 0)),
                pl.BlockSpec((r, D), lambda s: (s + off, 1)),
                pl.BlockSpec((r, D), lambda s: (s + off, 2)),
                pl.BlockSpec((r, LANES), lambda s: (s + off, g_col)),
                pl.BlockSpec((nc, 2 * H, T), lambda s: (s, 0, 0)),
                pl.BlockSpec((1, LANES), lambda s: (0, 0)),
                pl.BlockSpec((2 * H, 1), lambda s: (0, 0)),
                pl.BlockSpec((1, D), lambda s: (0, 0)),
                pl.BlockSpec((None, H, 128, 64), lambda s: (s // sps, 0, 0, 0)),
                pl.BlockSpec((None, H, 64), lambda s: (s // sps, 0, 0)),
                pl.BlockSpec((None, 1, H), lambda s: (s // sps, 0, 0))]
    args = [u, u, u, gates, gt, bgc, bgr, norm_g.reshape(1, D), c0, n0, m0.reshape(nseq, 1, H)]
    return pl.pallas_call(
        kern,
        grid=(m // r,),
        in_specs=in_specs,
        out_specs=[pl.BlockSpec((r, D), lambda s: (s, 0)),
                   pl.BlockSpec((None, H, 128, 64), lambda s: (s // sps, 0, 0, 0)),
                   pl.BlockSpec((None, H, 64), lambda s: (s // sps, 0, 0)),
                   pl.BlockSpec((None, 1, H), lambda s: (s // sps, 0, 0))],
        out_shape=[jax.ShapeDtypeStruct((m, D), F32),
                   jax.ShapeDtypeStruct((nseq, H, 128, 64), F32),
                   jax.ShapeDtypeStruct((nseq, H, 64), F32),
                   jax.ShapeDtypeStruct((nseq, 1, H), F32)],
        scratch_shapes=[pltpu.VMEM((H, 128, 64), F32), pltpu.VMEM((H, 64), F32), pltpu.VMEM((1, H), F32)],
        compiler_params=_cp(1),
        name="mlstm",
    )(*args)


def _gdn_kernel(x_ref, z_ref, gc_ref, gt_ref, cw_ref, pc_ref, pr_ref, ng_ref, s0_ref, cv0_ref,
                mix_ref, so_ref, xbuf, cv_scr, s_scr, *, T, nc, n_valid, sps):
    s = pl.program_id(0)
    R = T * nc

    @pl.when(s % sps == 0)
    def _():
        s_scr[...] = s0_ref[...]
        xbuf[5:8, :] = cv0_ref[...]

    xbuf[8:8 + R, :] = x_ref[...]
    conv = (cw_ref[0:1, :] * xbuf[5:5 + R, :] + cw_ref[1:2, :] * xbuf[6:6 + R, :]
            + cw_ref[2:3, :] * xbuf[7:7 + R, :] + cw_ref[3:4, :] * xbuf[8:8 + R, :])
    cv_scr[...] = _silu(conv)
    xbuf[0:8, :] = xbuf[R:R + 8, :]

    row = lax.broadcasted_iota(jnp.int32, (T, T), 0)
    col = lax.broadcasted_iota(jnp.int32, (T, T), 1)
    causal = col <= row
    strict = col < row
    tril = causal.astype(F32)
    triu = (row <= col).astype(F32)

    def chunk(c, carry):
        r0 = pl.multiple_of(c * T, T)
        rows = pl.ds(r0, T)
        gpre = gc_ref[rows, :]
        beta_all = _sigmoid(gpre)
        g_all = -jnp.exp(pc_ref[0:1, :]) * _softplus(gpre + pc_ref[1:2, :])
        gt = gt_ref[c]
        gr_all = -jnp.exp(pr_ref[:, 0:1]) * _softplus(gt + pr_ref[:, 1:2])
        gr = gr_all[H:2 * H, :]
        if n_valid < T:
            rid = lax.broadcasted_iota(jnp.int32, (T, LANES), 0)
            beta_all = jnp.where(rid < n_valid, beta_all, 0.0)
            g_all = jnp.where(rid < n_valid, g_all, 0.0)
            cid = lax.broadcasted_iota(jnp.int32, (H, T), 1)
            gr = jnp.where(cid < n_valid, gr, 0.0)
        gamc_all = _dot_hi(tril, g_all)
        gamr_all = _dot_hi(gr, triu)
        hs = range(H)
        qf = [cv_scr[rows, h * LANES:(h + 1) * LANES] for h in hs]
        kf = [cv_scr[rows, D + h * LANES:D + (h + 1) * LANES] for h in hs]
        vf = [cv_scr[rows, 2 * D + h * LANES:2 * D + (h + 1) * LANES] for h in hs]
        qb = [(qf[h] * lax.rsqrt(jnp.sum(qf[h] * qf[h], axis=-1, keepdims=True) + EPS) * LANES ** -0.5).astype(BF16)
              for h in hs]
        kn = [kf[h] * lax.rsqrt(jnp.sum(kf[h] * kf[h], axis=-1, keepdims=True) + EPS) for h in hs]
        kb = [kn[h].astype(BF16) for h in hs]
        beta = [beta_all[:, h:h + 1] for h in hs]
        gamc = [gamc_all[:, H + h:H + h + 1] for h in hs]
        egam = [jnp.exp(gamc[h]) for h in hs]
        dec = [jnp.exp(jnp.where(causal, gamc[h] - gamr_all[h:h + 1, :], NEG)) for h in hs]
        s_old = [s_scr[h] for h in hs]
        sb = [s_old[h].astype(BF16) for h in hs]
        qkk = [lax.dot_general(jnp.concatenate([qb[h], kb[h]], axis=0), kb[h], NT, preferred_element_type=F32)
               for h in hs]
        qs = [lax.dot_general(qb[h], sb[h], NT, preferred_element_type=F32) for h in hs]
        nmat = [jnp.where(strict, beta[h] * qkk[h][T:2 * T] * dec[h], 0.0) for h in hs]
        y = [-nmat[h] for h in hs]
        npow = nmat
        nsp = [_split_bf16(nmat[h]) for h in hs]
        span = 2
        while span < T:
            if T < 16:
                npow = [_dot_hi(npow[h], npow[h]) for h in hs]
                y = [y[h] + npow[h] + _dot_hi(y[h], npow[h]) for h in hs]
            else:
                npow = [_dot_split(nsp[h][0], nsp[h][1], nsp[h][0], nsp[h][1]) for h in hs]
                nsp = [_split_bf16(npow[h]) for h in hs]
                ysp = [_split_bf16(y[h]) for h in hs]
                y = [y[h] + npow[h] + _dot_split(ysp[h][0], ysp[h][1], nsp[h][0], nsp[h][1]) for h in hs]
            span *= 2
        rhs = [jnp.concatenate([beta[h] * vf[h], (beta[h] * egam[h]) * kn[h]], axis=1) for h in hs]
        if T < 16:
            sol = [rhs[h] + _dot_hi(y[h], rhs[h]) for h in hs]
        else:
            sol = [rhs[h] + _dot(y[h], rhs[h]) for h in hs]
        delta = [sol[h][:, 0:LANES] - _dot_nt(sol[h][:, LANES:2 * LANES], sb[h]) for h in hs]
        o = [egam[h] * qs[h] + _dot(qkk[h][0:T] * dec[h], delta[h]) for h in hs]
        g_end = [gamc[h][T - 1:T, :] for h in hs]
        upd = [_dot_tn(jnp.exp(g_end[h] - gamc[h]) * delta[h], kb[h]) for h in hs]
        for h in hs:
            sl = slice(h * LANES, (h + 1) * LANES)
            s_scr[h] = jnp.exp(g_end[h]) * s_old[h] + upd[h]
            mix_ref[rows, sl] = _rms(o[h]) * ng_ref[:, sl] * _silu(z_ref[rows, sl])
        return carry

    lax.fori_loop(0, nc, chunk, 0, unroll=min(nc, 2))

    @pl.when(s % sps == sps - 1)
    def _():
        so_ref[...] = s_scr[...]


def gdn(u, g_col, conv_w, a_log, dt_bias, norm_g, s0, conv0, nseq, seqlen, T, rows_per_step, n_valid, row0=0):
    m = nseq * seqlen
    nc = rows_per_step // T
    sps = seqlen // rows_per_step
    r = rows_per_step
    off = row0 // r
    gates = u
    g0 = g_col * LANES
    gt = jnp.transpose(u[row0:row0 + m, g0:g0 + 2 * H].reshape(m // T, T, 2 * H), (0, 2, 1))
    pc = jnp.zeros((2, LANES), F32).at[0, H:2 * H].set(a_log).at[1, H:2 * H].set(dt_bias)
    pr = jnp.zeros((2 * H, 2), F32).at[H:, 0].set(a_log).at[H:, 1].set(dt_bias)
    kern = functools.partial(_gdn_kernel, T=T, nc=nc, n_valid=n_valid, sps=sps)
    in_specs = [pl.BlockSpec((r, 3 * D), lambda s: (s + off, 0)),
                pl.BlockSpec((r, D), lambda s: (s + off, 3)),
                pl.BlockSpec((r, LANES), lambda s: (s + off, g_col)),
                pl.BlockSpec((nc, 2 * H, T), lambda s: (s, 0, 0)),
                pl.BlockSpec((4, 3 * D), lambda s: (0, 0)),
                pl.BlockSpec((2, LANES), lambda s: (0, 0)),
                pl.BlockSpec((2 * H, 2), lambda s: (0, 0)),
                pl.BlockSpec((1, D), lambda s: (0, 0)),
                pl.BlockSpec((None, H, 128, 128), lambda s: (s // sps, 0, 0, 0)),
                pl.BlockSpec((None, 3, 3 * D), lambda s: (s // sps, 0, 0))]
    args = [u, u, gates, gt, conv_w, pc, pr, norm_g.reshape(1, D), s0, conv0]
    return pl.pallas_call(
        kern,
        grid=(m // r,),
        in_specs=in_specs,
        out_specs=[pl.BlockSpec((r, D), lambda s: (s, 0)),
                   pl.BlockSpec((None, H, 128, 128), lambda s: (s // sps, 0, 0, 0))],
        out_shape=[jax.ShapeDtypeStruct((m, D), F32),
                   jax.ShapeDtypeStruct((nseq, H, 128, 128), F32)],
        scratch_shapes=[pltpu.VMEM((r + 8, 3 * D), F32), pltpu.VMEM((r, 3 * D), F32),
                        pltpu.VMEM((H, 128, 128), F32)],
        compiler_params=_cp(1),
        name="gdn",
    )(*args)


def _bucket(rel):
    n = jnp.maximum(rel, 0)
    nf = jnp.maximum(n, 1).astype(F32)
    large = 16 + (jnp.log(nf / 16) / math.log(128 / 16) * 16).astype(jnp.int32)
    return jnp.where(n < 16, n, jnp.minimum(large, 31))


def _diff_lambda(lw_ref, lam_init):
    lw = lw_ref[...]
    a = jnp.sum(lw[0:1, :] * lw[1:2, :], axis=-1, keepdims=True)
    b = jnp.sum(lw[2:3, :] * lw[3:4, :], axis=-1, keepdims=True)
    return jnp.exp(a) - jnp.exp(b) + lam_init


def _dattn_p_kernel(q_ref, k_ref, v_ref, bn_ref, lw_ref, ng_ref, o_ref, m_scr, acc_scr, *, TB, lam_init):
    i = pl.program_id(2)
    W = 2 * TB
    lane = lax.broadcasted_iota(jnp.int32, (TB, LANES), 1)
    q = q_ref[...] * 0.125
    qq = jnp.concatenate([jnp.where(lane < 64, q, 0.0), jnp.where(lane >= 64, q, 0.0)], axis=0).astype(BF16)
    m_scr[...] = jnp.full_like(m_scr, NEG)
    acc_scr[...] = jnp.zeros_like(acc_scr)

    def step(k0, width, bias):
        rows = pl.ds(pl.multiple_of(k0, TB), width)
        kb = k_ref[rows, :].astype(BF16)
        vext = jnp.concatenate([v_ref[rows, :].astype(BF16), jnp.ones((width, LANES), BF16)], axis=1)
        sc = lax.dot_general(qq, kb, NT, preferred_element_type=F32)
        if bias is not None:
            sc = sc + bias
        ng = width // LANES
        mloc = sc[:, 0:LANES]
        for g in range(1, ng):
            mloc = jnp.maximum(mloc, sc[:, g * LANES:(g + 1) * LANES])
        m_old = m_scr[...]
        m_new = jnp.maximum(m_old, jnp.max(mloc, axis=-1, keepdims=True))
        alpha = jnp.exp(m_old - m_new)
        p = jnp.exp(sc - jnp.concatenate([m_new] * ng, axis=1)).astype(BF16)
        acc_scr[...] = (jnp.concatenate([alpha, alpha], axis=1) * acc_scr[...]
                        + jnp.dot(p, vext, preferred_element_type=F32))
        m_scr[...] = m_new

    n_far = jnp.maximum(i - 1, 0)
    n_big = n_far // 4
    rem = n_far % 4

    def body(c, carry):
        step(c * 2 * W, W, None)
        step(c * 2 * W + W, W, None)
        return carry

    lax.fori_loop(0, n_big, body, 0)

    @pl.when(rem >= 2)
    def _():
        step(n_big * 2 * W, W, None)

    @pl.when(rem % 2 == 1)
    def _():
        step((n_far - 1) * TB, TB, None)

    step(n_far * TB, W, bn_ref[...])
    lam = _diff_lambda(lw_ref, lam_init)
    acc = acc_scr[...]
    o = (acc[0:TB, 0:LANES] / acc[0:TB, LANES:2 * LANES]
         - lam * (acc[TB:2 * TB, 0:LANES] / acc[TB:2 * TB, LANES:2 * LANES]))
    o_ref[...] = _rms(o) * ng_ref[...] * (1.0 - lam_init)


def _bias_table(rel_bias, rel):
    onehot = (_bucket(rel)[..., None] == jnp.arange(32)).astype(F32)
    return jnp.einsum('...k,kh->h...', onehot, rel_bias - rel_bias[31:32], precision=HI)


def diff_attn_prompt(u, lam_w, norm_g, rel_bias, lam_init, TB=256):
    nq = SEQ // TB
    ii = jnp.arange(TB)
    rel_d = ii[:, None] - ii[None, :]
    bd = jnp.where((rel_d >= 0)[None], _bias_table(rel_bias, rel_d), NEG)
    bs = _bias_table(rel_bias, rel_d + TB)
    masked = jnp.full((H, TB, TB), NEG, F32)
    bn = jnp.stack([jnp.concatenate([bd, masked], axis=2), jnp.concatenate([bs, bd], axis=2)], axis=1)
    bn = jnp.concatenate([bn, bn], axis=2)
    kern = functools.partial(_dattn_p_kernel, TB=TB, lam_init=lam_init)
    return pl.pallas_call(
        kern,
        grid=(BATCH, H, nq),
        in_specs=[pl.BlockSpec((TB, LANES), lambda b, h, i: (b * nq + i, h)),
                  pl.BlockSpec((SEQ, LANES), lambda b, h, i: (b, H + h)),
                  pl.BlockSpec((SEQ, LANES), lambda b, h, i: (b, 2 * H + h)),
                  pl.BlockSpec((None, None, 2 * TB, 2 * TB), lambda b, h, i: (h, jnp.minimum(i, 1), 0, 0)),
                  pl.BlockSpec((4, 64), lambda b, h, i: (0, 0)),
                  pl.BlockSpec((1, LANES), lambda b, h, i: (0, h))],
        out_specs=pl.BlockSpec((TB, LANES), lambda b, h, i: (b * nq + i, h)),
        out_shape=jax.ShapeDtypeStruct((BATCH * SEQ, D), F32),
        scratch_shapes=[pltpu.VMEM((2 * TB, LANES), F32), pltpu.VMEM((2 * TB, 2 * LANES), F32)],
        compiler_params=_cp(3),
        name="diff_attn_prompt",
    )(u, u, u, bn, lam_w, norm_g.reshape(1, D))


def _dattn_s_kernel(pt_ref, qm_ref, qb_ref, *refs, lam_init):
    del pt_ref
    kp_refs = refs[0:PAGES_PER_STEP]
    vp_refs = refs[PAGES_PER_STEP:2 * PAGES_PER_STEP]
    kn_ref, vn_ref, bl_ref, bn_ref, lw_ref, ng_ref, o_ref, m_scr, l_scr, acc_scr = refs[2 * PAGES_PER_STEP:]
    p = pl.program_id(1)
    last = pl.num_programs(1) - 1
    rows_h = 2 * DEC_PAD

    @pl.when(p == 0)
    def _():
        m_scr[...] = jnp.full_like(m_scr, NEG)
        l_scr[...] = jnp.zeros_like(l_scr)
        acc_scr[...] = jnp.zeros_like(acc_scr)

    def head_rows(page_refs, h):
        return jnp.concatenate([r[pl.ds(h, PAGE, stride=H), :].astype(BF16) for r in page_refs], axis=0)

    qm = qm_ref[...].astype(BF16)
    sc = jnp.concatenate(
        [lax.dot_general(qm[h * rows_h:(h + 1) * rows_h], head_rows(kp_refs, h), NT, preferred_element_type=F32)
         for h in range(H)], axis=0)
    sc = sc + jnp.where(p == last, bl_ref[...], 0.0)
    m_old = m_scr[...]
    m_new = jnp.maximum(m_old, jnp.max(sc, axis=-1, keepdims=True))
    alpha = jnp.exp(m_old - m_new)
    pr = jnp.exp(sc - jnp.concatenate([m_new] * PAGES_PER_STEP, axis=1))
    l_scr[...] = alpha * l_scr[...] + jnp.sum(pr, axis=-1, keepdims=True)
    prb = pr.astype(BF16)
    pv = jnp.concatenate(
        [jnp.dot(prb[h * rows_h:(h + 1) * rows_h], head_rows(vp_refs, h), preferred_element_type=F32)
         for h in range(H)], axis=0)
    acc_scr[...] = alpha * acc_scr[...] + pv
    m_scr[...] = m_new

    @pl.when(p == last)
    def _():
        scn = lax.dot_general(qb_ref[...], kn_ref[...], NT, preferred_element_type=F32) + bn_ref[...]
        m_o = m_scr[...]
        m_n = jnp.maximum(m_o, jnp.max(scn, axis=-1, keepdims=True))
        al = jnp.exp(m_o - m_n)
        pn = jnp.exp(scn - m_n[:, 0:DEC_PAD])
        l_n = al * l_scr[...] + jnp.sum(pn, axis=-1, keepdims=True)
        pvn = jnp.dot(pn, vn_ref[...], preferred_element_type=F32)
        lam = _diff_lambda(lw_ref, lam_init)
        for h in range(H):
            sl = slice(h * LANES, (h + 1) * LANES)
            r0 = slice(h * rows_h, h * rows_h + DEC_PAD)
            r1 = slice(h * rows_h + DEC_PAD, (h + 1) * rows_h)
            a0 = (al[r0] * acc_scr[r0, :] + pvn[r0, sl]) / l_n[r0]
            a1 = (al[r1] * acc_scr[r1, :] + pvn[r1, sl]) / l_n[r1]
            o_ref[:, sl] = _rms(a0 - lam * a1) * ng_ref[:, sl] * (1.0 - lam_init)


def diff_attn_sample(u, cache_k, cache_v, page_table, lam_w, norm_g, rel_bias, lam_init, row0):
    nb = DEC_BATCH
    nr = 2 * H * DEC_PAD
    off = row0 // DEC_PAD
    q = u[row0:row0 + nb * DEC_PAD, :D].reshape(nb, DEC_PAD, D) * 0.125
    grp_row = jnp.arange(2 * H)[:, None, None]
    grp_col = (jnp.arange(D) // 64)[None, None, :]
    qbig = jnp.where(grp_row == grp_col, q[:, None, :, :], 0.0).reshape(nb, nr, D)
    q4 = jnp.transpose(q.reshape(nb, DEC_PAD, H, 1, LANES), (0, 2, 3, 1, 4))
    comp = (jnp.arange(LANES) // 64)[None, None, None, None, :]
    qm = jnp.where(comp == jnp.arange(2)[None, None, :, None, None], q4, 0.0).reshape(nb, nr, LANES)
    qi = jnp.arange(DEC_PAD)
    rel_last = PAGE + qi[:, None] - jnp.arange(PAGE)[None, :]
    b_last = jnp.broadcast_to(_bias_table(rel_bias, rel_last)[:, None], (H, 2, DEC_PAD, PAGE)).reshape(nr, PAGE)
    b_last = jnp.pad(b_last, ((0, 0), ((PAGES_PER_STEP - 1) * PAGE, 0)))
    rel_new = qi[:, None] - qi[None, :]
    vis = (rel_new >= 0) & (qi[None, :] < DEC_SEQ)
    b_new = jnp.where(vis[None], _bias_table(rel_bias, rel_new), NEG)
    b_new = jnp.broadcast_to(b_new[:, None], (H, 2, DEC_PAD, DEC_PAD)).reshape(nr, DEC_PAD)
    kern = functools.partial(_dattn_s_kernel, lam_init=lam_init)
    def page_spec(slot):
        return pl.BlockSpec((None, PAGE * H, LANES), lambda b, p, pt: (pt[b, p * PAGES_PER_STEP + slot], 0, 0))

    pages = [page_spec(slot) for slot in range(PAGES_PER_STEP)]
    in_specs = [pl.BlockSpec((None, nr, LANES), lambda b, p, pt: (b, 0, 0)),
                pl.BlockSpec((None, nr, D), lambda b, p, pt: (b, 0, 0))] + pages + pages + [
                pl.BlockSpec((DEC_PAD, D), lambda b, p, pt: (b + off, 1)),
                pl.BlockSpec((DEC_PAD, D), lambda b, p, pt: (b + off, 2)),
                pl.BlockSpec((nr, PAGES_PER_STEP * PAGE), lambda b, p, pt: (0, 0)),
                pl.BlockSpec((nr, DEC_PAD), lambda b, p, pt: (0, 0)),
                pl.BlockSpec((4, 64), lambda b, p, pt: (0, 0)),
                pl.BlockSpec((1, D), lambda b, p, pt: (0, 0))]
    args = [qm, qbig, *([cache_k] * PAGES_PER_STEP), *([cache_v] * PAGES_PER_STEP), u, u, b_last, b_new,
            lam_w, norm_g.reshape(1, D)]
    grid_spec = pltpu.PrefetchScalarGridSpec(
        num_scalar_prefetch=1,
        grid=(nb, N_PAGES // PAGES_PER_STEP),
        in_specs=in_specs,
        out_specs=pl.BlockSpec((DEC_PAD, D), lambda b, p, pt: (b, 0)),
        scratch_shapes=[pltpu.VMEM((nr, LANES), F32), pltpu.VMEM((nr, LANES), F32), pltpu.VMEM((nr, LANES), F32)],
    )
    return pl.pallas_call(
        kern,
        grid_spec=grid_spec,
        out_shape=jax.ShapeDtypeStruct((nb * DEC_PAD, D), F32),
        compiler_params=_cp(2),
        name="diff_attn_sample",
    )(page_table, *args)


def _router_kernel(x_ref, w_ref, o_ref):
    logits = _dot(x_ref[...], w_ref[...])
    lane = lax.broadcasted_iota(jnp.int32, logits.shape, 1)
    logits = jnp.where(lane < N_EXPERTS, logits, -jnp.inf)
    m1 = jnp.max(logits, axis=-1, keepdims=True)
    i1 = jnp.min(jnp.where(logits == m1, lane, LANES), axis=-1, keepdims=True)
    rest = jnp.where(lane == i1, -jnp.inf, logits)
    m2 = jnp.max(rest, axis=-1, keepdims=True)
    i2 = jnp.min(jnp.where(rest == m2, lane, LANES), axis=-1, keepdims=True)
    e2 = jnp.exp(m2 - m1)
    g1 = 1.0 / (1.0 + e2)
    g2 = e2 / (1.0 + e2)
    o_ref[...] = jnp.where(lane == 0, i1.astype(F32),
                           jnp.where(lane == 1, i2.astype(F32),
                                     jnp.where(lane == 2, g1, jnp.where(lane == 3, g2, 0.0))))


def router(x, w_router, tm):
    m = x.shape[0]
    w = jnp.pad(w_router, ((0, 0), (0, LANES - N_EXPERTS)))
    return pl.pallas_call(
        _router_kernel,
        grid=(m // tm,),
        in_specs=[pl.BlockSpec((tm, D), lambda i: (i, 0)),
                  pl.BlockSpec((D, LANES), lambda i: (0, 0))],
        out_specs=pl.BlockSpec((tm, LANES), lambda i: (i, 0)),
        out_shape=jax.ShapeDtypeStruct((m, LANES), F32),
        compiler_params=_cp(1),
        name="router",
    )(x, w)


def _expert_kernel(be_ref, src_ref, x_hbm, wgu_ref, wd_ref, o_ref, xbuf, sem, *, tf, tm):
    del be_ref
    i = pl.program_id(0)
    nb = pl.num_programs(0)
    slot = i % 2
    nslot = 1 - slot
    nxt = jnp.minimum(i + 1, nb - 1)

    def row_copy(block, r, sl):
        return pltpu.make_async_copy(x_hbm.at[pl.ds(src_ref[block * tm + r], 1), :],
                                     xbuf.at[sl, pl.ds(r, 1), :], sem.at[sl])

    def wait_rows(sl):
        def drain(r, carry):
            row_copy(0, r, sl).wait()
            return carry
        lax.fori_loop(0, tm, drain, 0, unroll=8)

    @pl.when(i == 0)
    def _():
        def issue(r, carry):
            row_copy(0, r, 0).start()
            return carry
        lax.fori_loop(0, tm, issue, 0, unroll=8)

    wait_rows(slot)
    xb = xbuf[slot].astype(BF16)
    chunks = list(range(0, D_FF_EXPERT, tf))
    per = -(-tm // len(chunks))
    for ci, f0 in enumerate(chunks):
        gate = jnp.dot(xb, wgu_ref[:, f0:f0 + tf], preferred_element_type=F32)
        up = jnp.dot(xb, wgu_ref[:, D_FF_EXPERT + f0:D_FF_EXPERT + f0 + tf], preferred_element_type=F32)
        part = jnp.dot((_silu(gate) * up).astype(BF16), wd_ref[f0:f0 + tf, :], preferred_element_type=F32)
        if ci == 0:
            o_ref[...] = part
        else:
            o_ref[...] += part
        for r in range(ci * per, min((ci + 1) * per, tm)):
            row_copy(nxt, r, nslot).start()

    @pl.when(i == nb - 1)
    def _():
        wait_rows(nslot)


def expert_ffn(x, src, w_gu, w_down, layer, blk_e, tm, tf=512):
    n = src.shape[0]
    grid_spec = pltpu.PrefetchScalarGridSpec(
        num_scalar_prefetch=2,
        grid=(n // tm,),
        in_specs=[pl.BlockSpec(memory_space=pl.ANY),
                  pl.BlockSpec((None, None, D, 2 * D_FF_EXPERT), lambda i, be, sr: (layer, be[i], 0, 0),
                               pipeline_mode=pl.Buffered(1)),
                  pl.BlockSpec((None, None, D_FF_EXPERT, D), lambda i, be, sr: (layer, be[i], 0, 0),
                               pipeline_mode=pl.Buffered(1))],
        out_specs=pl.BlockSpec((tm, D), lambda i, be, sr: (i, 0)),
        scratch_shapes=[pltpu.VMEM((2, tm, D), F32), pltpu.SemaphoreType.DMA((2,))],
    )
    return pl.pallas_call(
        functools.partial(_expert_kernel, tf=tf, tm=tm),
        grid_spec=grid_spec,
        out_shape=jax.ShapeDtypeStruct((n, D), F32),
        compiler_params=_cp(1),
        name="expert_ffn",
    )(blk_e, src, x, w_gu, w_down)


def _combine_kernel(d0_ref, d1_ref, yb_hbm, x_ref, rt_ref, g_ref, b_ref, o_ref, buf0, buf1, sem, *, G):
    base = pl.program_id(0) * G

    def issue(r, carry):
        pltpu.make_async_copy(yb_hbm.at[pl.ds(d0_ref[base + r], 1), :], buf0.at[pl.ds(r, 1), :], sem).start()
        pltpu.make_async_copy(yb_hbm.at[pl.ds(d1_ref[base + r], 1), :], buf1.at[pl.ds(r, 1), :], sem).start()
        return carry

    lax.fori_loop(0, G, issue, 0, unroll=4)

    def drain(r, carry):
        pltpu.make_async_copy(yb_hbm.at[pl.ds(0, 1), :], buf0.at[pl.ds(r, 1), :], sem).wait()
        pltpu.make_async_copy(yb_hbm.at[pl.ds(0, 1), :], buf1.at[pl.ds(r, 1), :], sem).wait()
        return carry

    lax.fori_loop(0, G, drain, 0, unroll=4)
    rt = rt_ref[...]
    y = rt[:, 2:3] * buf0[...] + rt[:, 3:4] * buf1[...]
    o_ref[...] = _layer_norm(ALPHA * x_ref[...] + y, g_ref[...], b_ref[...])


def combine_ln(yb, x, route, d0, d1, g, b, G):
    m = x.shape[0]
    grid_spec = pltpu.PrefetchScalarGridSpec(
        num_scalar_prefetch=2,
        grid=(m // G,),
        in_specs=[pl.BlockSpec(memory_space=pl.ANY),
                  pl.BlockSpec((G, D), lambda i, a, c: (i, 0)),
                  pl.BlockSpec((G, LANES), lambda i, a, c: (i, 0)),
                  pl.BlockSpec((1, D), lambda i, a, c: (0, 0)),
                  pl.BlockSpec((1, D), lambda i, a, c: (0, 0))],
        out_specs=pl.BlockSpec((G, D), lambda i, a, c: (i, 0)),
        scratch_shapes=[pltpu.VMEM((G, D), F32), pltpu.VMEM((G, D), F32), pltpu.SemaphoreType.DMA(())],
    )
    return pl.pallas_call(
        functools.partial(_combine_kernel, G=G),
        grid_spec=grid_spec,
        out_shape=jax.ShapeDtypeStruct((m, D), F32),
        compiler_params=_cp(1),
        name="combine_ln",
    )(d0, d1, yb, x, route, g.reshape(1, D), b.reshape(1, D))


def moe_ln(x, w_router, w_gu, w_down, layer, g, b, tm=256):
    m = x.shape[0]
    route = router(x, w_router, 640)
    flat_e = route[:, :2].astype(jnp.int32).reshape(-1)
    onehot = (flat_e[:, None] == jnp.arange(N_EXPERTS)[None, :]).astype(jnp.int32)
    csum = jnp.cumsum(onehot, axis=0)
    counts = csum[-1]
    padded = (counts + tm - 1) // tm * tm
    p_end = jnp.cumsum(padded)
    dest = jnp.sum(onehot * ((p_end - padded)[None, :] + csum - 1), axis=1)
    n_blocks = -(-(2 * m + N_EXPERTS * (tm - 1)) // tm)
    n_rows = n_blocks * tm
    n_used = (p_end[-1] // tm).astype(jnp.int32).reshape(1)
    blk = jnp.minimum(jnp.arange(n_blocks), n_used[0] - 1) * tm
    blk_e = jnp.minimum(jnp.sum((blk[:, None] >= p_end[None, :]).astype(jnp.int32), axis=1), N_EXPERTS - 1)
    src = jnp.zeros((n_rows,), jnp.int32).at[dest].set(jnp.arange(2 * m, dtype=jnp.int32) // 2)
    yb = expert_ffn(x, src, w_gu, w_down, layer, blk_e, tm)
    d = dest.reshape(m, 2).astype(jnp.int32)
    return combine_ln(yb, x, route, d[:, 0], d[:, 1], g, b, 640)


def kernel(x_prompt, x_sample, mem_prompt, page_table, cache_diff_k, cache_diff_v, cache_mem_k, cache_mem_v,
           state_mlstm_C, state_mlstm_n, state_mlstm_m, state_gdn_S, state_gdn_conv, rel_bias,
           w_in_a, b_gate_a, norm_a, w_in_b, lambda_b, norm_b, w_in_c, conv_c, a_log_c, dt_bias_c, norm_c,
           w_mem_kv, w_out, ln_g, ln_b, w_ffn_gu, w_ffn_down, w_router, w_exp_gu, w_exp_down):
    mp = BATCH * SEQ
    ms = DEC_BATCH * DEC_PAD
    tm = ROW_TILE
    x = jnp.concatenate([x_prompt.reshape(mp, D),
                         jnp.pad(x_sample, ((0, 0), (0, DEC_PAD - DEC_SEQ), (0, 0))).reshape(ms, D)], axis=0)
    mem2 = mem_prompt.reshape(BATCH * MEM_TOKENS, D)
    n_phys = cache_diff_k.shape[1]
    w_exp_gu_b = w_exp_gu.astype(BF16)
    w_exp_down_b = w_exp_down.astype(BF16)
    cmk = cache_mem_k.reshape(DEPTH * DEC_BATCH, MEM_TOKENS * MEM_HEADS, LANES)
    cmv = cache_mem_v.reshape(DEPTH * DEC_BATCH, MEM_TOKENS * MEM_HEADS, LANES)

    def sample_rows(t):
        return t[mp:].reshape(DEC_BATCH, DEC_PAD, -1)[:, :DEC_SEQ]

    def pad_gates(w):
        return jnp.pad(w, ((0, 0), (0, LANES - w.shape[1])))

    mlstm_p, mlstm_s, diff_p, diff_s, gdn_p, gdn_s, mem_k_new, mem_v_new = [], [], [], [], [], [], [], []
    for i in range(DEPTH):
        kind, j = i % 3, i // 3
        mem_kv = matmul(mem2, w_mem_kv[i], MEM_TOKENS, 512).reshape(BATCH, MEM_TOKENS, 2 * MEM_Q)
        mem_k_new.append(mem_kv[:, :, :MEM_Q].reshape(BATCH, MEM_TOKENS, MEM_HEADS, 128))
        mem_v_new.append(mem_kv[:, :, MEM_Q:].reshape(BATCH, MEM_TOKENS, MEM_HEADS, 128))
        if kind == 0:
            w = w_in_a[j]
            w_main = jnp.concatenate([w[:, :3 * D], w[:, 3 * D + 2 * H:], pad_gates(w[:, 3 * D:3 * D + 2 * H])], axis=1)
            u = matmul(x, w_main, tm, 512)
            gts = (3 * D + MEM_Q) // LANES
            zc = jnp.zeros((BATCH, H, 128, 64), F32)
            zn = jnp.zeros((BATCH, H, 64), F32)
            zm = jnp.zeros((BATCH, H), F32)
            mix_p, c_p, n_p, m_p = mlstm(u, gts, b_gate_a[j], norm_a[j], zc, zn, zm, BATCH, SEQ, CHUNK, 256, CHUNK)
            mix_s, c_s, n_s, m_s = mlstm(u, gts, b_gate_a[j], norm_a[j], state_mlstm_C[j], state_mlstm_n[j],
                                         state_mlstm_m[j], DEC_BATCH, DEC_PAD, DEC_PAD, DEC_PAD, DEC_SEQ, row0=mp)
            mlstm_p.append((c_p, n_p, m_p.reshape(BATCH, H)))
            mlstm_s.append((c_s, n_s, m_s.reshape(DEC_BATCH, H)))
            mq_col = 6
        elif kind == 1:
            lam_init = 0.8 - 0.6 * math.exp(-0.3 * i)
            u = matmul(x, w_in_b[j], tm, 512)
            mix_p = diff_attn_prompt(u, lambda_b[j], norm_b[j], rel_bias, lam_init)
            mix_s = diff_attn_sample(u, cache_diff_k[j].reshape(n_phys, PAGE * H, LANES),
                                     cache_diff_v[j].reshape(n_phys, PAGE * H, LANES), page_table,
                                     lambda_b[j], norm_b[j], rel_bias, lam_init, row0=mp)
            k_rows, v_rows = kv_head_rows(u, mp, 256)
            diff_p.append((k_rows.reshape(BATCH, SEQ, H, 128), v_rows.reshape(BATCH, SEQ, H, 128)))
            u_s = sample_rows(u)
            diff_s.append((u_s[:, :, D:2 * D].reshape(DEC_BATCH, DEC_SEQ, H, 128),
                           u_s[:, :, 2 * D:3 * D].reshape(DEC_BATCH, DEC_SEQ, H, 128)))
            mq_col = 6
        else:
            w = w_in_c[j]
            w_main = jnp.concatenate([w[:, :4 * D], w[:, 4 * D + 2 * H:], pad_gates(w[:, 4 * D:4 * D + 2 * H])], axis=1)
            u = matmul(x, w_main, tm, 512)
            gts = (4 * D + MEM_Q) // LANES
            zs = jnp.zeros((BATCH, H, 128, 128), F32)
            zv = jnp.zeros((BATCH, 3, 3 * D), F32)
            mix_p, s_p = gdn(u, gts, conv_c[j], a_log_c[j], dt_bias_c[j], norm_c[j], zs, zv,
                             BATCH, SEQ, CHUNK, 256, CHUNK)
            mix_s, s_s = gdn(u, gts, conv_c[j], a_log_c[j], dt_bias_c[j], norm_c[j], state_gdn_S[j],
                             state_gdn_conv[j], DEC_BATCH, DEC_PAD, DEC_PAD, DEC_PAD, DEC_SEQ, row0=mp)
            conv_p = jnp.stack([u[(b + 1) * SEQ - 3:(b + 1) * SEQ, :3 * D] for b in range(BATCH)])
            conv_s = sample_rows(u)[:, DEC_SEQ - 3:, :3 * D]
            gdn_p.append((s_p, conv_p))
            gdn_s.append((s_s, conv_s))
            mq_col = 8
        mo_s = mem_attention(u, mq_col, cmk, 0, cmv, 0, DEC_PAD, DEC_BATCH, DEC_PAD, row0=mp, seq0=i * DEC_BATCH,
                             head_rows=True)
        x = outproj_ln(mix_p, u, mq_col, mem_kv, mix_s, mo_s, x, w_out[i], ln_g[i, 0], ln_b[i, 0])
        k_f = i // 2
        if i % 2 == 0:
            x = ffn_ln(x, w_ffn_gu[k_f], w_ffn_down[k_f], ln_g[i, 1], ln_b[i, 1], tm)
        else:
            x = moe_ln(x, w_router[k_f], w_exp_gu_b, w_exp_down_b, k_f, ln_g[i, 1], ln_b[i, 1])

    def stack(lst, k):
        return jnp.stack([t[k] for t in lst])

    return (x[:mp].reshape(BATCH, SEQ, D), sample_rows(x),
            stack(mlstm_p, 0), stack(mlstm_p, 1), stack(mlstm_p, 2),
            stack(mlstm_s, 0), stack(mlstm_s, 1), stack(mlstm_s, 2),
            stack(diff_p, 0), stack(diff_p, 1), stack(diff_s, 0), stack(diff_s, 1),
            stack(gdn_p, 0), stack(gdn_p, 1), stack(gdn_s, 0), stack(gdn_s, 1),
            jnp.stack(mem_k_new), jnp.stack(mem_v_new))
```
